```python
import math
import jax, jax.numpy as jnp
from jax import lax
import numpy as np

D_MODEL = 1024
BATCH = 8
SEQ = 2048
DEPTH = 2
DEC_BATCH = 128
DEC_SEQ = 8
PAST_LEN = 16384
PAGE_SIZE = 128

N_MIXERS = 2
N_POOL_LAYERS = (DEPTH + 1) // 2
N_SSM_LAYERS = DEPTH // 2
POOL_WINDOWS = (2, 4, 8, 16)
POOL_GROUPS = len(POOL_WINDOWS)
POOL_GC = D_MODEL // POOL_GROUPS
POOL_MAX = 16
POOL_STATE = POOL_MAX - 1
SSM_GC = 16
SSM_GROUPS = D_MODEL // SSM_GC
SSM_P = 64
FFN_HIDDEN = 2816
FFN_UP = 2 * FFN_HIDDEN
CONV_W = 3
EPS = 1e-6

kernel_name = "hybrid_pool_s5_convffn_step"


def rmsnorm(x, g):
    xf = x.astype(jnp.float32)
    return xf * lax.rsqrt(jnp.mean(xf * xf, axis=-1, keepdims=True) + EPS) * g.astype(jnp.float32)


def ada_mod(c, w, b):
    m = jax.nn.silu(c.astype(jnp.float32)) @ w.astype(jnp.float32) + b.astype(jnp.float32)
    return jnp.split(m, 6, axis=-1)


def pre_mod(x, g, shift, scale):
    return (rmsnorm(x, g) * (1.0 + scale[:, None, :]) + shift[:, None, :]).astype(x.dtype)


def gated_residual(x, m, g, gate):
    return x + (gate[:, None, :] * rmsnorm(m, g)).astype(x.dtype)


def pool_mix(h_ext, n_prev, pos0, w_pool, pool_scale):
    Bsz, Le, D = h_ext.shape
    L = Le - n_prev
    hf = h_ext.astype(jnp.float32)
    cs = jnp.cumsum(jnp.pad(hf, ((0, 0), (POOL_MAX, 0), (0, 0))), axis=1)
    pos = pos0 + jnp.arange(n_prev, Le)
    parts = []
    for g, w in enumerate(POOL_WINDOWS):
        sl = slice(g * POOL_GC, (g + 1) * POOL_GC)
        hi = cs[:, POOL_MAX + n_prev:POOL_MAX + Le, sl]
        lo = cs[:, POOL_MAX + n_prev - w:POOL_MAX + Le - w, sl]
        cnt = jnp.minimum(pos + 1, w).astype(jnp.float32)[None, :, None]
        parts.append((hi - lo) / cnt - hf[:, n_prev:, sl])
    pooled = jnp.stack(parts, axis=2)
    y = jnp.einsum('blgc,gcd->blgd', pooled, w_pool.astype(jnp.float32)).reshape(Bsz, L, D)
    return y * pool_scale.astype(jnp.float32)


def _cmul_combine(e1, e2):
    a1r, a1i, b1r, b1i = e1
    a2r, a2i, b2r, b2i = e2
    ar = a2r * a1r - a2i * a1i
    ai = a2r * a1i + a2i * a1r
    br = a2r * b1r - a2i * b1i + b2r
    bi = a2r * b1i + a2i * b1r + b2i
    return (ar, ai, br, bi)


def s5_mix(u, x0_re, x0_im, A_re, A_im, log_dt, B_re, B_im, C_re, C_im, D_skip, w_glu_a, w_glu_b):
    f32 = jnp.float32
    Bsz, L, D = u.shape
    uf = u.astype(f32).reshape(Bsz, L, SSM_GROUPS, SSM_GC)
    A_re = A_re.astype(f32); A_im = A_im.astype(f32)
    dt = jnp.exp(log_dt.astype(f32))[:, None]
    mag = jnp.exp(A_re * dt); ang = A_im * dt
    lb_re = mag * jnp.cos(ang); lb_im = mag * jnp.sin(ang)
    n_re = lb_re - 1.0; n_im = lb_im
    den = A_re * A_re + A_im * A_im
    f_re = (n_re * A_re + n_im * A_im) / den
    f_im = (n_im * A_re - n_re * A_im) / den
    B_re = B_re.astype(f32); B_im = B_im.astype(f32)
    Bb_re = f_re[..., None] * B_re - f_im[..., None] * B_im
    Bb_im = f_re[..., None] * B_im + f_im[..., None] * B_re
    bu_re = jnp.einsum('blgc,gpc->blgp', uf, Bb_re)
    bu_im = jnp.einsum('blgc,gpc->blgp', uf, Bb_im)
    x0_re = x0_re.astype(f32); x0_im = x0_im.astype(f32)
    bu_re = bu_re.at[:, 0].add(lb_re * x0_re - lb_im * x0_im)
    bu_im = bu_im.at[:, 0].add(lb_re * x0_im + lb_im * x0_re)
    a_re = jnp.broadcast_to(lb_re, bu_re.shape)
    a_im = jnp.broadcast_to(lb_im, bu_im.shape)
    _, _, s_re, s_im = lax.associative_scan(_cmul_combine, (a_re, a_im, bu_re, bu_im), axis=1)
    y = (jnp.einsum('blgp,gcp->blgc', s_re, C_re.astype(f32))
         - jnp.einsum('blgp,gcp->blgc', s_im, C_im.astype(f32))
         + D_skip.astype(f32).reshape(SSM_GROUPS, SSM_GC) * uf).reshape(Bsz, L, D)
    g = jax.nn.gelu(y, approximate=False)
    out = (g @ w_glu_a.astype(f32)) * jax.nn.sigmoid(g @ w_glu_b.astype(f32))
    return out, s_re[:, -1], s_im[:, -1]


def conv_ffn(h, conv_prev, w_up, conv_w, conv_b, w_down):
    L = h.shape[1]
    up = h @ w_up
    ext = jnp.concatenate([conv_prev.astype(up.dtype), up], axis=1)
    conv = conv_b
    for k in range(CONV_W):
        conv = conv + ext[:, k:k + L] * conv_w[k]
    gate, val = jnp.split(conv, 2, axis=-1)
    out = (jax.nn.gelu(gate, approximate=False) * val) @ w_down
    return out, ext[:, -(CONV_W - 1):]


def setup_inputs(seed: int = 0) -> dict:
    key = jax.random.key(seed)
    ks = jax.random.split(key, 32)
    nrm = jax.random.normal
    f32 = jnp.float32
    D = D_MODEL
    d = {}
    d["x_prompt"] = nrm(ks[0], (BATCH, SEQ, D), f32)
    d["x_sample"] = nrm(ks[1], (DEC_BATCH, DEC_SEQ, D), f32)
    d["c_prompt"] = nrm(ks[2], (BATCH, D), f32)
    d["c_sample"] = nrm(ks[3], (DEC_BATCH, D), f32)
    d["state_pool"] = nrm(ks[4], (N_POOL_LAYERS, DEC_BATCH, POOL_STATE, D), f32)
    d["state_ssm_re"] = 0.1 * nrm(ks[5], (N_SSM_LAYERS, DEC_BATCH, SSM_GROUPS, SSM_P), f32)
    d["state_ssm_im"] = 0.1 * nrm(ks[6], (N_SSM_LAYERS, DEC_BATCH, SSM_GROUPS, SSM_P), f32)
    d["state_ffn_conv"] = 0.5 * nrm(ks[7], (DEPTH, DEC_BATCH, CONV_W - 1, FFN_UP), f32)
    d["ada_w"] = nrm(ks[8], (DEPTH, D, 6 * D), f32) * (0.5 * D ** -0.5)
    d["ada_b"] = 0.02 * nrm(ks[9], (DEPTH, 6 * D), f32)
    d["mix_pre_g"] = 1.0 + 0.02 * nrm(ks[10], (DEPTH, D), f32)
    d["mix_post_g"] = 1.0 + 0.02 * nrm(ks[11], (DEPTH, D), f32)
    d["ffn_pre_g"] = 1.0 + 0.02 * nrm(ks[12], (DEPTH, D), f32)
    d["ffn_post_g"] = 1.0 + 0.02 * nrm(ks[13], (DEPTH, D), f32)
    d["pool_w"] = nrm(ks[14], (N_POOL_LAYERS, POOL_GROUPS, POOL_GC, POOL_GC), f32) * POOL_GC ** -0.5
    d["pool_scale"] = 1.0 + 0.02 * nrm(ks[15], (N_POOL_LAYERS, D), f32)
    d["ssm_A_re"] = -0.5 + 0.01 * nrm(ks[16], (N_SSM_LAYERS, SSM_GROUPS, SSM_P), f32)
    d["ssm_A_im"] = math.pi * jnp.arange(SSM_P, dtype=f32) + 0.01 * nrm(ks[17], (N_SSM_LAYERS, SSM_GROUPS, SSM_P), f32)
    d["ssm_log_dt"] = jax.random.uniform(ks[18], (N_SSM_LAYERS, SSM_GROUPS), f32, math.log(1e-3), math.log(1e-1))
    d["ssm_B_re"] = nrm(ks[19], (N_SSM_LAYERS, SSM_GROUPS, SSM_P, SSM_GC), f32) * (2 * SSM_GC) ** -0.5
    d["ssm_B_im"] = nrm(ks[20], (N_SSM_LAYERS, SSM_GROUPS, SSM_P, SSM_GC), f32) * (2 * SSM_GC) ** -0.5
    d["ssm_C_re"] = nrm(ks[21], (N_SSM_LAYERS, SSM_GROUPS, SSM_GC, SSM_P), f32) * SSM_P ** -0.5
    d["ssm_C_im"] = nrm(ks[22], (N_SSM_LAYERS, SSM_GROUPS, SSM_GC, SSM_P), f32) * SSM_P ** -0.5
    d["ssm_D"] = nrm(ks[23], (N_SSM_LAYERS, D), f32)
    d["ssm_glu_a"] = nrm(ks[24], (N_SSM_LAYERS, D, D), f32) * D ** -0.5
    d["ssm_glu_b"] = nrm(ks[25], (N_SSM_LAYERS, D, D), f32) * D ** -0.5
    d["ffn_w_up"] = nrm(ks[26], (DEPTH, D, FFN_UP), f32) * D ** -0.5
    d["ffn_conv_w"] = nrm(ks[27], (DEPTH, CONV_W, FFN_UP), f32) * CONV_W ** -0.5
    d["ffn_conv_b"] = 0.02 * nrm(ks[28], (DEPTH, FFN_UP), f32)
    d["ffn_w_down"] = nrm(ks[29], (DEPTH, FFN_HIDDEN, D), f32) * FFN_HIDDEN ** -0.5
    return d


def reference(x_prompt, x_sample, c_prompt, c_sample, state_pool, state_ssm_re, state_ssm_im,
              state_ffn_conv, ada_w, ada_b, mix_pre_g, mix_post_g, ffn_pre_g, ffn_post_g,
              pool_w, pool_scale, ssm_A_re, ssm_A_im, ssm_log_dt, ssm_B_re, ssm_B_im,
              ssm_C_re, ssm_C_im, ssm_D, ssm_glu_a, ssm_glu_b,
              ffn_w_up, ffn_conv_w, ffn_conv_b, ffn_w_down):
    yp, ys = x_prompt, x_sample
    pool_p, pool_s, sre_p, sim_p, sre_s, sim_s, conv_p, conv_s = [], [], [], [], [], [], [], []
    for l in range(DEPTH):
        sh1p, sc1p, g1p, sh2p, sc2p, g2p = ada_mod(c_prompt, ada_w[l], ada_b[l])
        sh1s, sc1s, g1s, sh2s, sc2s, g2s = ada_mod(c_sample, ada_w[l], ada_b[l])
        hp = pre_mod(yp, mix_pre_g[l], sh1p, sc1p)
        hs = pre_mod(ys, mix_pre_g[l], sh1s, sc1s)
        j = l // N_MIXERS
        if l % N_MIXERS == 0:
            mp = pool_mix(hp, 0, 0, pool_w[j], pool_scale[j])
            hs_ext = jnp.concatenate([state_pool[j].astype(hs.dtype), hs], axis=1)
            ms = pool_mix(hs_ext, POOL_STATE, PAST_LEN - POOL_STATE, pool_w[j], pool_scale[j])
            pool_p.append(hp[:, -POOL_STATE:])
            pool_s.append(hs_ext[:, -POOL_STATE:])
        else:
            z0 = jnp.zeros((hp.shape[0], SSM_GROUPS, SSM_P), jnp.float32)
            mp, lr_p, li_p = s5_mix(hp, z0, z0, ssm_A_re[j], ssm_A_im[j], ssm_log_dt[j], ssm_B_re[j],
                                    ssm_B_im[j], ssm_C_re[j], ssm_C_im[j], ssm_D[j], ssm_glu_a[j], ssm_glu_b[j])
            ms, lr_s, li_s = s5_mix(hs, state_ssm_re[j], state_ssm_im[j], ssm_A_re[j], ssm_A_im[j], ssm_log_dt[j],
                                    ssm_B_re[j], ssm_B_im[j], ssm_C_re[j], ssm_C_im[j], ssm_D[j],
                                    ssm_glu_a[j], ssm_glu_b[j])
            sre_p.append(lr_p.astype(state_ssm_re.dtype)); sim_p.append(li_p.astype(state_ssm_im.dtype))
            sre_s.append(lr_s.astype(state_ssm_re.dtype)); sim_s.append(li_s.astype(state_ssm_im.dtype))
        yp = gated_residual(yp, mp, mix_post_g[l], g1p)
        ys = gated_residual(ys, ms, mix_post_g[l], g1s)
        fp = pre_mod(yp, ffn_pre_g[l], sh2p, sc2p)
        fs = pre_mod(ys, ffn_pre_g[l], sh2s, sc2s)
        zc = jnp.zeros((fp.shape[0], CONV_W - 1, FFN_UP), fp.dtype)
        op, cp = conv_ffn(fp, zc, ffn_w_up[l], ffn_conv_w[l], ffn_conv_b[l], ffn_w_down[l])
        os_, cs_ = conv_ffn(fs, state_ffn_conv[l], ffn_w_up[l], ffn_conv_w[l], ffn_conv_b[l], ffn_w_down[l])
        conv_p.append(cp); conv_s.append(cs_)
        yp = gated_residual(yp, op, ffn_post_g[l], g2p)
        ys = gated_residual(ys, os_, ffn_post_g[l], g2s)
    new_pool_prompt = jnp.stack(pool_p, axis=0)
    new_pool_sample = jnp.stack(pool_s, axis=0)
    new_ssm_re_prompt = jnp.stack(sre_p, axis=0)
    new_ssm_im_prompt = jnp.stack(sim_p, axis=0)
    new_ssm_re_sample = jnp.stack(sre_s, axis=0)
    new_ssm_im_sample = jnp.stack(sim_s, axis=0)
    new_conv_prompt = jnp.stack(conv_p, axis=0)
    new_conv_sample = jnp.stack(conv_s, axis=0)
    return (yp, ys, new_pool_prompt, new_pool_sample, new_ssm_re_prompt, new_ssm_im_prompt,
            new_ssm_re_sample, new_ssm_im_sample, new_conv_prompt, new_conv_sample)
```

```python
import functools
import math

import jax
import jax.numpy as jnp
from jax import lax
from jax.experimental import pallas as pl
from jax.experimental.pallas import tpu as pltpu

POOL_WINDOWS = (2, 4, 8, 16)
POOL_STATE = max(POOL_WINDOWS) - 1
SSM_GC = 16
SSM_P = 64
CONV_W = 3
EPS = 1e-6
PAST_LEN = 16384

V7X_SUBLANES = 8
V7X_MXU_DIM = 256
VMEM_LIMIT_BYTES = 56 * 1024 * 1024

_F32 = jnp.float32
_BF16 = jnp.bfloat16


def _const_spec(shape):
    nd = len(shape)
    return pl.BlockSpec(shape, lambda i: (0,) * nd, pipeline_mode=pl.Buffered(1))


def _rmsnorm(x, g):
    return x * lax.rsqrt(jnp.mean(x * x, axis=-1, keepdims=True) + EPS) * g


def _pre_mod(x, g, scale, shift, bm):
    t, d = x.shape
    r = _rmsnorm(x, g).reshape(t // bm, bm, d)
    return (r * (1.0 + scale)[None] + shift[None]).reshape(t, d)


def _gated_residual(x, m, g, gate, bm):
    t, d = x.shape
    r = _rmsnorm(m, g).reshape(t // bm, bm, d)
    return x + (gate[None] * r).reshape(t, d)


def _gelu(x):
    return 0.5 * x * (1.0 + lax.erf(x * math.sqrt(0.5)))


def _dot(a, b):
    return jnp.dot(a, b, preferred_element_type=_F32)


def _ada_kernel(cp_ref, cs_ref, w_ref, b_ref, mp_ref, ms_ref):
    w = w_ref[...].astype(_BF16)
    b = b_ref[...]
    for c_ref, o_ref in ((cp_ref, mp_ref), (cs_ref, ms_ref)):
        c = c_ref[...]
        s = (c * jax.nn.sigmoid(c)).astype(_BF16)
        o_ref[...] = _dot(s, w) + b


def _ada_mod(c_prompt, c_sample, ada_w, ada_b):
    depth, d, n = ada_w.shape
    tn = 1536
    bp, bs = c_prompt.shape[0], c_sample.shape[0]
    return pl.pallas_call(
        _ada_kernel,
        grid=(depth, n // tn),
        in_specs=[
            pl.BlockSpec((bp, d), lambda l, j: (0, 0)),
            pl.BlockSpec((bs, d), lambda l, j: (0, 0)),
            pl.BlockSpec((None, d, tn), lambda l, j: (l, 0, j)),
            pl.BlockSpec((None, 1, tn), lambda l, j: (l, 0, j)),
        ],
        out_specs=[
            pl.BlockSpec((None, bp, tn), lambda l, j: (l, 0, j)),
            pl.BlockSpec((None, bs, tn), lambda l, j: (l, 0, j)),
        ],
        out_shape=[
            jax.ShapeDtypeStruct((depth, bp, n), _F32),
            jax.ShapeDtypeStruct((depth, bs, n), _F32),
        ],
        compiler_params=pltpu.CompilerParams(
            dimension_semantics=("arbitrary", "arbitrary"),
            vmem_limit_bytes=VMEM_LIMIT_BYTES),
        name="ada_mod",
    )(c_prompt, c_sample, ada_w, ada_b.reshape(depth, 1, n))


def _pool_kernel(x_ref, mod_ref, st0_ref, gpre_ref, gpost_ref, pw_ref, ps_ref,
                 y_ref, st_ref, *, bm, pos_base):
    i = pl.program_id(0)
    t, d = x_ref.shape
    tt = t // bm
    gc = d // len(POOL_WINDOWS)

    @pl.when(i == 0)
    def _():
        st_ref[...] = st0_ref[...]

    x = x_ref[...]
    shift, scale, gate = mod_ref[:, 0:d], mod_ref[:, d:2 * d], mod_ref[:, 2 * d:3 * d]
    h = _pre_mod(x, gpre_ref[...], scale, shift, bm)
    ext = jnp.concatenate([st_ref[...], h], axis=0)
    st_ref[...] = ext[t:, :]

    parts = []
    for gi, w in enumerate(POOL_WINDOWS):
        cols = slice(gi * gc, (gi + 1) * gc)
        s = ext[:, cols]
        span = 1
        while span < w:
            n = s.shape[0]
            s = s[span * bm:] + s[:n - span * bm]
            span *= 2
        k0 = (POOL_STATE - (w - 1)) * bm
        wsum = s[k0:k0 + t]
        if pos_base + 1 >= w:
            cnt = float(w)
        else:
            row = lax.broadcasted_iota(jnp.int32, (t, gc), 0)
            pos = pos_base + i * tt + lax.shift_right_logical(row, bm.bit_length() - 1)
            cnt = jnp.minimum(pos + 1, w).astype(_F32)
        pooled = wsum / cnt - h[:, cols]
        parts.append(_dot(pooled.astype(_BF16), pw_ref[gi]))
    m = jnp.concatenate(parts, axis=1) * ps_ref[...]
    y_ref[...] = _gated_residual(x, m, gpost_ref[...], gate, bm)


def _pool_layer(x, mod, st0, g_pre, g_post, pool_w, pool_scale, *, bm, tt, pos_base):
    r, d = x.shape
    t = tt * bm
    ns = st0.shape[0]
    return pl.pallas_call(
        functools.partial(_pool_kernel, bm=bm, pos_base=pos_base),
        grid=(r // t,),
        in_specs=[
            pl.BlockSpec((t, d), lambda i: (i, 0)),
            _const_spec(mod.shape),
            _const_spec((ns, d)),
            _const_spec((1, d)),
            _const_spec((1, d)),
            _const_spec(pool_w.shape),
            _const_spec((1, d)),
        ],
        out_specs=[
            pl.BlockSpec((t, d), lambda i: (i, 0)),
            pl.BlockSpec((ns, d), lambda i: (0, 0)),
        ],
        out_shape=[
            jax.ShapeDtypeStruct((r, d), _F32),
            jax.ShapeDtypeStruct((ns, d), _F32),
        ],
        compiler_params=pltpu.CompilerParams(
            dimension_semantics=("arbitrary",), vmem_limit_bytes=VMEM_LIMIT_BYTES),
        name="pool_layer",
    )(x, mod, st0, g_pre, g_post, pool_w, pool_scale)


def _ffn_kernel(x_ref, mod_ref, cv0_ref, gpre_ref, gpost_ref, wup_ref, cw_ref, cb_ref, wdn_ref,
                y_ref, cv_ref, h_ref, *, bm):
    i = pl.program_id(0)
    t, d = x_ref.shape
    hid = wdn_ref.shape[0]
    hc = V7X_MXU_DIM

    @pl.when(i == 0)
    def _():
        cv_ref[...] = cv0_ref[...]

    x = x_ref[...]
    shift, scale, gate = mod_ref[:, 3 * d:4 * d], mod_ref[:, 4 * d:5 * d], mod_ref[:, 5 * d:6 * d]
    f = _pre_mod(x, gpre_ref[...], scale, shift, bm).astype(_BF16)

    def conv_cols(c0):
        cols = slice(c0, c0 + hc)
        up = _dot(f, wup_ref[:, cols])
        ext = jnp.concatenate([cv_ref[:, cols], up], axis=0)
        cv_ref[:, cols] = ext[t:, :]
        conv = cb_ref[:, cols]
        for k in range(CONV_W):
            conv = conv + ext[k * bm:k * bm + t, :] * cw_ref[k:k + 1, cols]
        return conv

    for c in range(hid // hc):
        gate_c = conv_cols(c * hc)
        val_c = conv_cols(hid + c * hc)
        h_ref[:, c * hc:(c + 1) * hc] = (_gelu(gate_c) * val_c).astype(_BF16)
    o = _dot(h_ref[...], wdn_ref[...])
    y_ref[...] = _gated_residual(x, o, gpost_ref[...], gate, bm)


def _ffn_layer(x, mod, cv0, g_pre, g_post, w_up, conv_w, conv_b, w_down, *, bm, tt):
    r, d = x.shape
    t = tt * bm
    ns, nup = cv0.shape
    hid = w_down.shape[0]
    assert hid % V7X_MXU_DIM == 0 and nup == 2 * hid
    return pl.pallas_call(
        functools.partial(_ffn_kernel, bm=bm),
        grid=(r // t,),
        in_specs=[
            pl.BlockSpec((t, d), lambda i: (i, 0)),
            _const_spec(mod.shape),
            _const_spec((ns, nup)),
            _const_spec((1, d)),
            _const_spec((1, d)),
            _const_spec(w_up.shape),
            _const_spec(conv_w.shape),
            _const_spec((1, nup)),
            _const_spec(w_down.shape),
        ],
        out_specs=[
            pl.BlockSpec((t, d), lambda i: (i, 0)),
            pl.BlockSpec((ns, nup), lambda i: (0, 0)),
        ],
        out_shape=[
            jax.ShapeDtypeStruct((r, d), _F32),
            jax.ShapeDtypeStruct((ns, nup), _F32),
        ],
        scratch_shapes=[pltpu.VMEM((t, hid), _BF16)],
        compiler_params=pltpu.CompilerParams(
            dimension_semantics=("arbitrary",), vmem_limit_bytes=VMEM_LIMIT_BYTES),
        name="ffn_layer",
    )(x, mod, cv0, g_pre, g_post, w_up, conv_w, conv_b, w_down)


def _s5_prep_kernel(are_ref, aim_ref, ldt_ref, btr_ref, bti_ref, cti_ref,
                    lbr_ref, lbi_ref, bbr_ref, bbi_ref, ncti_ref):
    a_re, a_im = are_ref[...], aim_ref[...]
    dt = jnp.exp(ldt_ref[...])
    mag = jnp.exp(a_re * dt)
    ang = a_im * dt
    lb_re = mag * jnp.cos(ang)
    lb_im = mag * jnp.sin(ang)
    n_re = lb_re - 1.0
    n_im = lb_im
    den = a_re * a_re + a_im * a_im
    f_re = (n_re * a_re + n_im * a_im) / den
    f_im = (n_im * a_re - n_re * a_im) / den
    lbr_ref[...] = lb_re
    lbi_ref[...] = lb_im
    b_re, b_im = btr_ref[...], bti_ref[...]
    bbr_ref[...] = f_re[:, None, :] * b_re - f_im[:, None, :] * b_im
    bbi_ref[...] = f_re[:, None, :] * b_im + f_im[:, None, :] * b_re
    ncti_ref[...] = -cti_ref[...]


def _s5_prep(a_re, a_im, log_dt, b_re, b_im, c_im):
    g, p = a_re.shape
    gc = b_re.shape[-1]
    bt_re = jnp.swapaxes(b_re, 1, 2)
    bt_im = jnp.swapaxes(b_im, 1, 2)
    ct_im = jnp.swapaxes(c_im, 1, 2)
    return pl.pallas_call(
        _s5_prep_kernel,
        out_shape=[
            jax.ShapeDtypeStruct((g, p), _F32),
            jax.ShapeDtypeStruct((g, p), _F32),
            jax.ShapeDtypeStruct((g, gc, p), _F32),
            jax.ShapeDtypeStruct((g, gc, p), _F32),
            jax.ShapeDtypeStruct((g, p, gc), _F32),
        ],
        name="s5_prep",
    )(a_re, a_im, log_dt.reshape(g, 1), bt_re, bt_im, ct_im)


def _block_diag_tiles(w, gpt):
    g, a, b = w.shape
    nt = g // gpt
    w = w.reshape(nt, gpt, a, 1, b)
    eye = jnp.eye(gpt, dtype=bool)[None, :, None, :, None]
    return jnp.where(eye, w, 0.0).reshape(nt, gpt * a, gpt * b)


def _s5_kernel(x_ref, mod_ref, xr0_ref, xi0_ref, gpre_ref, gpost_ref, wb_ref, wc_ref,
               lbr_ref, lbi_ref, dsk_ref, ga_ref, gb_ref,
               y_ref, xr_ref, xi_ref, s_ref, *, bm, unroll):
    i = pl.program_id(0)
    t, d = x_ref.shape
    tt = t // bm
    ntile, kin, ncol2 = wb_ref.shape
    ncol = ncol2 // 2

    @pl.when(i == 0)
    def _():
        xr_ref[...] = xr0_ref[...]
        xi_ref[...] = xi0_ref[...]

    x = x_ref[...]
    shift, scale, gate = mod_ref[:, 0:d], mod_ref[:, d:2 * d], mod_ref[:, 2 * d:3 * d]
    h = _pre_mod(x, gpre_ref[...], scale, shift, bm)
    u = h.astype(_BF16)
    for j in range(ntile):
        s_ref[:, j * ncol2:(j + 1) * ncol2] = _dot(u[:, j * kin:(j + 1) * kin], wb_ref[j])

    sub = V7X_SUBLANES
    for j in range(ntile):
        scol = slice(j * ncol, (j + 1) * ncol)
        rcol = slice(j * ncol2, j * ncol2 + ncol)
        icol = slice(j * ncol2 + ncol, (j + 1) * ncol2)
        lr = jnp.broadcast_to(lbr_ref[:, scol], (sub, ncol))
        li = jnp.broadcast_to(lbi_ref[:, scol], (sub, ncol))

        def rb_body(rb, carry, scol=scol, rcol=rcol, icol=icol, lr=lr, li=li):
            r0 = pl.multiple_of(rb * sub, sub)
            xr = xr_ref[pl.ds(r0, sub), scol]
            xi = xi_ref[pl.ds(r0, sub), scol]

            def t_body(ts, c):
                pr, pi = c
                row = pl.multiple_of(ts * bm + r0, sub)
                nr = lr * pr - li * pi + s_ref[pl.ds(row, sub), rcol]
                ni = lr * pi + li * pr + s_ref[pl.ds(row, sub), icol]
                s_ref[pl.ds(row, sub), rcol] = nr
                s_ref[pl.ds(row, sub), icol] = ni
                return nr, ni

            xr, xi = lax.fori_loop(0, tt, t_body, (xr, xi), unroll=unroll)
            xr_ref[pl.ds(r0, sub), scol] = xr
            xi_ref[pl.ds(r0, sub), scol] = xi
            return carry

        lax.fori_loop(0, bm // sub, rb_body, 0)

    ys = [_dot(s_ref[:, j * ncol2:(j + 1) * ncol2].astype(_BF16), wc_ref[j]) for j in range(ntile)]
    y = jnp.concatenate(ys, axis=1) + dsk_ref[...] * h
    g = _gelu(y).astype(_BF16)
    out = _dot(g, ga_ref[...]) * jax.nn.sigmoid(_dot(g, gb_ref[...]))
    y_ref[...] = _gated_residual(x, out, gpost_ref[...], gate, bm)


def _s5_layer(x, mod, xr0, xi0, g_pre, g_post, wb, wc, lbr, lbi, dskip, glu_a, glu_b, *, bm, tt):
    r, d = x.shape
    t = tt * bm
    nst = xr0.shape[1]
    return pl.pallas_call(
        functools.partial(_s5_kernel, bm=bm, unroll=min(tt, 4)),
        grid=(r // t,),
        in_specs=[
            pl.BlockSpec((t, d), lambda i: (i, 0)),
            _const_spec(mod.shape),
            _const_spec((bm, nst)),
            _const_spec((bm, nst)),
            _const_spec((1, d)),
            _const_spec((1, d)),
            _const_spec(wb.shape),
            _const_spec(wc.shape),
            _const_spec((1, nst)),
            _const_spec((1, nst)),
            _const_spec((1, d)),
            _const_spec(glu_a.shape),
            _const_spec(glu_b.shape),
        ],
        out_specs=[
            pl.BlockSpec((t, d), lambda i: (i, 0)),
            pl.BlockSpec((bm, nst), lambda i: (0, 0)),
            pl.BlockSpec((bm, nst), lambda i: (0, 0)),
        ],
        out_shape=[
            jax.ShapeDtypeStruct((r, d), _F32),
            jax.ShapeDtypeStruct((bm, nst), _F32),
            jax.ShapeDtypeStruct((bm, nst), _F32),
        ],
        scratch_shapes=[pltpu.VMEM((t, 2 * nst), _F32)],
        compiler_params=pltpu.CompilerParams(
            dimension_semantics=("arbitrary",), vmem_limit_bytes=VMEM_LIMIT_BYTES),
        name="s5_layer",
    )(x, mod, xr0, xi0, g_pre, g_post, wb, wc, lbr, lbi, dskip, glu_a, glu_b)


def _time_major(a):
    b, l, c = a.shape
    return jnp.swapaxes(a, 0, 1).reshape(l * b, c)


def _batch_major(a, b):
    lb, c = a.shape
    return jnp.swapaxes(a.reshape(lb // b, b, c), 0, 1)


def kernel(x_prompt, x_sample, c_prompt, c_sample, state_pool, state_ssm_re, state_ssm_im, state_ffn_conv, ada_w, ada_b, mix_pre_g, mix_post_g, ffn_pre_g, ffn_post_g, pool_w, pool_scale, ssm_A_re, ssm_A_im, ssm_log_dt, ssm_B_re, ssm_B_im, ssm_C_re, ssm_C_im, ssm_D, ssm_glu_a, ssm_glu_b, ffn_w_up, ffn_conv_w, ffn_conv_b, ffn_w_down):
    depth = ada_w.shape[0]
    bp, lp, d = x_prompt.shape
    bs, ls, _ = x_sample.shape
    nup = ffn_w_up.shape[-1]
    groups = ssm_A_re.shape[1]
    nst = groups * SSM_P
    gpt = V7X_MXU_DIM // SSM_GC

    mod_p, mod_s = _ada_mod(c_prompt, c_sample, ada_w, ada_b)

    geo = {
        "p": dict(bm=bp, pool_tt=64, ffn_tt=64, s5_tt=32, pos_base=0),
        "s": dict(bm=bs, pool_tt=ls, ffn_tt=ls // 2, s5_tt=ls // 4, pos_base=PAST_LEN),
    }
    ys = {"p": _time_major(x_prompt), "s": _time_major(x_sample)}
    mods = {"p": mod_p, "s": mod_s}
    pool_out = {"p": [], "s": []}
    ssm_out = {"p": [], "s": []}
    conv_out = {"p": [], "s": []}

    for l in range(depth):
        j = l // 2
        row = lambda a: a[l].reshape(1, -1)
        if l % 2 == 0:
            pw = pool_w[j].astype(_BF16)
            ps = pool_scale[j].reshape(1, d)
            for k in ("p", "s"):
                gk = geo[k]
                if k == "p":
                    st0 = jnp.zeros((POOL_STATE * bp, d), _F32)
                else:
                    st0 = _time_major(state_pool[j])
                ys[k], st = _pool_layer(ys[k], mods[k][l], st0, row(mix_pre_g), row(mix_post_g), pw, ps,
                                        bm=gk["bm"], tt=gk["pool_tt"], pos_base=gk["pos_base"])
                pool_out[k].append(_batch_major(st, gk["bm"]))
        else:
            lb_re, lb_im, bbt_re, bbt_im, nct_im = _s5_prep(
                ssm_A_re[j], ssm_A_im[j], ssm_log_dt[j], ssm_B_re[j], ssm_B_im[j], ssm_C_im[j])
            ct_re = jnp.swapaxes(ssm_C_re[j], 1, 2)
            wb = jnp.concatenate([_block_diag_tiles(bbt_re, gpt), _block_diag_tiles(bbt_im, gpt)],
                                 axis=2).astype(_BF16)
            wc = jnp.concatenate([_block_diag_tiles(ct_re, gpt), _block_diag_tiles(nct_im, gpt)],
                                 axis=1).astype(_BF16)
            lbr = lb_re.reshape(1, nst)
            lbi = lb_im.reshape(1, nst)
            ga = ssm_glu_a[j].astype(_BF16)
            gb = ssm_glu_b[j].astype(_BF16)
            for k in ("p", "s"):
                gk = geo[k]
                if k == "p":
                    xr0 = jnp.zeros((bp, nst), _F32)
                    xi0 = xr0
                else:
                    xr0 = state_ssm_re[j].reshape(bs, nst)
                    xi0 = state_ssm_im[j].reshape(bs, nst)
                ys[k], xr, xi = _s5_layer(ys[k], mods[k][l], xr0, xi0, row(mix_pre_g), row(mix_post_g),
                                          wb, wc, lbr, lbi, ssm_D[j].reshape(1, d), ga, gb,
                                          bm=gk["bm"], tt=gk["s5_tt"])
                ssm_out[k].append((xr.reshape(gk["bm"], groups, SSM_P), xi.reshape(gk["bm"], groups, SSM_P)))

        w_up = ffn_w_up[l].astype(_BF16)
        w_dn = ffn_w_down[l].astype(_BF16)
        for k in ("p", "s"):
            gk = geo[k]
            if k == "p":
                cv0 = jnp.zeros(((CONV_W - 1) * bp, nup), _F32)
            else:
                cv0 = _time_major(state_ffn_conv[l])
            ys[k], cv = _ffn_layer(ys[k], mods[k][l], cv0, row(ffn_pre_g), row(ffn_post_g),
                                   w_up, ffn_conv_w[l], ffn_conv_b[l].reshape(1, nup), w_dn,
                                   bm=gk["bm"], tt=gk["ffn_tt"])
            conv_out[k].append(_batch_major(cv, gk["bm"]))

    y_prompt = _batch_major(ys["p"], bp)
    y_sample = _batch_major(ys["s"], bs)
    return (y_prompt, y_sample,
            jnp.stack(pool_out["p"]), jnp.stack(pool_out["s"]),
            jnp.stack([a for a, _ in ssm_out["p"]]), jnp.stack([b for _, b in ssm_out["p"]]),
            jnp.stack([a for a, _ in ssm_out["s"]]), jnp.stack([b for _, b in ssm_out["s"]]),
            jnp.stack(conv_out["p"]), jnp.stack(conv_out["s"]))
```

```python
import functools
import math

import jax
import jax.numpy as jnp
from jax import lax
from jax.experimental import pallas as pl
from jax.experimental.pallas import tpu as pltpu

POOL_WINDOWS = (2, 4, 8, 16)
POOL_STATE = max(POOL_WINDOWS) - 1
SSM_GC = 16
SSM_P = 64
CONV_W = 3
EPS = 1e-6
PAST_LEN = 16384

V7X_SUBLANES = 8
V7X_MXU_DIM = 256
VMEM_LIMIT_BYTES = 56 * 1024 * 1024

_F32 = jnp.float32
_BF16 = jnp.bfloat16


def _resident(block_shape, index_map):
    return pl.BlockSpec(block_shape, index_map, pipeline_mode=pl.Buffered(1))


def _whole(shape):
    nd = len(shape)
    return _resident(shape, lambda i: (0,) * nd)


def _layer_slab(shape, l):
    nd = len(shape)
    return _resident((None,) + tuple(shape[1:]), lambda i: (l,) + (0,) * (nd - 1))


def _rmsnorm(x, g):
    return x * lax.rsqrt(jnp.mean(x * x, axis=-1, keepdims=True) + EPS) * g


def _pre_mod(x, g, scale, shift, bm):
    t, d = x.shape
    r = _rmsnorm(x, g).reshape(t // bm, bm, d)
    return (r * (1.0 + scale)[None] + shift[None]).reshape(t, d)


def _gated_residual(x, m, g, gate, bm):
    t, d = x.shape
    r = _rmsnorm(m, g).reshape(t // bm, bm, d)
    return x + (gate[None] * r).reshape(t, d)


def _gelu(x):
    return 0.5 * x * (1.0 + lax.erf(x * math.sqrt(0.5)))


def _dot(a, b):
    return jnp.dot(a, b, preferred_element_type=_F32)


def _load_rows(x_ref, batch_major):
    if not batch_major:
        return x_ref[...]
    b, tt, d = x_ref.shape
    return jnp.swapaxes(x_ref[...], 0, 1).reshape(tt * b, d)


def _store_rows(y_ref, y, batch_major):
    if not batch_major:
        y_ref[...] = y
    else:
        b, tt, d = y_ref.shape
        y_ref[...] = jnp.swapaxes(y.reshape(tt, b, d), 0, 1)


def _mod3(mod_ref, k, d):
    return tuple(mod_ref[:, (3 * k + n) * d:(3 * k + n + 1) * d] for n in range(3))


def _ada_kernel(cp_ref, cs_ref, w_ref, b_ref, mp_ref, ms_ref):
    w = w_ref[...].astype(_BF16)
    b = b_ref[...]
    for c_ref, o_ref in ((cp_ref, mp_ref), (cs_ref, ms_ref)):
        c = c_ref[...]
        s = (c * jax.nn.sigmoid(c)).astype(_BF16)
        o_ref[...] = _dot(s, w) + b


def _ada_mod(c_prompt, c_sample, ada_w, ada_b):
    depth, d, n = ada_w.shape
    tn = 1536
    bp, bs = c_prompt.shape[0], c_sample.shape[0]
    return pl.pallas_call(
        _ada_kernel,
        grid=(depth, n // tn),
        in_specs=[
            pl.BlockSpec((bp, d), lambda l, j: (0, 0)),
            pl.BlockSpec((bs, d), lambda l, j: (0, 0)),
            pl.BlockSpec((None, d, tn), lambda l, j: (l, 0, j)),
            pl.BlockSpec((None, 1, tn), lambda l, j: (l, 0, j)),
        ],
        out_specs=[
            pl.BlockSpec((None, bp, tn), lambda l, j: (l, 0, j)),
            pl.BlockSpec((None, bs, tn), lambda l, j: (l, 0, j)),
        ],
        out_shape=[
            jax.ShapeDtypeStruct((depth, bp, n), _F32),
            jax.ShapeDtypeStruct((depth, bs, n), _F32),
        ],
        compiler_params=pltpu.CompilerParams(
            dimension_semantics=("arbitrary", "arbitrary"),
            vmem_limit_bytes=VMEM_LIMIT_BYTES),
        name="ada_mod",
    )(c_prompt, c_sample, ada_w, ada_b.reshape(depth, 1, n))


def _pool_mix(h, st_ref, pw_ref, ps_ref, i, *, bm, pos_base):
    t, d = h.shape
    tt = t // bm
    gc = d // len(POOL_WINDOWS)
    ext = jnp.concatenate([st_ref[...], h], axis=0)
    st_ref[...] = ext[t:, :]
    parts = []
    for gi, w in enumerate(POOL_WINDOWS):
        cols = slice(gi * gc, (gi + 1) * gc)
        s = ext[:, cols]
        span = 1
        while span < w:
            n = s.shape[0]
            s = s[span * bm:] + s[:n - span * bm]
            span *= 2
        k0 = (POOL_STATE - (w - 1)) * bm
        wsum = s[k0:k0 + t]
        if pos_base + 1 >= w:
            cnt = float(w)
        else:
            row = lax.broadcasted_iota(jnp.int32, (t, gc), 0)
            pos = pos_base + i * tt + lax.shift_right_logical(row, bm.bit_length() - 1)
            cnt = jnp.minimum(pos + 1, w).astype(_F32)
        pooled = wsum / cnt - h[:, cols]
        parts.append(_dot(pooled.astype(_BF16), pw_ref[gi]))
    return jnp.concatenate(parts, axis=1) * ps_ref[...]


def _conv_ffn(f, cv_ref, wup_ref, cw_ref, cb_ref, wdn_ref, h_ref, *, bm):
    t = f.shape[0]
    hid = wdn_ref.shape[0]
    hc = V7X_MXU_DIM

    def conv_cols(c0):
        cols = slice(c0, c0 + hc)
        up = _dot(f, wup_ref[:, cols])
        ext = jnp.concatenate([cv_ref[:, cols], up], axis=0)
        cv_ref[:, cols] = ext[t:, :]
        conv = cb_ref[:, cols]
        for k in range(CONV_W):
            conv = conv + ext[k * bm:k * bm + t, :] * cw_ref[k:k + 1, cols]
        return conv

    for c in range(hid // hc):
        gate_c = conv_cols(c * hc)
        val_c = conv_cols(hid + c * hc)
        h_ref[:, c * hc:(c + 1) * hc] = (_gelu(gate_c) * val_c).astype(_BF16)
    return _dot(h_ref[...], wdn_ref[...])


def _layer_kernel(*refs, layer, bm, pos_base, with_pool, with_ffn, in_batch_major, out_batch_major):
    refs = list(refs)
    x_ref, mod_ref = refs[:2]
    del refs[:2]
    if with_pool:
        st0_ref, mpre_ref, mpost_ref, pw_ref, ps_ref = refs[:5]
        del refs[:5]
    if with_ffn:
        cv0_ref, fpre_ref, fpost_ref, wup_ref, cw_ref, cb_ref, wdn_ref = refs[:7]
        del refs[:7]
    y_ref = refs.pop(0)
    if with_pool:
        st_ref = refs.pop(0)
    if with_ffn:
        cv_ref, h_ref = refs
    i = pl.program_id(0)
    d = mod_ref.shape[1] // 6
    lrow = slice(layer, layer + 1)

    @pl.when(i == 0)
    def _():
        if with_pool:
            st_ref[...] = st0_ref[...]
        if with_ffn:
            cv_ref[...] = cv0_ref[...]

    x = _load_rows(x_ref, in_batch_major)
    if with_pool:
        shift, scale, gate = _mod3(mod_ref, 0, d)
        h = _pre_mod(x, mpre_ref[lrow, :], scale, shift, bm)
        m = _pool_mix(h, st_ref, pw_ref, ps_ref, i, bm=bm, pos_base=pos_base)
        x = _gated_residual(x, m, mpost_ref[lrow, :], gate, bm)
    if with_ffn:
        shift, scale, gate = _mod3(mod_ref, 1, d)
        f = _pre_mod(x, fpre_ref[lrow, :], scale, shift, bm).astype(_BF16)
        o = _conv_ffn(f, cv_ref, wup_ref, cw_ref, cb_ref.at[lrow, :], wdn_ref, h_ref, bm=bm)
        x = _gated_residual(x, o, fpost_ref[lrow, :], gate, bm)
    _store_rows(y_ref, x, out_batch_major)


def _token_layer(x, mod, gains, *, layer, bm, tt, pos_base=0, pool=None, ffn=None,
                 in_batch_major=False, out_batch_major=False):
    mix_pre_g, mix_post_g, ffn_pre_g, ffn_post_g = gains
    d = x.shape[-1]
    nrows = x.shape[0] * x.shape[1] if in_batch_major else x.shape[0]
    t = tt * bm
    tm_spec = pl.BlockSpec((t, d), lambda i: (i, 0))
    bm_spec = pl.BlockSpec((bm, tt, d), lambda i: (0, i, 0))

    args = [x, mod]
    in_specs = [bm_spec if in_batch_major else tm_spec, _layer_slab(mod.shape, layer)]
    out_specs = [bm_spec if out_batch_major else tm_spec]
    out_shape = [jax.ShapeDtypeStruct((bm, nrows // bm, d) if out_batch_major else (nrows, d), _F32)]
    scratch = []
    if pool is not None:
        st0, pool_w, pool_scale = pool
        args += [st0, mix_pre_g, mix_post_g, pool_w, pool_scale]
        in_specs += [_whole(a.shape) for a in args[-5:]]
        out_specs += [_whole(st0.shape)]
        out_shape += [jax.ShapeDtypeStruct(st0.shape, _F32)]
    if ffn is not None:
        cv0, w_up, conv_w, conv_b, w_down = ffn
        hid = w_down.shape[1]
        assert hid % V7X_MXU_DIM == 0 and cv0.shape[1] == 2 * hid
        args += [cv0, ffn_pre_g, ffn_post_g, w_up, conv_w, conv_b, w_down]
        in_specs += [_whole(cv0.shape), _whole(ffn_pre_g.shape), _whole(ffn_post_g.shape),
                     _layer_slab(w_up.shape, layer), _layer_slab(conv_w.shape, layer),
                     _whole(conv_b.shape), _layer_slab(w_down.shape, layer)]
        out_specs += [_whole(cv0.shape)]
        out_shape += [jax.ShapeDtypeStruct(cv0.shape, _F32)]
        scratch += [pltpu.VMEM((t, hid), _BF16)]
    return pl.pallas_call(
        functools.partial(_layer_kernel, layer=layer, bm=bm, pos_base=pos_base,
                          with_pool=pool is not None, with_ffn=ffn is not None,
                          in_batch_major=in_batch_major, out_batch_major=out_batch_major),
        grid=(nrows // t,),
        in_specs=in_specs,
        out_specs=out_specs,
        out_shape=out_shape,
        scratch_shapes=scratch,
        compiler_params=pltpu.CompilerParams(
            dimension_semantics=("arbitrary",), vmem_limit_bytes=VMEM_LIMIT_BYTES),
        name="_".join(n for n, on in (("pool", pool), ("ffn", ffn)) if on is not None) + "_layer",
    )(*args)


def _s5_prep_kernel(are_ref, aim_ref, ldt_ref, btr_ref, bti_ref, cti_ref,
                    lbr_ref, lbi_ref, bbr_ref, bbi_ref, ncti_ref):
    a_re, a_im = are_ref[...], aim_ref[...]
    dt = jnp.exp(ldt_ref[...])
    mag = jnp.exp(a_re * dt)
    ang = a_im * dt
    lb_re = mag * jnp.cos(ang)
    lb_im = mag * jnp.sin(ang)
    n_re = lb_re - 1.0
    n_im = lb_im
    den = a_re * a_re + a_im * a_im
    f_re = (n_re * a_re + n_im * a_im) / den
    f_im = (n_im * a_re - n_re * a_im) / den
    lbr_ref[...] = lb_re
    lbi_ref[...] = lb_im
    b_re, b_im = btr_ref[...], bti_ref[...]
    bbr_ref[...] = f_re[:, None, :] * b_re - f_im[:, None, :] * b_im
    bbi_ref[...] = f_re[:, None, :] * b_im + f_im[:, None, :] * b_re
    ncti_ref[...] = -cti_ref[...]


def _s5_prep(a_re, a_im, log_dt, b_re, b_im, c_im):
    g, p = a_re.shape
    gc = b_re.shape[-1]
    bt_re = jnp.swapaxes(b_re, 1, 2)
    bt_im = jnp.swapaxes(b_im, 1, 2)
    ct_im = jnp.swapaxes(c_im, 1, 2)
    return pl.pallas_call(
        _s5_prep_kernel,
        out_shape=[
            jax.ShapeDtypeStruct((g, p), _F32),
            jax.ShapeDtypeStruct((g, p), _F32),
            jax.ShapeDtypeStruct((g, gc, p), _F32),
            jax.ShapeDtypeStruct((g, gc, p), _F32),
            jax.ShapeDtypeStruct((g, p, gc), _F32),
        ],
        name="s5_prep",
    )(a_re, a_im, log_dt.reshape(g, 1), bt_re, bt_im, ct_im)


def _block_diag_tiles(w, gpt):
    g, a, b = w.shape
    nt = g // gpt
    w = w.reshape(nt, gpt, a, 1, b)
    eye = jnp.eye(gpt, dtype=bool)[None, :, None, :, None]
    return jnp.where(eye, w, 0.0).reshape(nt, gpt * a, gpt * b)


def _s5_kernel(x_ref, mod_ref, xr0_ref, xi0_ref, gpre_ref, gpost_ref, wb_ref, wc_ref,
               lbr_ref, lbi_ref, dsk_ref, ga_ref, gb_ref,
               y_ref, xr_ref, xi_ref, *, layer, bm):
    i = pl.program_id(0)
    t, d = x_ref.shape
    tt = t // bm
    ntile, kin, ncol2 = wb_ref.shape
    ncol = ncol2 // 2
    sub = V7X_SUBLANES
    lrow = slice(layer, layer + 1)

    @pl.when(i == 0)
    def _():
        xr_ref[...] = xr0_ref[...]
        xi_ref[...] = xi0_ref[...]

    x = x_ref[...]
    shift, scale, gate = _mod3(mod_ref, 0, d)
    h = _pre_mod(x, gpre_ref[lrow, :], scale, shift, bm)
    u = h.astype(_BF16)

    ys = []
    for j in range(ntile):
        scol = slice(j * ncol, (j + 1) * ncol)
        bu = _dot(u[:, j * kin:(j + 1) * kin], wb_ref[j])
        lr = jnp.broadcast_to(lbr_ref[:, scol], (sub, ncol))
        li = jnp.broadcast_to(lbi_ref[:, scol], (sub, ncol))
        blocks = [None] * (t // sub)
        for rb in range(bm // sub):
            rows = slice(rb * sub, (rb + 1) * sub)
            pr, pi = xr_ref[rows, scol], xi_ref[rows, scol]
            for ts in range(tt):
                r0 = ts * bm + rb * sub
                nr = lr * pr - li * pi + bu[r0:r0 + sub, :ncol]
                ni = lr * pi + li * pr + bu[r0:r0 + sub, ncol:]
                blocks[r0 // sub] = jnp.concatenate([nr, ni], axis=1)
                pr, pi = nr, ni
            xr_ref[rows, scol] = pr
            xi_ref[rows, scol] = pi
        ys.append(_dot(jnp.concatenate(blocks, axis=0).astype(_BF16), wc_ref[j]))
    y = jnp.concatenate(ys, axis=1) + dsk_ref[...] * h
    g = _gelu(y).astype(_BF16)
    out = _dot(g, ga_ref[...]) * jax.nn.sigmoid(_dot(g, gb_ref[...]))
    y_ref[...] = _gated_residual(x, out, gpost_ref[lrow, :], gate, bm)


def _s5_layer(x, mod, xr0, xi0, g_pre, g_post, wb, wc, lbr, lbi, dskip, glu_a, glu_b, *, layer, bm, tt):
    r, d = x.shape
    t = tt * bm
    nst = xr0.shape[1]
    return pl.pallas_call(
        functools.partial(_s5_kernel, layer=layer, bm=bm),
        grid=(r // t,),
        in_specs=[
            pl.BlockSpec((t, d), lambda i: (i, 0)),
            _layer_slab(mod.shape, layer),
            _whole((bm, nst)),
            _whole((bm, nst)),
            _whole(g_pre.shape),
            _whole(g_post.shape),
            _whole(wb.shape),
            _whole(wc.shape),
            _whole((1, nst)),
            _whole((1, nst)),
            _whole((1, d)),
            _whole(glu_a.shape),
            _whole(glu_b.shape),
        ],
        out_specs=[
            pl.BlockSpec((t, d), lambda i: (i, 0)),
            _whole((bm, nst)),
            _whole((bm, nst)),
        ],
        out_shape=[
            jax.ShapeDtypeStruct((r, d), _F32),
            jax.ShapeDtypeStruct((bm, nst), _F32),
            jax.ShapeDtypeStruct((bm, nst), _F32),
        ],
        compiler_params=pltpu.CompilerParams(
            dimension_semantics=("arbitrary",), vmem_limit_bytes=VMEM_LIMIT_BYTES),
        name="s5_layer",
    )(x, mod, xr0, xi0, g_pre, g_post, wb, wc, lbr, lbi, dskip, glu_a, glu_b)


def _time_major(a):
    b, l, c = a.shape
    return jnp.swapaxes(a, 0, 1).reshape(l * b, c)


def _batch_major(a, b):
    lb, c = a.shape
    return jnp.swapaxes(a.reshape(lb // b, b, c), 0, 1)


def kernel(x_prompt, x_sample, c_prompt, c_sample, state_pool, state_ssm_re, state_ssm_im, state_ffn_conv, ada_w, ada_b, mix_pre_g, mix_post_g, ffn_pre_g, ffn_post_g, pool_w, pool_scale, ssm_A_re, ssm_A_im, ssm_log_dt, ssm_B_re, ssm_B_im, ssm_C_re, ssm_C_im, ssm_D, ssm_glu_a, ssm_glu_b, ffn_w_up, ffn_conv_w, ffn_conv_b, ffn_w_down):
    depth = ada_w.shape[0]
    bp, lp, d = x_prompt.shape
    bs, ls, _ = x_sample.shape
    nup = ffn_w_up.shape[-1]
    groups = ssm_A_re.shape[1]
    nst = groups * SSM_P
    gpt = V7X_MXU_DIM // SSM_GC

    mod_p, mod_s = _ada_mod(c_prompt, c_sample, ada_w, ada_b)
    gains = (mix_pre_g, mix_post_g, ffn_pre_g, ffn_post_g)
    w_up = ffn_w_up.astype(_BF16)
    w_dn = ffn_w_down.astype(_BF16)

    geo = {
        "p": dict(bm=bp, ffn_tt=64, s5_tt=32, pos_base=0, fuse_pool=True),
        "s": dict(bm=bs, ffn_tt=ls // 2, s5_tt=ls // 4, pos_base=PAST_LEN, fuse_pool=False, pool_tt=ls),
    }
    ys = {"p": x_prompt, "s": _time_major(x_sample)}
    mods = {"p": mod_p, "s": mod_s}
    pool_out = {"p": [], "s": []}
    ssm_out = {"p": [], "s": []}
    conv_out = {"p": [], "s": []}

    for l in range(depth):
        j = l // 2
        pool = {"p": None, "s": None}
        if l % 2 == 0:
            pw = pool_w[j].astype(_BF16)
            ps = pool_scale[j].reshape(1, d)
            pool["p"] = (jnp.zeros((POOL_STATE * bp, d), _F32), pw, ps)
            pool["s"] = (_time_major(state_pool[j]), pw, ps)
        else:
            lb_re, lb_im, bbt_re, bbt_im, nct_im = _s5_prep(
                ssm_A_re[j], ssm_A_im[j], ssm_log_dt[j], ssm_B_re[j], ssm_B_im[j], ssm_C_im[j])
            ct_re = jnp.swapaxes(ssm_C_re[j], 1, 2)
            wb = jnp.concatenate([_block_diag_tiles(bbt_re, gpt), _block_diag_tiles(bbt_im, gpt)],
                                 axis=2).astype(_BF16)
            wc = jnp.concatenate([_block_diag_tiles(ct_re, gpt), _block_diag_tiles(nct_im, gpt)],
                                 axis=1).astype(_BF16)
            lbr = lb_re.reshape(1, nst)
            lbi = lb_im.reshape(1, nst)
            ga = ssm_glu_a[j].astype(_BF16)
            gb = ssm_glu_b[j].astype(_BF16)
            for k in ("p", "s"):
                gk = geo[k]
                if k == "p":
                    xr0 = jnp.zeros((bp, nst), _F32)
                    xi0 = xr0
                else:
                    xr0 = state_ssm_re[j].reshape(bs, nst)
                    xi0 = state_ssm_im[j].reshape(bs, nst)
                ys[k], xr, xi = _s5_layer(ys[k], mods[k], xr0, xi0, mix_pre_g, mix_post_g,
                                          wb, wc, lbr, lbi, ssm_D[j].reshape(1, d), ga, gb,
                                          layer=l, bm=gk["bm"], tt=gk["s5_tt"])
                ssm_out[k].append((xr.reshape(gk["bm"], groups, SSM_P), xi.reshape(gk["bm"], groups, SSM_P)))

        for k in ("p", "s"):
            gk = geo[k]
            if k == "p":
                cv0 = jnp.zeros(((CONV_W - 1) * bp, nup), _F32)
            else:
                cv0 = _time_major(state_ffn_conv[l])
            ffn = (cv0, w_up, ffn_conv_w, ffn_conv_b, w_dn)
            common = dict(layer=l, bm=gk["bm"], pos_base=gk["pos_base"])
            if pool[k] is not None and not gk["fuse_pool"]:
                ys[k], st = _token_layer(ys[k], mods[k], gains, tt=gk["pool_tt"], pool=pool[k], **common)
                pool_out[k].append(_batch_major(st, gk["bm"]))
                pool[k] = None
            outs = _token_layer(ys[k], mods[k], gains, tt=gk["ffn_tt"], pool=pool[k], ffn=ffn,
                                in_batch_major=(k == "p" and l == 0),
                                out_batch_major=(k == "p" and l == depth - 1), **common)
            ys[k], cv = outs[0], outs[-1]
            if pool[k] is not None:
                pool_out[k].append(_batch_major(outs[1], gk["bm"]))
            conv_out[k].append(_batch_major(cv, gk["bm"]))

    y_prompt = ys["p"]
    y_sample = _batch_major(ys["s"], bs)
    return (y_prompt, y_sample,
            jnp.stack(pool_out["p"]), jnp.stack(pool_out["s"]),
            jnp.stack([a for a, _ in ssm_out["p"]]), jnp.stack([b for _, b in ssm_out["p"]]),
            jnp.stack([a for a, _ in ssm_out["s"]]), jnp.stack([b for _, b in ssm_out["s"]]),
            jnp.stack(conv_out["p"]), jnp.stack(conv_out["s"]))
```

```python
import functools
import math

import jax
import jax.numpy as jnp
from jax import lax
from jax.experimental import pallas as pl
from jax.experimental.pallas import tpu as pltpu

POOL_WINDOWS = (2, 4, 8, 16)
POOL_STATE = max(POOL_WINDOWS) - 1
SSM_GC = 16
SSM_P = 64
CONV_W = 3
S5_CHUNK = 16
EPS = 1e-6
PAST_LEN = 16384

V7X_SUBLANES = 8
V7X_LANES = 128
V7X_MXU_DIM = 256
VMEM_LIMIT_BYTES = 56 * 1024 * 1024

_F32 = jnp.float32
_BF16 = jnp.bfloat16
_HI = lax.Precision.HIGHEST


def _resident(block_shape, index_map):
    return pl.BlockSpec(block_shape, index_map, pipeline_mode=pl.Buffered(1))


def _whole(shape):
    nd = len(shape)
    return _resident(shape, lambda i: (0,) * nd)


def _layer_slab(shape, l):
    nd = len(shape)
    return _resident((None,) + tuple(shape[1:]), lambda i: (l,) + (0,) * (nd - 1))


def _rmsnorm(x, g):
    return x * lax.rsqrt(jnp.mean(x * x, axis=-1, keepdims=True) + EPS) * g


def _pre_mod(x, g, scale, shift, bm):
    t, d = x.shape
    r = _rmsnorm(x, g).reshape(t // bm, bm, d)
    return (r * (1.0 + scale)[None] + shift[None]).reshape(t, d)


def _gated_residual(x, m, g, gate, bm):
    t, d = x.shape
    r = _rmsnorm(m, g).reshape(t // bm, bm, d)
    return x + (gate[None] * r).reshape(t, d)


def _gelu(x):
    return 0.5 * x * (1.0 + lax.erf(x * math.sqrt(0.5)))


def _dot(a, b):
    return jnp.dot(a, b, preferred_element_type=_F32)


def _load_rows(x_ref, s, tt, bm, batch_major):
    if not batch_major:
        return x_ref[s * tt * bm:(s + 1) * tt * bm, :]
    d = x_ref.shape[-1]
    return jnp.swapaxes(x_ref[:, s * tt:(s + 1) * tt, :], 0, 1).reshape(tt * bm, d)


def _store_rows(y_ref, y, s, tt, bm, batch_major):
    if not batch_major:
        y_ref[s * tt * bm:(s + 1) * tt * bm, :] = y
    else:
        d = y_ref.shape[-1]
        y_ref[:, s * tt:(s + 1) * tt, :] = jnp.swapaxes(y.reshape(tt, bm, d), 0, 1)


def _mod3(mod_ref, k, d):
    return tuple(mod_ref[:, (3 * k + n) * d:(3 * k + n + 1) * d] for n in range(3))


def _ada_kernel(cp_ref, cs_ref, w_ref, b_ref, mp_ref, ms_ref):
    w = w_ref[...].astype(_BF16)
    b = b_ref[...]
    for c_ref, o_ref in ((cp_ref, mp_ref), (cs_ref, ms_ref)):
        c = c_ref[...]
        s = (c * jax.nn.sigmoid(c)).astype(_BF16)
        o_ref[...] = _dot(s, w) + b


def _ada_mod(c_prompt, c_sample, ada_w, ada_b):
    depth, d, n = ada_w.shape
    tn = 1536
    bp, bs = c_prompt.shape[0], c_sample.shape[0]
    return pl.pallas_call(
        _ada_kernel,
        grid=(depth, n // tn),
        in_specs=[
            pl.BlockSpec((bp, d), lambda l, j: (0, 0)),
            pl.BlockSpec((bs, d), lambda l, j: (0, 0)),
            pl.BlockSpec((None, d, tn), lambda l, j: (l, 0, j)),
            pl.BlockSpec((None, 1, tn), lambda l, j: (l, 0, j)),
        ],
        out_specs=[
            pl.BlockSpec((None, bp, tn), lambda l, j: (l, 0, j)),
            pl.BlockSpec((None, bs, tn), lambda l, j: (l, 0, j)),
        ],
        out_shape=[
            jax.ShapeDtypeStruct((depth, bp, n), _F32),
            jax.ShapeDtypeStruct((depth, bs, n), _F32),
        ],
        compiler_params=pltpu.CompilerParams(
            dimension_semantics=("arbitrary", "arbitrary"),
            vmem_limit_bytes=VMEM_LIMIT_BYTES),
        name="ada_mod",
    )(c_prompt, c_sample, ada_w, ada_b.reshape(depth, 1, n))


def _pool_mix(h, st_ref, pw_ref, ps_ref, i, *, bm, pos_base):
    t, d = h.shape
    tt = t // bm
    gc = d // len(POOL_WINDOWS)
    ext = jnp.concatenate([st_ref[...], h], axis=0)
    st_ref[...] = ext[t:, :]
    parts = []
    for gi, w in enumerate(POOL_WINDOWS):
        cols = slice(gi * gc, (gi + 1) * gc)
        s = ext[:, cols]
        span = 1
        while span < w:
            n = s.shape[0]
            s = s[span * bm:] + s[:n - span * bm]
            span *= 2
        k0 = (POOL_STATE - (w - 1)) * bm
        wsum = s[k0:k0 + t]
        if pos_base + 1 >= w:
            cnt = float(w)
        else:
            row = lax.broadcasted_iota(jnp.int32, (t, gc), 0)
            pos = pos_base + i * tt + lax.shift_right_logical(row, bm.bit_length() - 1)
            cnt = jnp.minimum(pos + 1, w).astype(_F32)
        pooled = wsum / cnt - h[:, cols]
        parts.append(_dot(pooled.astype(_BF16), pw_ref[gi]))
    return jnp.concatenate(parts, axis=1) * ps_ref[...]


def _ffn_hidden(f, cv_ref, wup_ref, cw_ref, cb_ref, h_ref, *, bm):
    t = f.shape[0]
    hid = h_ref.shape[1]
    hc = V7X_MXU_DIM

    def conv_cols(c0):
        cols = slice(c0, c0 + hc)
        up = _dot(f, wup_ref[:, cols])
        ext = jnp.concatenate([cv_ref[:, cols], up], axis=0)
        cv_ref[:, cols] = ext[t:, :]
        conv = cb_ref[:, cols]
        for k in range(CONV_W):
            conv = conv + ext[k * bm:k * bm + t, :] * cw_ref[k:k + 1, cols]
        return conv

    for c in range(hid // hc):
        gate_c = conv_cols(c * hc)
        val_c = conv_cols(hid + c * hc)
        h_ref[:, c * hc:(c + 1) * hc] = (_gelu(gate_c) * val_c).astype(_BF16)


def _run_skewed(gens):
    n = len(gens)
    step = lambda g: next(g, None)
    step(gens[0])
    for s in range(n):
        if s + 1 < n:
            step(gens[s + 1])
        step(gens[s])
        if s >= 1:
            step(gens[s - 1])
        step(gens[s])
    step(gens[n - 1])


def _layer_kernel(*refs, layer, bm, tt, nsub, pos_base, with_pool, with_ffn, in_batch_major, out_batch_major):
    refs = list(refs)
    x_ref, mod_ref = refs[:2]
    del refs[:2]
    if with_pool:
        st0_ref, mpre_ref, mpost_ref, pw_ref, ps_ref = refs[:5]
        del refs[:5]
    if with_ffn:
        cv0_ref, fpre_ref, fpost_ref, wup_ref, cw_ref, cb_ref, wdn_ref = refs[:7]
        del refs[:7]
    y_ref = refs.pop(0)
    if with_pool:
        st_ref = refs.pop(0)
    if with_ffn:
        cv_ref, h_ref = refs
    i = pl.program_id(0)
    d = mod_ref.shape[1] // 6
    lrow = slice(layer, layer + 1)

    @pl.when(i == 0)
    def _():
        if with_pool:
            st_ref[...] = st0_ref[...]
        if with_ffn:
            cv_ref[...] = cv0_ref[...]

    def sub_tile(s):
        x = _load_rows(x_ref, s, tt, bm, in_batch_major)
        if with_pool:
            shift, scale, gate = _mod3(mod_ref, 0, d)
            h = _pre_mod(x, mpre_ref[lrow, :], scale, shift, bm)
            m = _pool_mix(h, st_ref, pw_ref, ps_ref, i * nsub + s, bm=bm, pos_base=pos_base)
            x = _gated_residual(x, m, mpost_ref[lrow, :], gate, bm)
        if with_ffn:
            shift, scale, gate = _mod3(mod_ref, 1, d)
            f = _pre_mod(x, fpre_ref[lrow, :], scale, shift, bm).astype(_BF16)
            yield
            _ffn_hidden(f, cv_ref, wup_ref, cw_ref, cb_ref.at[lrow, :], h_ref.at[s], bm=bm)
            yield
            o = _dot(h_ref[s], wdn_ref[...])
            yield
            x = _gated_residual(x, o, fpost_ref[lrow, :], gate, bm)
        _store_rows(y_ref, x, s, tt, bm, out_batch_major)

    _run_skewed([sub_tile(s) for s in range(nsub)])


def _token_layer(x, mod, gains, *, layer, bm, tt, nsub=1, pos_base=0, pool=None, ffn=None,
                 in_batch_major=False, out_batch_major=False):
    mix_pre_g, mix_post_g, ffn_pre_g, ffn_post_g = gains
    d = x.shape[-1]
    nrows = x.shape[0] * x.shape[1] if in_batch_major else x.shape[0]
    t = tt * bm
    tm_spec = pl.BlockSpec((nsub * t, d), lambda i: (i, 0))
    bm_spec = pl.BlockSpec((bm, nsub * tt, d), lambda i: (0, i, 0))

    args = [x, mod]
    in_specs = [bm_spec if in_batch_major else tm_spec, _layer_slab(mod.shape, layer)]
    out_specs = [bm_spec if out_batch_major else tm_spec]
    out_shape = [jax.ShapeDtypeStruct((bm, nrows // bm, d) if out_batch_major else (nrows, d), _F32)]
    scratch = []
    if pool is not None:
        st0, pool_w, pool_scale = pool
        args += [st0, mix_pre_g, mix_post_g, pool_w, pool_scale]
        in_specs += [_whole(a.shape) for a in args[-5:]]
        out_specs += [_whole(st0.shape)]
        out_shape += [jax.ShapeDtypeStruct(st0.shape, _F32)]
    if ffn is not None:
        cv0, w_up, conv_w, conv_b, w_down = ffn
        hid = w_down.shape[1]
        assert hid % V7X_MXU_DIM == 0 and cv0.shape[1] == 2 * hid
        args += [cv0, ffn_pre_g, ffn_post_g, w_up, conv_w, conv_b, w_down]
        in_specs += [_whole(cv0.shape), _whole(ffn_pre_g.shape), _whole(ffn_post_g.shape),
                     _layer_slab(w_up.shape, layer), _layer_slab(conv_w.shape, layer),
                     _whole(conv_b.shape), _layer_slab(w_down.shape, layer)]
        out_specs += [_whole(cv0.shape)]
        out_shape += [jax.ShapeDtypeStruct(cv0.shape, _F32)]
        scratch += [pltpu.VMEM((nsub, t, hid), _BF16)]
    return pl.pallas_call(
        functools.partial(_layer_kernel, layer=layer, bm=bm, tt=tt, nsub=nsub, pos_base=pos_base,
                          with_pool=pool is not None, with_ffn=ffn is not None,
                          in_batch_major=in_batch_major, out_batch_major=out_batch_major),
        grid=(nrows // (nsub * t),),
        in_specs=in_specs,
        out_specs=out_specs,
        out_shape=out_shape,
        scratch_shapes=scratch,
        compiler_params=pltpu.CompilerParams(
            dimension_semantics=("arbitrary",), vmem_limit_bytes=VMEM_LIMIT_BYTES),
        name="_".join(n for n, on in (("pool", pool), ("ffn", ffn)) if on is not None) + "_layer",
    )(*args)


def _s5_prep_kernel(are_ref, aim_ref, ldt_ref, btr_ref, bti_ref, cti_ref,
                    lbr_ref, lbi_ref, bbr_ref, bbi_ref, ncti_ref):
    a_re, a_im = are_ref[...], aim_ref[...]
    dt = jnp.exp(ldt_ref[...])
    mag = jnp.exp(a_re * dt)
    ang = a_im * dt
    lb_re = mag * jnp.cos(ang)
    lb_im = mag * jnp.sin(ang)
    n_re = lb_re - 1.0
    n_im = lb_im
    den = a_re * a_re + a_im * a_im
    f_re = (n_re * a_re + n_im * a_im) / den
    f_im = (n_im * a_re - n_re * a_im) / den
    lbr_ref[...] = lb_re
    lbi_ref[...] = lb_im
    b_re, b_im = btr_ref[...], bti_ref[...]
    bbr_ref[...] = f_re[:, None, :] * b_re - f_im[:, None, :] * b_im
    bbi_ref[...] = f_re[:, None, :] * b_im + f_im[:, None, :] * b_re
    ncti_ref[...] = -cti_ref[...]


def _s5_prep(a_re, a_im, log_dt, b_re, b_im, c_im):
    g, p = a_re.shape
    gc = b_re.shape[-1]
    bt_re = jnp.swapaxes(b_re, 1, 2)
    bt_im = jnp.swapaxes(b_im, 1, 2)
    ct_im = jnp.swapaxes(c_im, 1, 2)
    return pl.pallas_call(
        _s5_prep_kernel,
        out_shape=[
            jax.ShapeDtypeStruct((g, p), _F32),
            jax.ShapeDtypeStruct((g, p), _F32),
            jax.ShapeDtypeStruct((g, gc, p), _F32),
            jax.ShapeDtypeStruct((g, gc, p), _F32),
            jax.ShapeDtypeStruct((g, p, gc), _F32),
        ],
        name="s5_prep",
    )(a_re, a_im, log_dt.reshape(g, 1), bt_re, bt_im, ct_im)


def _block_diag_tiles(w, gpt):
    g, a, b = w.shape
    nt = g // gpt
    w = w.reshape(nt, gpt, a, 1, b)
    eye = jnp.eye(gpt, dtype=bool)[None, :, None, :, None]
    return jnp.where(eye, w, 0.0).reshape(nt, gpt * a, gpt * b)


def _s5_kernel(x_ref, mod_ref, xr0_ref, xi0_ref, gpre_ref, gpost_ref, wb_ref, wc_ref,
               lbr_ref, lbi_ref, dsk_ref, ga_ref, gb_ref,
               y_ref, xr_ref, xi_ref, *, layer, bm):
    i = pl.program_id(0)
    t, d = x_ref.shape
    tt = t // bm
    ntile, kin, ncol2 = wb_ref.shape
    ncol = ncol2 // 2
    sub = V7X_SUBLANES
    lrow = slice(layer, layer + 1)

    @pl.when(i == 0)
    def _():
        xr_ref[...] = xr0_ref[...]
        xi_ref[...] = xi0_ref[...]

    x = x_ref[...]
    shift, scale, gate = _mod3(mod_ref, 0, d)
    h = _pre_mod(x, gpre_ref[lrow, :], scale, shift, bm)
    u = h.astype(_BF16)

    ys = []
    for j in range(ntile):
        scol = slice(j * ncol, (j + 1) * ncol)
        bu = _dot(u[:, j * kin:(j + 1) * kin], wb_ref[j])
        lr = jnp.broadcast_to(lbr_ref[:, scol], (sub, ncol))
        li = jnp.broadcast_to(lbi_ref[:, scol], (sub, ncol))
        blocks = [None] * (t // sub)
        for rb in range(bm // sub):
            rows = slice(rb * sub, (rb + 1) * sub)
            pr, pi = xr_ref[rows, scol], xi_ref[rows, scol]
            for ts in range(tt):
                r0 = ts * bm + rb * sub
                nr = lr * pr - li * pi + bu[r0:r0 + sub, :ncol]
                ni = lr * pi + li * pr + bu[r0:r0 + sub, ncol:]
                blocks[r0 // sub] = jnp.concatenate([nr, ni], axis=1)
                pr, pi = nr, ni
            xr_ref[rows, scol] = pr
            xi_ref[rows, scol] = pi
        ys.append(_dot(jnp.concatenate(blocks, axis=0).astype(_BF16), wc_ref[j]))
    y = jnp.concatenate(ys, axis=1) + dsk_ref[...] * h
    g = _gelu(y).astype(_BF16)
    out = _dot(g, ga_ref[...]) * jax.nn.sigmoid(_dot(g, gb_ref[...]))
    y_ref[...] = _gated_residual(x, out, gpost_ref[lrow, :], gate, bm)


def _s5_layer(x, mod, xr0, xi0, g_pre, g_post, wb, wc, lbr, lbi, dskip, glu_a, glu_b, *, layer, bm, tt):
    r, d = x.shape
    t = tt * bm
    nst = xr0.shape[1]
    return pl.pallas_call(
        functools.partial(_s5_kernel, layer=layer, bm=bm),
        grid=(r // t,),
        in_specs=[
            pl.BlockSpec((t, d), lambda i: (i, 0)),
            _layer_slab(mod.shape, layer),
            _whole((bm, nst)),
            _whole((bm, nst)),
            _whole(g_pre.shape),
            _whole(g_post.shape),
            _whole(wb.shape),
            _whole(wc.shape),
            _whole((1, nst)),
            _whole((1, nst)),
            _whole((1, d)),
            _whole(glu_a.shape),
            _whole(glu_b.shape),
        ],
        out_specs=[
            pl.BlockSpec((t, d), lambda i: (i, 0)),
            _whole((bm, nst)),
            _whole((bm, nst)),
        ],
        out_shape=[
            jax.ShapeDtypeStruct((r, d), _F32),
            jax.ShapeDtypeStruct((bm, nst), _F32),
            jax.ShapeDtypeStruct((bm, nst), _F32),
        ],
        compiler_params=pltpu.CompilerParams(
            dimension_semantics=("arbitrary",), vmem_limit_bytes=VMEM_LIMIT_BYTES),
        name="s5_layer",
    )(x, mod, xr0, xi0, g_pre, g_post, wb, wc, lbr, lbi, dskip, glu_a, glu_b)


def _s5_chunk_prep_kernel(are_ref, aim_ref, ldt_ref, btr_ref, bti_ref, cr_ref, ci_ref,
                          mt_ref, wsr_ref, wsi_ref, ckr_ref, ncki_ref, l1_ref, l2_ref, l2s_ref):
    a_re, a_im = are_ref[...], aim_ref[...]
    dt = jnp.exp(ldt_ref[...])
    mag = jnp.exp(a_re * dt)
    ang = a_im * dt
    lb_re = mag * jnp.cos(ang)
    lb_im = mag * jnp.sin(ang)
    n_re = lb_re - 1.0
    n_im = lb_im
    den = a_re * a_re + a_im * a_im
    f_re = ((n_re * a_re + n_im * a_im) / den)[:, None, :]
    f_im = ((n_im * a_re - n_re * a_im) / den)[:, None, :]
    b_re, b_im = btr_ref[...], bti_ref[...]
    bb_re = f_re * b_re - f_im * b_im
    bb_im = f_re * b_im + f_im * b_re
    c_re, c_im = cr_ref[...], ci_ref[...]

    pw = [(jnp.ones_like(lb_re), jnp.zeros_like(lb_re))]
    for _ in range(S5_CHUNK):
        pr, pi = pw[-1]
        pw.append((pr * lb_re - pi * lb_im, pr * lb_im + pi * lb_re))

    gc = b_re.shape[1]
    for k, (pr, pi) in enumerate(pw):
        pr, pi = pr[:, None, :], pi[:, None, :]
        rows = slice(k * gc, (k + 1) * gc)
        ckr_ref[:, rows, :] = c_re * pr - c_im * pi
        ncki_ref[:, rows, :] = -(c_re * pi + c_im * pr)
        if k < S5_CHUNK:
            s = S5_CHUNK - 1 - k
            srows = slice(s * gc, (s + 1) * gc)
            wsr_ref[:, srows, :] = pr * bb_re - pi * bb_im
            wsi_ref[:, srows, :] = pr * bb_im + pi * bb_re
    ck_re = ckr_ref[:, :S5_CHUNK * gc, :]
    nck_im = ncki_ref[:, :S5_CHUNK * gc, :]
    mt_ref[...] = (jnp.einsum("gcp,gkp->gck", bb_re, ck_re, precision=_HI, preferred_element_type=_F32)
                   + jnp.einsum("gcp,gkp->gck", bb_im, nck_im, precision=_HI, preferred_element_type=_F32))
    pr, pi = pw[S5_CHUNK]
    l1_ref[...] = jnp.concatenate([pr, pr], axis=1)
    l2_ref[...] = jnp.concatenate([-pi, pi], axis=1)
    l2s_ref[...] = jnp.concatenate([pi, -pi], axis=1)


def _s5_chunk_weights(a_re, a_im, log_dt, b_re, b_im, c_re, c_im):
    g, p = a_re.shape
    gc = b_re.shape[-1]
    n = S5_CHUNK * gc
    f = lambda *s: jax.ShapeDtypeStruct(s, _F32)
    mt, ws_re, ws_im, ck_re, nck_im, l1, l2, l2s = pl.pallas_call(
        _s5_chunk_prep_kernel,
        out_shape=[f(g, gc, n), f(g, n, p), f(g, n, p), f(g, n + gc, p), f(g, n + gc, p),
                   f(g, 2 * p), f(g, 2 * p), f(g, 2 * p)],
        compiler_params=pltpu.CompilerParams(vmem_limit_bytes=VMEM_LIMIT_BYTES),
        name="s5_chunk_prep",
    )(a_re, a_im, log_dt.reshape(g, 1), jnp.swapaxes(b_re, 1, 2), jnp.swapaxes(b_im, 1, 2), c_re, c_im)
    mpad = jnp.pad(mt, ((0, 0), (0, 0), (n - gc, 0)))
    w_t = jnp.stack([mpad[:, :, n - gc - s * gc:2 * n - gc - s * gc] for s in range(S5_CHUNK)], axis=1)
    w_t = w_t.reshape(g, n, n)
    w1 = jnp.concatenate([w_t, ws_re, ws_im, ws_im, ws_re], axis=2).astype(_BF16)
    w2 = jnp.concatenate([jnp.swapaxes(ck_re[:, gc:, :], 1, 2), jnp.swapaxes(nck_im[:, gc:, :], 1, 2)],
                         axis=1).astype(_BF16)
    return w1, w2, l1[:, None, :], l2[:, None, :], l2s[:, None, :]


def _piece_transpose(sets, piece):
    n = len(sets[0])
    lanes = sets[0][0].shape[-1]
    assert n * piece == lanes and n & (n - 1) == 0
    lane = lax.broadcasted_iota(jnp.int32, sets[0][0].shape, 1)
    sets = [list(xs) for xs in sets]
    d = n // 2
    while d >= 1:
        keep = (lane & (d * piece)) == 0
        for xs in sets:
            for i in range(n):
                if i & d:
                    continue
                lo, hi = xs[i], xs[i + d]
                xs[i] = jnp.where(keep, lo, pltpu.roll(hi, d * piece, axis=1))
                xs[i + d] = jnp.where(keep, pltpu.roll(lo, lanes - d * piece, axis=1), hi)
        d //= 2
    return sets


def _to_group_major(h, bm):
    t, d = h.shape
    nck = t // (S5_CHUNK * bm)
    h4 = h.reshape(nck, S5_CHUNK, bm, d)
    a = [h4[:, s].reshape(nck * bm, d) for s in range(S5_CHUNK)]
    per = V7X_LANES // SSM_GC
    nq, nh = d // V7X_LANES, S5_CHUNK // per
    sets = _piece_transpose(
        [[a[hh * per + k][:, q * V7X_LANES:(q + 1) * V7X_LANES] for k in range(per)]
         for q in range(nq) for hh in range(nh)], SSM_GC)
    return [jnp.concatenate([sets[q * nh + hh][aa] for hh in range(nh)], axis=1)
            for q in range(nq) for aa in range(per)]


def _from_group_major(get, nck, bm, d):
    per = V7X_LANES // SSM_GC
    nq, nh = d // V7X_LANES, S5_CHUNK // per
    sets = _piece_transpose(
        [[get(q * per + aa, hh) for aa in range(per)] for q in range(nq) for hh in range(nh)], SSM_GC)
    rows = [jnp.concatenate([sets[q * nh + s // per][s % per] for q in range(nq)], axis=1).reshape(nck, 1, bm, d)
            for s in range(S5_CHUNK)]
    return jnp.concatenate(rows, axis=1).reshape(nck * S5_CHUNK * bm, d)


def _s5_in_kernel(x_ref, mod_ref, gpre_ref, v_ref, *, layer, bm):
    d = x_ref.shape[1]
    shift, scale, _ = _mod3(mod_ref, 0, d)
    h = _pre_mod(x_ref[...], gpre_ref[layer:layer + 1, :], scale, shift, bm)
    for g, vg in enumerate(_to_group_major(h, bm)):
        v_ref[g] = vg.astype(_BF16)


def _s5_core_kernel(v_ref, w1_ref, w2_ref, l1_ref, l2_ref, l2s_ref, x0_ref, x0s_ref,
                    yv_ref, xe_ref, r_ref, *, bm):
    gb, m, n = v_ref.shape
    nck = m // bm
    ns = l1_ref.shape[-1]
    for g in range(gb):
        r_ref[g] = _dot(v_ref[g], w1_ref[g])
    xs = [(x0_ref[g], x0s_ref[g]) for g in range(gb)]
    starts = [[] for _ in range(gb)]
    for ck in range(nck):
        rows = slice(ck * bm, (ck + 1) * bm)
        for g in range(gb):
            x, xsw = xs[g]
            starts[g].append(x)
            xn = l1_ref[g] * x + l2_ref[g] * xsw + r_ref[g, rows, n:n + ns]
            xswn = l1_ref[g] * xsw + l2s_ref[g] * x + r_ref[g, rows, n + ns:n + 2 * ns]
            xs[g] = (xn, xswn)
    for g in range(gb):
        xe_ref[g] = xs[g][0]
        xstart = jnp.concatenate(starts[g], axis=0).astype(_BF16)
        yv_ref[g] = r_ref[g, :, :n] + _dot(xstart, w2_ref[g])


def _s5_out_kernel(x_ref, yv_ref, mod_ref, gpre_ref, gpost_ref, dsk_ref, ga_ref, gb_ref, y_ref, *, layer, bm):
    t, d = x_ref.shape
    lrow = slice(layer, layer + 1)
    nck = t // (S5_CHUNK * bm)
    x = x_ref[...]
    shift, scale, gate = _mod3(mod_ref, 0, d)
    h = _pre_mod(x, gpre_ref[lrow, :], scale, shift, bm)
    get = lambda g, hh: yv_ref[g, :, hh * V7X_LANES:(hh + 1) * V7X_LANES]
    y = _from_group_major(get, nck, bm, d) + dsk_ref[...] * h
    g = _gelu(y).astype(_BF16)
    out = _dot(g, ga_ref[...]) * jax.nn.sigmoid(_dot(g, gb_ref[...]))
    y_ref[...] = _gated_residual(x, out, gpost_ref[lrow, :], gate, bm)


def _s5_chunked_layer(x, mod, x0, x0s, g_pre, g_post, w1, w2, l1, l2, l2s, dskip, glu_a, glu_b,
                      *, layer, bm, tt, gb):
    r, d = x.shape
    t = tt * bm
    groups, n, _ = w1.shape
    nt = r // t
    mrows = r // S5_CHUNK
    mt = t // S5_CHUNK
    params = pltpu.CompilerParams(dimension_semantics=("arbitrary",), vmem_limit_bytes=VMEM_LIMIT_BYTES)
    v = pl.pallas_call(
        functools.partial(_s5_in_kernel, layer=layer, bm=bm),
        grid=(nt,),
        in_specs=[pl.BlockSpec((t, d), lambda i: (i, 0)), _layer_slab(mod.shape, layer), _whole(g_pre.shape)],
        out_specs=pl.BlockSpec((groups, mt, n), lambda i: (0, i, 0)),
        out_shape=jax.ShapeDtypeStruct((groups, mrows, n), _BF16),
        compiler_params=params,
        name="s5_in",
    )(x, mod, g_pre)
    gspec = lambda a: pl.BlockSpec((gb,) + a.shape[1:], lambda i: (i,) + (0,) * (a.ndim - 1))
    yv, xend = pl.pallas_call(
        functools.partial(_s5_core_kernel, bm=bm),
        grid=(groups // gb,),
        in_specs=[gspec(v), gspec(w1), gspec(w2), gspec(l1), gspec(l2), gspec(l2s), gspec(x0), gspec(x0s)],
        out_specs=[pl.BlockSpec((gb, mrows, n), lambda i: (i, 0, 0)),
                   pl.BlockSpec((gb, bm, x0.shape[-1]), lambda i: (i, 0, 0))],
        out_shape=[jax.ShapeDtypeStruct((groups, mrows, n), _F32),
                   jax.ShapeDtypeStruct(x0.shape, _F32)],
        scratch_shapes=[pltpu.VMEM((gb, mrows, w1.shape[-1]), _F32)],
        compiler_params=params,
        name="s5_core",
    )(v, w1, w2, l1, l2, l2s, x0, x0s)
    y = pl.pallas_call(
        functools.partial(_s5_out_kernel, layer=layer, bm=bm),
        grid=(nt,),
        in_specs=[pl.BlockSpec((t, d), lambda i: (i, 0)),
                  pl.BlockSpec((groups, mt, n), lambda i: (0, i, 0)),
                  _layer_slab(mod.shape, layer), _whole(g_pre.shape), _whole(g_post.shape),
                  _whole(dskip.shape), _whole(glu_a.shape), _whole(glu_b.shape)],
        out_specs=pl.BlockSpec((t, d), lambda i: (i, 0)),
        out_shape=jax.ShapeDtypeStruct((r, d), _F32),
        compiler_params=params,
        name="s5_out",
    )(x, yv, mod, g_pre, g_post, dskip, glu_a, glu_b)
    return y, xend


def _time_major(a):
    b, l, c = a.shape
    return jnp.swapaxes(a, 0, 1).reshape(l * b, c)


def _batch_major(a, b):
    lb, c = a.shape
    return jnp.swapaxes(a.reshape(lb // b, b, c), 0, 1)


def kernel(x_prompt, x_sample, c_prompt, c_sample, state_pool, state_ssm_re, state_ssm_im, state_ffn_conv, ada_w, ada_b, mix_pre_g, mix_post_g, ffn_pre_g, ffn_post_g, pool_w, pool_scale, ssm_A_re, ssm_A_im, ssm_log_dt, ssm_B_re, ssm_B_im, ssm_C_re, ssm_C_im, ssm_D, ssm_glu_a, ssm_glu_b, ffn_w_up, ffn_conv_w, ffn_conv_b, ffn_w_down):
    depth = ada_w.shape[0]
    bp, lp, d = x_prompt.shape
    bs, ls, _ = x_sample.shape
    nup = ffn_w_up.shape[-1]
    groups = ssm_A_re.shape[1]
    nst = groups * SSM_P
    gpt = V7X_MXU_DIM // SSM_GC

    mod_p, mod_s = _ada_mod(c_prompt, c_sample, ada_w, ada_b)
    gains = (mix_pre_g, mix_post_g, ffn_pre_g, ffn_post_g)
    w_up = ffn_w_up.astype(_BF16)
    w_dn = ffn_w_down.astype(_BF16)

    geo = {
        "p": dict(bm=bp, ffn_tt=64, ffn_nsub=1, s5_tt=64, pos_base=0, fuse_pool=True),
        "s": dict(bm=bs, ffn_tt=ls // 2, ffn_nsub=1, s5_tt=ls // 4, pos_base=PAST_LEN, fuse_pool=False, pool_tt=ls),
    }
    ys = {"p": x_prompt, "s": _time_major(x_sample)}
    mods = {"p": mod_p, "s": mod_s}
    pool_out = {"p": [], "s": []}
    ssm_out = {"p": [], "s": []}
    conv_out = {"p": [], "s": []}

    for l in range(depth):
        j = l // 2
        pool = {"p": None, "s": None}
        if l % 2 == 0:
            pw = pool_w[j].astype(_BF16)
            ps = pool_scale[j].reshape(1, d)
            pool["p"] = (jnp.zeros((POOL_STATE * bp, d), _F32), pw, ps)
            pool["s"] = (_time_major(state_pool[j]), pw, ps)
        else:
            lb_re, lb_im, bbt_re, bbt_im, nct_im = _s5_prep(
                ssm_A_re[j], ssm_A_im[j], ssm_log_dt[j], ssm_B_re[j], ssm_B_im[j], ssm_C_im[j])
            ct_re = jnp.swapaxes(ssm_C_re[j], 1, 2)
            wb = jnp.concatenate([_block_diag_tiles(bbt_re, gpt), _block_diag_tiles(bbt_im, gpt)],
                                 axis=2).astype(_BF16)
            wc = jnp.concatenate([_block_diag_tiles(ct_re, gpt), _block_diag_tiles(nct_im, gpt)],
                                 axis=1).astype(_BF16)
            lbr = lb_re.reshape(1, nst)
            lbi = lb_im.reshape(1, nst)
            ga = ssm_glu_a[j].astype(_BF16)
            gb = ssm_glu_b[j].astype(_BF16)
            dskip = ssm_D[j].reshape(1, d)
            w1, w2, l1, l2, l2s = _s5_chunk_weights(ssm_A_re[j], ssm_A_im[j], ssm_log_dt[j], ssm_B_re[j],
                                                    ssm_B_im[j], ssm_C_re[j], ssm_C_im[j])
            x0 = jnp.zeros((groups, bp, 2 * SSM_P), _F32)
            ys["p"], xe = _s5_chunked_layer(ys["p"], mods["p"], x0, x0, mix_pre_g, mix_post_g, w1, w2, l1, l2, l2s,
                                            dskip, ga, gb, layer=l, bm=bp, tt=geo["p"]["s5_tt"], gb=4)
            xe = jnp.swapaxes(xe, 0, 1)
            ssm_out["p"].append((xe[..., :SSM_P], xe[..., SSM_P:]))
            xr0 = state_ssm_re[j].reshape(bs, nst)
            xi0 = state_ssm_im[j].reshape(bs, nst)
            ys["s"], xr, xi = _s5_layer(ys["s"], mods["s"], xr0, xi0, mix_pre_g, mix_post_g,
                                        wb, wc, lbr, lbi, dskip, ga, gb,
                                        layer=l, bm=bs, tt=geo["s"]["s5_tt"])
            ssm_out["s"].append((xr.reshape(bs, groups, SSM_P), xi.reshape(bs, groups, SSM_P)))

        for k in ("p", "s"):
            gk = geo[k]
            if k == "p":
                cv0 = jnp.zeros(((CONV_W - 1) * bp, nup), _F32)
            else:
                cv0 = _time_major(state_ffn_conv[l])
            ffn = (cv0, w_up, ffn_conv_w, ffn_conv_b, w_dn)
            common = dict(layer=l, bm=gk["bm"], pos_base=gk["pos_base"])
            if pool[k] is not None and not gk["fuse_pool"]:
                ys[k], st = _token_layer(ys[k], mods[k], gains, tt=gk["pool_tt"], pool=pool[k], **common)
                pool_out[k].append(_batch_major(st, gk["bm"]))
                pool[k] = None
            outs = _token_layer(ys[k], mods[k], gains, tt=gk["ffn_tt"], nsub=gk["ffn_nsub"], pool=pool[k], ffn=ffn,
                                in_batch_major=(k == "p" and l == 0),
                                out_batch_major=(k == "p" and l == depth - 1), **common)
            ys[k], cv = outs[0], outs[-1]
            if pool[k] is not None:
                pool_out[k].append(_batch_major(outs[1], gk["bm"]))
            conv_out[k].append(_batch_major(cv, gk["bm"]))

    y_prompt = ys["p"]
    y_sample = _batch_major(ys["s"], bs)
    return (y_prompt, y_sample,
            jnp.stack(pool_out["p"]), jnp.stack(pool_out["s"]),
            jnp.stack([a for a, _ in ssm_out["p"]]), jnp.stack([b for _, b in ssm_out["p"]]),
            jnp.stack([a for a, _ in ssm_out["s"]]), jnp.stack([b for _, b in ssm_out["s"]]),
            jnp.stack(conv_out["p"]), jnp.stack(conv_out["s"]))
```

```python
import functools
import math

import jax
import jax.numpy as jnp
from jax import lax
from jax.experimental import pallas as pl
from jax.experimental.pallas import tpu as pltpu

POOL_WINDOWS = (2, 4, 8, 16)
POOL_STATE = max(POOL_WINDOWS) - 1
SSM_GC = 16
SSM_P = 64
CONV_W = 3
EPS = 1e-6
PAST_LEN = 16384

V7X_SUBLANES = 8
V7X_LANES = 128
V7X_MXU_DIM = 256
VMEM_LIMIT_BYTES = 56 * 1024 * 1024

_F32 = jnp.float32
_BF16 = jnp.bfloat16


def _resident(block_shape, index_map):
    return pl.BlockSpec(block_shape, index_map, pipeline_mode=pl.Buffered(1))


def _whole(shape):
    nd = len(shape)
    return _resident(shape, lambda i: (0,) * nd)


def _layer_slab(shape, l):
    nd = len(shape)
    return _resident((None,) + tuple(shape[1:]), lambda i: (l,) + (0,) * (nd - 1))


def _rmsnorm(x, g):
    return x * lax.rsqrt(jnp.mean(x * x, axis=-1, keepdims=True) + EPS) * g


def _pre_mod(x, g, scale, shift, bm):
    t, d = x.shape
    r = _rmsnorm(x, g).reshape(t // bm, bm, d)
    return (r * (1.0 + scale)[None] + shift[None]).reshape(t, d)


def _gated_residual(x, m, g, gate, bm):
    t, d = x.shape
    r = _rmsnorm(m, g).reshape(t // bm, bm, d)
    return x + (gate[None] * r).reshape(t, d)


def _gelu(x):
    return 0.5 * x * (1.0 + lax.erf(x * math.sqrt(0.5)))


def _dot(a, b):
    return jnp.dot(a, b, preferred_element_type=_F32)


def _load_rows(x_ref, batch_major):
    if not batch_major:
        return x_ref[...]
    b, tt, d = x_ref.shape
    return jnp.swapaxes(x_ref[...], 0, 1).reshape(tt * b, d)


def _store_rows(y_ref, y, batch_major):
    if not batch_major:
        y_ref[...] = y
    else:
        b, tt, d = y_ref.shape
        y_ref[...] = jnp.swapaxes(y.reshape(tt, b, d), 0, 1)


def _mod3(mod_ref, k, d):
    return tuple(mod_ref[:, (3 * k + n) * d:(3 * k + n + 1) * d] for n in range(3))


def _ada_kernel(cp_ref, cs_ref, w_ref, b_ref, mp_ref, ms_ref):
    bp = cp_ref.shape[0]
    c = jnp.concatenate([cp_ref[...], cs_ref[...]], axis=0)
    s = (c * jax.nn.sigmoid(c)).astype(_BF16)
    o = _dot(s, w_ref[...].astype(_BF16)) + b_ref[...]
    mp_ref[...] = o[:bp]
    ms_ref[...] = o[bp:]


def _ada_mod(c_prompt, c_sample, ada_w, ada_b):
    depth, d, n = ada_w.shape
    tn = 1536
    bp, bs = c_prompt.shape[0], c_sample.shape[0]
    return pl.pallas_call(
        _ada_kernel,
        grid=(depth, n // tn),
        in_specs=[
            pl.BlockSpec((bp, d), lambda l, j: (0, 0)),
            pl.BlockSpec((bs, d), lambda l, j: (0, 0)),
            pl.BlockSpec((None, d, tn), lambda l, j: (l, 0, j)),
            pl.BlockSpec((None, 1, tn), lambda l, j: (l, 0, j)),
        ],
        out_specs=[
            pl.BlockSpec((None, bp, tn), lambda l, j: (l, 0, j)),
            pl.BlockSpec((None, bs, tn), lambda l, j: (l, 0, j)),
        ],
        out_shape=[
            jax.ShapeDtypeStruct((depth, bp, n), _F32),
            jax.ShapeDtypeStruct((depth, bs, n), _F32),
        ],
        compiler_params=pltpu.CompilerParams(
            dimension_semantics=("arbitrary", "arbitrary"),
            vmem_limit_bytes=VMEM_LIMIT_BYTES),
        name="ada_mod",
    )(c_prompt, c_sample, ada_w, ada_b.reshape(depth, 1, n))


def _pool_mix(h, st_ref, pw_ref, ps_ref, i, *, bm, pos_base):
    t, d = h.shape
    tt = t // bm
    gc = d // len(POOL_WINDOWS)
    ext = jnp.concatenate([st_ref[...], h], axis=0)
    st_ref[...] = ext[t:, :]
    pos = None
    if pos_base + 1 < max(POOL_WINDOWS):
        row = lax.broadcasted_iota(jnp.int32, (t, V7X_LANES), 0)
        pos = pos_base + i * tt + lax.shift_right_logical(row, bm.bit_length() - 1)
    parts = []
    for gi, w in enumerate(POOL_WINDOWS):
        cols = slice(gi * gc, (gi + 1) * gc)
        s = ext[:, cols]
        span = 1
        while span < w:
            n = s.shape[0]
            s = s[span * bm:] + s[:n - span * bm]
            span *= 2
        k0 = (POOL_STATE - (w - 1)) * bm
        wsum = s[k0:k0 + t]
        if pos is None:
            inv = 1.0 / w
        else:
            inv = 1.0 / jnp.minimum(pos + 1, w).astype(_F32)
            inv = jnp.concatenate([inv] * (gc // V7X_LANES), axis=1)
        pooled = wsum * inv - h[:, cols]
        parts.append(_dot(pooled.astype(_BF16), pw_ref[gi]))
    return jnp.concatenate(parts, axis=1) * ps_ref[...]


def _conv_ffn(f, cv_ref, wup_ref, cw_ref, cb_ref, wdn_ref, h_ref, *, bm):
    t = f.shape[0]
    hid = wdn_ref.shape[0]
    hc = V7X_MXU_DIM

    def conv_cols(c0):
        cols = slice(c0, c0 + hc)
        up = _dot(f, wup_ref[:, cols])
        ext = jnp.concatenate([cv_ref[:, cols], up], axis=0)
        cv_ref[:, cols] = ext[t:, :]
        conv = cb_ref[:, cols]
        for k in range(CONV_W):
            conv = conv + ext[k * bm:k * bm + t, :] * cw_ref[k:k + 1, cols]
        return conv

    for c in range(hid // hc):
        gate_c = conv_cols(c * hc)
        val_c = conv_cols(hid + c * hc)
        h_ref[:, c * hc:(c + 1) * hc] = (_gelu(gate_c) * val_c).astype(_BF16)
    return _dot(h_ref[...], wdn_ref[...])


def _layer_kernel(*refs, layer, bm, pos_base, with_pool, with_ffn, in_batch_major, out_batch_major):
    refs = list(refs)
    x_ref, mod_ref = refs[:2]
    del refs[:2]
    if with_pool:
        st0_ref, mpre_ref, mpost_ref, pw_ref, ps_ref = refs[:5]
        del refs[:5]
    if with_ffn:
        cv0_ref, fpre_ref, fpost_ref, wup_ref, cw_ref, cb_ref, wdn_ref = refs[:7]
        del refs[:7]
    y_ref = refs.pop(0)
    if with_pool:
        st_ref = refs.pop(0)
    if with_ffn:
        cv_ref, h_ref = refs
    i = pl.program_id(0)
    d = mod_ref.shape[1] // 6
    lrow = slice(layer, layer + 1)

    @pl.when(i == 0)
    def _():
        if with_pool:
            st_ref[...] = st0_ref[...]
        if with_ffn:
            cv_ref[...] = cv0_ref[...]

    x = _load_rows(x_ref, in_batch_major)
    if with_pool:
        shift, scale, gate = _mod3(mod_ref, 0, d)
        h = _pre_mod(x, mpre_ref[lrow, :], scale, shift, bm)
        m = _pool_mix(h, st_ref, pw_ref, ps_ref, i, bm=bm, pos_base=pos_base)
        x = _gated_residual(x, m, mpost_ref[lrow, :], gate, bm)
    if with_ffn:
        shift, scale, gate = _mod3(mod_ref, 1, d)
        f = _pre_mod(x, fpre_ref[lrow, :], scale, shift, bm).astype(_BF16)
        o = _conv_ffn(f, cv_ref, wup_ref, cw_ref, cb_ref.at[lrow, :], wdn_ref, h_ref, bm=bm)
        x = _gated_residual(x, o, fpost_ref[lrow, :], gate, bm)
    _store_rows(y_ref, x, out_batch_major)


def _token_layer(x, mod, gains, *, layer, bm, tt, pos_base=0, pool=None, ffn=None,
                 in_batch_major=False, out_batch_major=False):
    mix_pre_g, mix_post_g, ffn_pre_g, ffn_post_g = gains
    d = x.shape[-1]
    nrows = x.shape[0] * x.shape[1] if in_batch_major else x.shape[0]
    t = tt * bm
    tm_spec = pl.BlockSpec((t, d), lambda i: (i, 0))
    bm_spec = pl.BlockSpec((bm, tt, d), lambda i: (0, i, 0))

    args = [x, mod]
    in_specs = [bm_spec if in_batch_major else tm_spec, _layer_slab(mod.shape, layer)]
    out_specs = [bm_spec if out_batch_major else tm_spec]
    out_shape = [jax.ShapeDtypeStruct((bm, nrows // bm, d) if out_batch_major else (nrows, d), _F32)]
    scratch = []
    if pool is not None:
        st0, pool_w, pool_scale = pool
        args += [st0, mix_pre_g, mix_post_g, pool_w, pool_scale]
        in_specs += [_whole(a.shape) for a in args[-5:]]
        out_specs += [_whole(st0.shape)]
        out_shape += [jax.ShapeDtypeStruct(st0.shape, _F32)]
    if ffn is not None:
        cv0, w_up, conv_w, conv_b, w_down = ffn
        hid = w_down.shape[1]
        assert hid % V7X_MXU_DIM == 0 and cv0.shape[1] == 2 * hid
        args += [cv0, ffn_pre_g, ffn_post_g, w_up, conv_w, conv_b, w_down]
        in_specs += [_whole(cv0.shape), _whole(ffn_pre_g.shape), _whole(ffn_post_g.shape),
                     _layer_slab(w_up.shape, layer), _layer_slab(conv_w.shape, layer),
                     _whole(conv_b.shape), _layer_slab(w_down.shape, layer)]
        out_specs += [_whole(cv0.shape)]
        out_shape += [jax.ShapeDtypeStruct(cv0.shape, _F32)]
        scratch += [pltpu.VMEM((t, hid), _BF16)]
    return pl.pallas_call(
        functools.partial(_layer_kernel, layer=layer, bm=bm, pos_base=pos_base,
                          with_pool=pool is not None, with_ffn=ffn is not None,
                          in_batch_major=in_batch_major, out_batch_major=out_batch_major),
        grid=(nrows // t,),
        in_specs=in_specs,
        out_specs=out_specs,
        out_shape=out_shape,
        scratch_shapes=scratch,
        compiler_params=pltpu.CompilerParams(
            dimension_semantics=("arbitrary",), vmem_limit_bytes=VMEM_LIMIT_BYTES),
        name="_".join(n for n, on in (("pool", pool), ("ffn", ffn)) if on is not None) + "_layer",
    )(*args)


def _s5_prep_kernel(are_ref, aim_ref, ldt_ref, btr_ref, bti_ref, cti_ref,
                    lbr_ref, lbi_ref, bbr_ref, bbi_ref, ncti_ref):
    a_re, a_im = are_ref[...], aim_ref[...]
    dt = jnp.exp(ldt_ref[...])
    mag = jnp.exp(a_re * dt)
    ang = a_im * dt
    lb_re = mag * jnp.cos(ang)
    lb_im = mag * jnp.sin(ang)
    n_re = lb_re - 1.0
    n_im = lb_im
    den = a_re * a_re + a_im * a_im
    f_re = (n_re * a_re + n_im * a_im) / den
    f_im = (n_im * a_re - n_re * a_im) / den
    lbr_ref[...] = lb_re
    lbi_ref[...] = lb_im
    b_re, b_im = btr_ref[...], bti_ref[...]
    bbr_ref[...] = f_re[:, None, :] * b_re - f_im[:, None, :] * b_im
    bbi_ref[...] = f_re[:, None, :] * b_im + f_im[:, None, :] * b_re
    ncti_ref[...] = -cti_ref[...]


def _s5_prep(a_re, a_im, log_dt, b_re, b_im, c_im):
    g, p = a_re.shape
    gc = b_re.shape[-1]
    bt_re = jnp.swapaxes(b_re, 1, 2)
    bt_im = jnp.swapaxes(b_im, 1, 2)
    ct_im = jnp.swapaxes(c_im, 1, 2)
    return pl.pallas_call(
        _s5_prep_kernel,
        out_shape=[
            jax.ShapeDtypeStruct((g, p), _F32),
            jax.ShapeDtypeStruct((g, p), _F32),
            jax.ShapeDtypeStruct((g, gc, p), _F32),
            jax.ShapeDtypeStruct((g, gc, p), _F32),
            jax.ShapeDtypeStruct((g, p, gc), _F32),
        ],
        name="s5_prep",
    )(a_re, a_im, log_dt.reshape(g, 1), bt_re, bt_im, ct_im)


def _block_diag_tiles(w, gpt):
    g, a, b = w.shape
    nt = g // gpt
    w = w.reshape(nt, gpt, a, 1, b)
    eye = jnp.eye(gpt, dtype=bool)[None, :, None, :, None]
    return jnp.where(eye, w, 0.0).reshape(nt, gpt * a, gpt * b)


def _s5_kernel(x_ref, mod_ref, xr0_ref, xi0_ref, gpre_ref, gpost_ref, wb_ref, wc_ref,
               lbr_ref, lbi_ref, dsk_ref, ga_ref, gb_ref,
               y_ref, xr_ref, xi_ref, *, layer, bm):
    i = pl.program_id(0)
    t, d = x_ref.shape
    tt = t // bm
    ntile, kin, ncol2 = wb_ref.shape
    ncol = ncol2 // 2
    sub = V7X_SUBLANES
    lrow = slice(layer, layer + 1)

    @pl.when(i == 0)
    def _():
        xr_ref[...] = xr0_ref[...]
        xi_ref[...] = xi0_ref[...]

    x = x_ref[...]
    shift, scale, gate = _mod3(mod_ref, 0, d)
    h = _pre_mod(x, gpre_ref[lrow, :], scale, shift, bm)
    u = h.astype(_BF16)

    ys = []
    for j in range(ntile):
        scol = slice(j * ncol, (j + 1) * ncol)
        bu = _dot(u[:, j * kin:(j + 1) * kin], wb_ref[j])
        lr = jnp.broadcast_to(lbr_ref[:, scol], (sub, ncol))
        li = jnp.broadcast_to(lbi_ref[:, scol], (sub, ncol))
        blocks = [None] * (t // sub)
        for rb in range(bm // sub):
            rows = slice(rb * sub, (rb + 1) * sub)
            pr, pi = xr_ref[rows, scol], xi_ref[rows, scol]
            for ts in range(tt):
                r0 = ts * bm + rb * sub
                nr = lr * pr - li * pi + bu[r0:r0 + sub, :ncol]
                ni = lr * pi + li * pr + bu[r0:r0 + sub, ncol:]
                blocks[r0 // sub] = jnp.concatenate([nr, ni], axis=1)
                pr, pi = nr, ni
            xr_ref[rows, scol] = pr
            xi_ref[rows, scol] = pi
        ys.append(_dot(jnp.concatenate(blocks, axis=0).astype(_BF16), wc_ref[j]))
    y = jnp.concatenate(ys, axis=1) + dsk_ref[...] * h
    g = _gelu(y).astype(_BF16)
    out = _dot(g, ga_ref[...]) * jax.nn.sigmoid(_dot(g, gb_ref[...]))
    y_ref[...] = _gated_residual(x, out, gpost_ref[lrow, :], gate, bm)


def _s5_layer(x, mod, xr0, xi0, g_pre, g_post, wb, wc, lbr, lbi, dskip, glu_a, glu_b, *, layer, bm, tt):
    r, d = x.shape
    t = tt * bm
    nst = xr0.shape[1]
    return pl.pallas_call(
        functools.partial(_s5_kernel, layer=layer, bm=bm),
        grid=(r // t,),
        in_specs=[
            pl.BlockSpec((t, d), lambda i: (i, 0)),
            _layer_slab(mod.shape, layer),
            _whole((bm, nst)),
            _whole((bm, nst)),
            _whole(g_pre.shape),
            _whole(g_post.shape),
            _whole(wb.shape),
            _whole(wc.shape),
            _whole((1, nst)),
            _whole((1, nst)),
            _whole((1, d)),
            _whole(glu_a.shape),
            _whole(glu_b.shape),
        ],
        out_specs=[
            pl.BlockSpec((t, d), lambda i: (i, 0)),
            _whole((bm, nst)),
            _whole((bm, nst)),
        ],
        out_shape=[
            jax.ShapeDtypeStruct((r, d), _F32),
            jax.ShapeDtypeStruct((bm, nst), _F32),
            jax.ShapeDtypeStruct((bm, nst), _F32),
        ],
        compiler_params=pltpu.CompilerParams(
            dimension_semantics=("arbitrary",), vmem_limit_bytes=VMEM_LIMIT_BYTES),
        name="s5_layer",
    )(x, mod, xr0, xi0, g_pre, g_post, wb, wc, lbr, lbi, dskip, glu_a, glu_b)


def _time_major(a):
    b, l, c = a.shape
    return jnp.swapaxes(a, 0, 1).reshape(l * b, c)


def _batch_major(a, b):
    lb, c = a.shape
    return jnp.swapaxes(a.reshape(lb // b, b, c), 0, 1)


def kernel(x_prompt, x_sample, c_prompt, c_sample, state_pool, state_ssm_re, state_ssm_im, state_ffn_conv, ada_w, ada_b, mix_pre_g, mix_post_g, ffn_pre_g, ffn_post_g, pool_w, pool_scale, ssm_A_re, ssm_A_im, ssm_log_dt, ssm_B_re, ssm_B_im, ssm_C_re, ssm_C_im, ssm_D, ssm_glu_a, ssm_glu_b, ffn_w_up, ffn_conv_w, ffn_conv_b, ffn_w_down):
    depth = ada_w.shape[0]
    bp, lp, d = x_prompt.shape
    bs, ls, _ = x_sample.shape
    nup = ffn_w_up.shape[-1]
    groups = ssm_A_re.shape[1]
    nst = groups * SSM_P
    gpt = V7X_MXU_DIM // SSM_GC

    mod_p, mod_s = _ada_mod(c_prompt, c_sample, ada_w, ada_b)
    gains = (mix_pre_g, mix_post_g, ffn_pre_g, ffn_post_g)
    w_up = ffn_w_up.astype(_BF16)
    w_dn = ffn_w_down.astype(_BF16)

    geo = {
        "p": dict(bm=bp, ffn_tt=128, s5_tt=64, pos_base=0, fuse_pool=True),
        "s": dict(bm=bs, ffn_tt=ls // 2, s5_tt=ls // 4, pos_base=PAST_LEN, fuse_pool=False, pool_tt=ls),
    }
    ys = {"p": x_prompt, "s": _time_major(x_sample)}
    mods = {"p": mod_p, "s": mod_s}
    pool_out = {"p": [], "s": []}
    ssm_out = {"p": [], "s": []}
    conv_out = {"p": [], "s": []}

    for l in range(depth):
        j = l // 2
        pool = {"p": None, "s": None}
        if l % 2 == 0:
            pw = pool_w[j].astype(_BF16)
            ps = pool_scale[j].reshape(1, d)
            pool["p"] = (jnp.zeros((POOL_STATE * bp, d), _F32), pw, ps)
            pool["s"] = (_time_major(state_pool[j]), pw, ps)
        else:
            lb_re, lb_im, bbt_re, bbt_im, nct_im = _s5_prep(
                ssm_A_re[j], ssm_A_im[j], ssm_log_dt[j], ssm_B_re[j], ssm_B_im[j], ssm_C_im[j])
            ct_re = jnp.swapaxes(ssm_C_re[j], 1, 2)
            wb = jnp.concatenate([_block_diag_tiles(bbt_re, gpt), _block_diag_tiles(bbt_im, gpt)],
                                 axis=2).astype(_BF16)
            wc = jnp.concatenate([_block_diag_tiles(ct_re, gpt), _block_diag_tiles(nct_im, gpt)],
                                 axis=1).astype(_BF16)
            lbr = lb_re.reshape(1, nst)
            lbi = lb_im.reshape(1, nst)
            ga = ssm_glu_a[j].astype(_BF16)
            gb = ssm_glu_b[j].astype(_BF16)
            for k in ("p", "s"):
                gk = geo[k]
                if k == "p":
                    xr0 = jnp.zeros((bp, nst), _F32)
                    xi0 = xr0
                else:
                    xr0 = state_ssm_re[j].reshape(bs, nst)
                    xi0 = state_ssm_im[j].reshape(bs, nst)
                ys[k], xr, xi = _s5_layer(ys[k], mods[k], xr0, xi0, mix_pre_g, mix_post_g,
                                          wb, wc, lbr, lbi, ssm_D[j].reshape(1, d), ga, gb,
                                          layer=l, bm=gk["bm"], tt=gk["s5_tt"])
                ssm_out[k].append((xr.reshape(gk["bm"], groups, SSM_P), xi.reshape(gk["bm"], groups, SSM_P)))

        for k in ("p", "s"):
            gk = geo[k]
            if k == "p":
                cv0 = jnp.zeros(((CONV_W - 1) * bp, nup), _F32)
            else:
                cv0 = _time_major(state_ffn_conv[l])
            ffn = (cv0, w_up, ffn_conv_w, ffn_conv_b, w_dn)
            common = dict(layer=l, bm=gk["bm"], pos_base=gk["pos_base"])
            if pool[k] is not None and not gk["fuse_pool"]:
                ys[k], st = _token_layer(ys[k], mods[k], gains, tt=gk["pool_tt"], pool=pool[k], **common)
                pool_out[k].append(_batch_major(st, gk["bm"]))
                pool[k] = None
            outs = _token_layer(ys[k], mods[k], gains, tt=gk["ffn_tt"], pool=pool[k], ffn=ffn,
                                in_batch_major=(k == "p" and l == 0),
                                out_batch_major=(k == "p" and l == depth - 1), **common)
            ys[k], cv = outs[0], outs[-1]
            if pool[k] is not None:
                pool_out[k].append(_batch_major(outs[1], gk["bm"]))
            conv_out[k].append(_batch_major(cv, gk["bm"]))

    y_prompt = ys["p"]
    y_sample = _batch_major(ys["s"], bs)
    return (y_prompt, y_sample,
            jnp.stack(pool_out["p"]), jnp.stack(pool_out["s"]),
            jnp.stack([a for a, _ in ssm_out["p"]]), jnp.stack([b for _, b in ssm_out["p"]]),
            jnp.stack([a for a, _ in ssm_out["s"]]), jnp.stack([b for _, b in ssm_out["s"]]),
            jnp.stack(conv_out["p"]), jnp.stack(conv_out["s"]))
```

```python
import functools
import math

import jax
import jax.numpy as jnp
from jax import lax
from jax.experimental import pallas as pl
from jax.experimental.pallas import tpu as pltpu

POOL_WINDOWS = (2, 4, 8, 16)
POOL_STATE = max(POOL_WINDOWS) - 1
SSM_GC = 16
SSM_P = 64
CONV_W = 3
EPS = 1e-6
PAST_LEN = 16384

V7X_SUBLANES = 8
V7X_LANES = 128
V7X_MXU_DIM = 256
VMEM_LIMIT_BYTES = 56 * 1024 * 1024

_F32 = jnp.float32
_BF16 = jnp.bfloat16


def _resident(block_shape, index_map):
    return pl.BlockSpec(block_shape, index_map, pipeline_mode=pl.Buffered(1))


def _whole(shape):
    nd = len(shape)
    return _resident(shape, lambda i: (0,) * nd)


def _layer_slab(shape, l):
    nd = len(shape)
    return _resident((None,) + tuple(shape[1:]), lambda i: (l,) + (0,) * (nd - 1))


def _rmsnorm(x, g):
    return x * lax.rsqrt(jnp.mean(x * x, axis=-1, keepdims=True) + EPS) * g


def _pre_mod(x, g, scale, shift, bm):
    t, d = x.shape
    r = _rmsnorm(x, g).reshape(t // bm, bm, d)
    return (r * (1.0 + scale)[None] + shift[None]).reshape(t, d)


def _gated_residual(x, m, g, gate, bm):
    t, d = x.shape
    r = _rmsnorm(m, g).reshape(t // bm, bm, d)
    return x + (gate[None] * r).reshape(t, d)


def _gelu(x):
    return 0.5 * x * (1.0 + lax.erf(x * math.sqrt(0.5)))


def _dot(a, b):
    return jnp.dot(a, b, preferred_element_type=_F32)


def _load_rows(x_ref, batch_major):
    if not batch_major:
        return x_ref[...]
    b, tt, d = x_ref.shape
    return jnp.swapaxes(x_ref[...], 0, 1).reshape(tt * b, d)


def _store_rows(y_ref, y, batch_major):
    if not batch_major:
        y_ref[...] = y
    else:
        b, tt, d = y_ref.shape
        y_ref[...] = jnp.swapaxes(y.reshape(tt, b, d), 0, 1)


def _mod3(mod_ref, k, d):
    return tuple(mod_ref[:, (3 * k + n) * d:(3 * k + n + 1) * d] for n in range(3))


def _ada_kernel(cp_ref, cs_ref, w_ref, b_ref, mp_ref, ms_ref):
    k = pl.program_id(1)
    bp = cp_ref.shape[0]

    @pl.when(k == 0)
    def _():
        mp_ref[...] = jnp.broadcast_to(b_ref[...], mp_ref.shape)
        ms_ref[...] = jnp.broadcast_to(b_ref[...], ms_ref.shape)

    c = jnp.concatenate([cp_ref[...], cs_ref[...]], axis=0)
    s = (c * jax.nn.sigmoid(c)).astype(_BF16)
    o = _dot(s, w_ref[...].astype(_BF16))
    mp_ref[...] += o[:bp]
    ms_ref[...] += o[bp:]


def _ada_mod(c_prompt, c_sample, ada_w, ada_b):
    depth, d, n = ada_w.shape
    tk = V7X_MXU_DIM
    bp, bs = c_prompt.shape[0], c_sample.shape[0]
    return pl.pallas_call(
        _ada_kernel,
        grid=(depth, d // tk),
        in_specs=[
            pl.BlockSpec((bp, tk), lambda l, k: (0, k)),
            pl.BlockSpec((bs, tk), lambda l, k: (0, k)),
            pl.BlockSpec((None, tk, n), lambda l, k: (l, k, 0)),
            pl.BlockSpec((None, 1, n), lambda l, k: (l, 0, 0)),
        ],
        out_specs=[
            pl.BlockSpec((None, bp, n), lambda l, k: (l, 0, 0)),
            pl.BlockSpec((None, bs, n), lambda l, k: (l, 0, 0)),
        ],
        out_shape=[
            jax.ShapeDtypeStruct((depth, bp, n), _F32),
            jax.ShapeDtypeStruct((depth, bs, n), _F32),
        ],
        compiler_params=pltpu.CompilerParams(
            dimension_semantics=("arbitrary", "arbitrary"),
            vmem_limit_bytes=VMEM_LIMIT_BYTES),
        name="ada_mod",
    )(c_prompt, c_sample, ada_w, ada_b.reshape(depth, 1, n))


def _pool_mix(h, st, pw_ref, ps_ref, i, *, bm, pos_base):
    t, d = h.shape
    tt = t // bm
    gc = d // len(POOL_WINDOWS)
    ext = jnp.concatenate([st, h], axis=0)
    pos = None
    if pos_base + 1 < max(POOL_WINDOWS):
        row = lax.broadcasted_iota(jnp.int32, (t, V7X_LANES), 0)
        pos = pos_base + i * tt + lax.shift_right_logical(row, bm.bit_length() - 1)
    parts = []
    for gi, w in enumerate(POOL_WINDOWS):
        cols = slice(gi * gc, (gi + 1) * gc)
        s = ext[:, cols]
        span = 1
        while span < w:
            n = s.shape[0]
            s = s[span * bm:] + s[:n - span * bm]
            span *= 2
        k0 = (POOL_STATE - (w - 1)) * bm
        wsum = s[k0:k0 + t]
        if pos is None:
            inv = 1.0 / w
        else:
            inv = 1.0 / jnp.minimum(pos + 1, w).astype(_F32)
            inv = jnp.concatenate([inv] * (gc // V7X_LANES), axis=1)
        pooled = wsum * inv - h[:, cols]
        parts.append(_dot(pooled.astype(_BF16), pw_ref[gi]))
    return jnp.concatenate(parts, axis=1) * ps_ref[...], ext[t:, :]


def _conv_ffn(f, cv_ref, wup_ref, cw_ref, cb_ref, wdn_ref, h_ref, *, bm):
    t = f.shape[0]
    hid = wdn_ref.shape[0]
    hc = V7X_MXU_DIM

    def conv_cols(c0):
        cols = slice(c0, c0 + hc)
        up = _dot(f, wup_ref[:, cols])
        ext = jnp.concatenate([cv_ref[:, cols], up], axis=0)
        cv_ref[:, cols] = ext[t:, :]
        conv = cb_ref[:, cols]
        for k in range(CONV_W):
            conv = conv + ext[k * bm:k * bm + t, :] * cw_ref[k:k + 1, cols]
        return conv

    for c in range(hid // hc):
        gate_c = conv_cols(c * hc)
        val_c = conv_cols(hid + c * hc)
        h_ref[:, c * hc:(c + 1) * hc] = (_gelu(gate_c) * val_c).astype(_BF16)
    return _dot(h_ref[...], wdn_ref[...])


def _layer_kernel(*refs, layer, bm, pos_base, with_pool, with_ffn, in_batch_major, out_batch_major,
                  state_batch_major):
    refs = list(refs)
    x_ref, mod_ref = refs[:2]
    del refs[:2]
    if with_pool:
        st0_ref, mpre_ref, mpost_ref, pw_ref, ps_ref = refs[:5]
        del refs[:5]
    if with_ffn:
        cv0_ref, fpre_ref, fpost_ref, wup_ref, cw_ref, cb_ref, wdn_ref = refs[:7]
        del refs[:7]
    y_ref = refs.pop(0)
    if with_pool:
        st_ref = refs.pop(0)
    if with_ffn:
        cv_ref, h_ref = refs
    i = pl.program_id(0)
    d = mod_ref.shape[1] // 6
    lrow = slice(layer, layer + 1)

    @pl.when(i == 0)
    def _():
        if with_pool and not state_batch_major:
            st_ref[...] = st0_ref[...]
        if with_ffn:
            cv_ref[...] = cv0_ref[...]

    x = _load_rows(x_ref, in_batch_major)
    if with_pool:
        shift, scale, gate = _mod3(mod_ref, 0, d)
        h = _pre_mod(x, mpre_ref[lrow, :], scale, shift, bm)
        st = _load_rows(st0_ref, True) if state_batch_major else st_ref[...]
        m, st = _pool_mix(h, st, pw_ref, ps_ref, i, bm=bm, pos_base=pos_base)
        _store_rows(st_ref, st, state_batch_major)
        x = _gated_residual(x, m, mpost_ref[lrow, :], gate, bm)
    if with_ffn:
        shift, scale, gate = _mod3(mod_ref, 1, d)
        f = _pre_mod(x, fpre_ref[lrow, :], scale, shift, bm).astype(_BF16)
        o = _conv_ffn(f, cv_ref, wup_ref, cw_ref, cb_ref.at[lrow, :], wdn_ref, h_ref, bm=bm)
        x = _gated_residual(x, o, fpost_ref[lrow, :], gate, bm)
    _store_rows(y_ref, x, out_batch_major)


def _token_layer(x, mod, gains, *, layer, bm, tt, pos_base=0, pool=None, ffn=None,
                 in_batch_major=False, out_batch_major=False, state_batch_major=False):
    mix_pre_g, mix_post_g, ffn_pre_g, ffn_post_g = gains
    d = x.shape[-1]
    nrows = x.shape[0] * x.shape[1] if in_batch_major else x.shape[0]
    t = tt * bm
    assert not state_batch_major or nrows == t
    tm_spec = pl.BlockSpec((t, d), lambda i: (i, 0))
    bm_spec = pl.BlockSpec((bm, tt, d), lambda i: (0, i, 0))

    args = [x, mod]
    in_specs = [bm_spec if in_batch_major else tm_spec, _layer_slab(mod.shape, layer)]
    out_specs = [bm_spec if out_batch_major else tm_spec]
    out_shape = [jax.ShapeDtypeStruct((bm, nrows // bm, d) if out_batch_major else (nrows, d), _F32)]
    scratch = []
    if pool is not None:
        st0, pool_w, pool_scale = pool
        args += [st0, mix_pre_g, mix_post_g, pool_w, pool_scale]
        in_specs += [_whole(a.shape) for a in args[-5:]]
        out_specs += [_whole(st0.shape)]
        out_shape += [jax.ShapeDtypeStruct(st0.shape, _F32)]
    if ffn is not None:
        cv0, w_up, conv_w, conv_b, w_down = ffn
        hid = w_down.shape[1]
        assert hid % V7X_MXU_DIM == 0 and cv0.shape[1] == 2 * hid
        args += [cv0, ffn_pre_g, ffn_post_g, w_up, conv_w, conv_b, w_down]
        in_specs += [_whole(cv0.shape), _whole(ffn_pre_g.shape), _whole(ffn_post_g.shape),
                     _layer_slab(w_up.shape, layer), _layer_slab(conv_w.shape, layer),
                     _whole(conv_b.shape), _layer_slab(w_down.shape, layer)]
        out_specs += [_whole(cv0.shape)]
        out_shape += [jax.ShapeDtypeStruct(cv0.shape, _F32)]
        scratch += [pltpu.VMEM((t, hid), _BF16)]
    return pl.pallas_call(
        functools.partial(_layer_kernel, layer=layer, bm=bm, pos_base=pos_base,
                          with_pool=pool is not None, with_ffn=ffn is not None,
                          in_batch_major=in_batch_major, out_batch_major=out_batch_major,
                          state_batch_major=state_batch_major),
        grid=(nrows // t,),
        in_specs=in_specs,
        out_specs=out_specs,
        out_shape=out_shape,
        scratch_shapes=scratch,
        compiler_params=pltpu.CompilerParams(
            dimension_semantics=("arbitrary",), vmem_limit_bytes=VMEM_LIMIT_BYTES),
        name="_".join(n for n, on in (("pool", pool), ("ffn", ffn)) if on is not None) + "_layer",
    )(*args)


def _s5_prep_kernel(are_ref, aim_ref, ldt_ref, btr_ref, bti_ref, ctr_ref, cti_ref,
                    lbr_ref, lbi_ref, wb_ref, wc_ref):
    a_re, a_im = are_ref[...], aim_ref[...]
    dt = jnp.exp(ldt_ref[...])
    mag = jnp.exp(a_re * dt)
    ang = a_im * dt
    lb_re = mag * jnp.cos(ang)
    lb_im = mag * jnp.sin(ang)
    n_re = lb_re - 1.0
    n_im = lb_im
    den = a_re * a_re + a_im * a_im
    f_re = ((n_re * a_re + n_im * a_im) / den)[:, None, :]
    f_im = ((n_im * a_re - n_re * a_im) / den)[:, None, :]
    lbr_ref[...] = lb_re
    lbi_ref[...] = lb_im
    b_re, b_im = btr_ref[...], bti_ref[...]
    bb_re = (f_re * b_re - f_im * b_im).astype(wb_ref.dtype)
    bb_im = (f_re * b_im + f_im * b_re).astype(wb_ref.dtype)
    c_re = ctr_ref[...].astype(wc_ref.dtype)
    nc_im = (-cti_ref[...]).astype(wc_ref.dtype)
    wb_ref[...] = jnp.zeros(wb_ref.shape, wb_ref.dtype)
    wc_ref[...] = jnp.zeros(wc_ref.shape, wc_ref.dtype)
    g, gc, p = b_re.shape
    gpt = wb_ref.shape[1] // gc
    half = gpt * p
    for gi in range(g):
        j, gl = divmod(gi, gpt)
        rows, cols = slice(gl * gc, (gl + 1) * gc), slice(gl * p, (gl + 1) * p)
        icols = slice(half + gl * p, half + (gl + 1) * p)
        wb_ref[j, rows, cols] = bb_re[gi]
        wb_ref[j, rows, icols] = bb_im[gi]
        wc_ref[j, cols, rows] = c_re[gi]
        wc_ref[j, icols, rows] = nc_im[gi]


def _s5_prep(a_re, a_im, log_dt, b_re, b_im, c_re, c_im):
    g, p = a_re.shape
    gc = b_re.shape[-1]
    gpt = V7X_MXU_DIM // gc
    tr = lambda a: jnp.swapaxes(a, 1, 2)
    return pl.pallas_call(
        _s5_prep_kernel,
        out_shape=[
            jax.ShapeDtypeStruct((g, p), _F32),
            jax.ShapeDtypeStruct((g, p), _F32),
            jax.ShapeDtypeStruct((g // gpt, gpt * gc, 2 * gpt * p), _BF16),
            jax.ShapeDtypeStruct((g // gpt, 2 * gpt * p, gpt * gc), _BF16),
        ],
        compiler_params=pltpu.CompilerParams(vmem_limit_bytes=VMEM_LIMIT_BYTES),
        name="s5_prep",
    )(a_re, a_im, log_dt.reshape(g, 1), tr(b_re), tr(b_im), tr(c_re), tr(c_im))


def _s5_kernel(x_ref, mod_ref, xr0_ref, xi0_ref, gpre_ref, gpost_ref, wb_ref, wc_ref,
               lbr_ref, lbi_ref, dsk_ref, ga_ref, gb_ref,
               y_ref, xr_ref, xi_ref, *, layer, bm):
    i = pl.program_id(0)
    t, d = x_ref.shape
    tt = t // bm
    ntile, kin, ncol2 = wb_ref.shape
    ncol = ncol2 // 2
    sub = V7X_SUBLANES
    lrow = slice(layer, layer + 1)

    @pl.when(i == 0)
    def _():
        xr_ref[...] = xr0_ref[...]
        xi_ref[...] = xi0_ref[...]

    x = x_ref[...]
    shift, scale, gate = _mod3(mod_ref, 0, d)
    h = _pre_mod(x, gpre_ref[lrow, :], scale, shift, bm)
    u = h.astype(_BF16)

    ys = []
    for j in range(ntile):
        scol = slice(j * ncol, (j + 1) * ncol)
        bu = _dot(u[:, j * kin:(j + 1) * kin], wb_ref[j])
        lr = jnp.broadcast_to(lbr_ref[:, scol], (sub, ncol))
        li = jnp.broadcast_to(lbi_ref[:, scol], (sub, ncol))
        blocks = [None] * (t // sub)
        for rb in range(bm // sub):
            rows = slice(rb * sub, (rb + 1) * sub)
            pr, pi = xr_ref[rows, scol], xi_ref[rows, scol]
            for ts in range(tt):
                r0 = ts * bm + rb * sub
                nr = lr * pr - li * pi + bu[r0:r0 + sub, :ncol]
                ni = lr * pi + li * pr + bu[r0:r0 + sub, ncol:]
                blocks[r0 // sub] = jnp.concatenate([nr, ni], axis=1)
                pr, pi = nr, ni
            xr_ref[rows, scol] = pr
            xi_ref[rows, scol] = pi
        ys.append(_dot(jnp.concatenate(blocks, axis=0).astype(_BF16), wc_ref[j]))
    y = jnp.concatenate(ys, axis=1) + dsk_ref[...] * h
    g = _gelu(y).astype(_BF16)
    out = _dot(g, ga_ref[...]) * jax.nn.sigmoid(_dot(g, gb_ref[...]))
    y_ref[...] = _gated_residual(x, out, gpost_ref[lrow, :], gate, bm)


def _s5_layer(x, mod, xr0, xi0, g_pre, g_post, wb, wc, lbr, lbi, dskip, glu_a, glu_b, *, layer, bm, tt):
    r, d = x.shape
    t = tt * bm
    nst = xr0.shape[1]
    return pl.pallas_call(
        functools.partial(_s5_kernel, layer=layer, bm=bm),
        grid=(r // t,),
        in_specs=[
            pl.BlockSpec((t, d), lambda i: (i, 0)),
            _layer_slab(mod.shape, layer),
            _whole((bm, nst)),
            _whole((bm, nst)),
            _whole(g_pre.shape),
            _whole(g_post.shape),
            _whole(wb.shape),
            _whole(wc.shape),
            _whole((1, nst)),
            _whole((1, nst)),
            _whole((1, d)),
            _whole(glu_a.shape),
            _whole(glu_b.shape),
        ],
        out_specs=[
            pl.BlockSpec((t, d), lambda i: (i, 0)),
            _whole((bm, nst)),
            _whole((bm, nst)),
        ],
        out_shape=[
            jax.ShapeDtypeStruct((r, d), _F32),
            jax.ShapeDtypeStruct((bm, nst), _F32),
            jax.ShapeDtypeStruct((bm, nst), _F32),
        ],
        compiler_params=pltpu.CompilerParams(
            dimension_semantics=("arbitrary",), vmem_limit_bytes=VMEM_LIMIT_BYTES),
        name="s5_layer",
    )(x, mod, xr0, xi0, g_pre, g_post, wb, wc, lbr, lbi, dskip, glu_a, glu_b)


def _time_major(a):
    b, l, c = a.shape
    return jnp.swapaxes(a, 0, 1).reshape(l * b, c)


def _batch_major(a, b):
    lb, c = a.shape
    return jnp.swapaxes(a.reshape(lb // b, b, c), 0, 1)


def kernel(x_prompt, x_sample, c_prompt, c_sample, state_pool, state_ssm_re, state_ssm_im, state_ffn_conv, ada_w, ada_b, mix_pre_g, mix_post_g, ffn_pre_g, ffn_post_g, pool_w, pool_scale, ssm_A_re, ssm_A_im, ssm_log_dt, ssm_B_re, ssm_B_im, ssm_C_re, ssm_C_im, ssm_D, ssm_glu_a, ssm_glu_b, ffn_w_up, ffn_conv_w, ffn_conv_b, ffn_w_down):
    depth = ada_w.shape[0]
    bp, lp, d = x_prompt.shape
    bs, ls, _ = x_sample.shape
    nup = ffn_w_up.shape[-1]
    groups = ssm_A_re.shape[1]
    nst = groups * SSM_P

    mod_p, mod_s = _ada_mod(c_prompt, c_sample, ada_w, ada_b)
    gains = (mix_pre_g, mix_post_g, ffn_pre_g, ffn_post_g)
    w_up = ffn_w_up.astype(_BF16)
    w_dn = ffn_w_down.astype(_BF16)

    geo = {
        "p": dict(bm=bp, ffn_tt=128, s5_tt=64, pos_base=0, fuse_pool=True),
        "s": dict(bm=bs, ffn_tt=ls // 2, s5_tt=ls // 4, pos_base=PAST_LEN, fuse_pool=False, pool_tt=ls),
    }
    ys = {"p": x_prompt, "s": x_sample}
    mods = {"p": mod_p, "s": mod_s}
    pool_out = {"p": [], "s": []}
    ssm_out = {"p": [], "s": []}
    conv_out = {"p": [], "s": []}

    for l in range(depth):
        j = l // 2
        pool = {"p": None, "s": None}
        if l % 2 == 0:
            pw = pool_w[j].astype(_BF16)
            ps = pool_scale[j].reshape(1, d)
            pool["p"] = (jnp.zeros((POOL_STATE * bp, d), _F32), pw, ps)
            pool["s"] = (state_pool[j], pw, ps)
        else:
            lb_re, lb_im, wb, wc = _s5_prep(ssm_A_re[j], ssm_A_im[j], ssm_log_dt[j], ssm_B_re[j], ssm_B_im[j],
                                            ssm_C_re[j], ssm_C_im[j])
            lbr = lb_re.reshape(1, nst)
            lbi = lb_im.reshape(1, nst)
            ga = ssm_glu_a[j].astype(_BF16)
            gb = ssm_glu_b[j].astype(_BF16)
            for k in ("p", "s"):
                gk = geo[k]
                if k == "p":
                    xr0 = jnp.zeros((bp, nst), _F32)
                    xi0 = xr0
                else:
                    xr0 = state_ssm_re[j].reshape(bs, nst)
                    xi0 = state_ssm_im[j].reshape(bs, nst)
                ys[k], xr, xi = _s5_layer(ys[k], mods[k], xr0, xi0, mix_pre_g, mix_post_g,
                                          wb, wc, lbr, lbi, ssm_D[j].reshape(1, d), ga, gb,
                                          layer=l, bm=gk["bm"], tt=gk["s5_tt"])
                ssm_out[k].append((xr.reshape(gk["bm"], groups, SSM_P), xi.reshape(gk["bm"], groups, SSM_P)))

        for k in ("p", "s"):
            gk = geo[k]
            if k == "p":
                cv0 = jnp.zeros(((CONV_W - 1) * bp, nup), _F32)
            else:
                cv0 = _time_major(state_ffn_conv[l])
            ffn = (cv0, w_up, ffn_conv_w, ffn_conv_b, w_dn)
            common = dict(layer=l, bm=gk["bm"], pos_base=gk["pos_base"])
            if pool[k] is not None and not gk["fuse_pool"]:
                ys[k], st = _token_layer(ys[k], mods[k], gains, tt=gk["pool_tt"], pool=pool[k],
                                         in_batch_major=True, state_batch_major=True, **common)
                pool_out[k].append(st)
                pool[k] = None
            outs = _token_layer(ys[k], mods[k], gains, tt=gk["ffn_tt"], pool=pool[k], ffn=ffn,
                                in_batch_major=(l == 0 and pool[k] is not None),
                                out_batch_major=(k == "p" and l == depth - 1), **common)
            ys[k], cv = outs[0], outs[-1]
            if pool[k] is not None:
                pool_out[k].append(_batch_major(outs[1], gk["bm"]))
            conv_out[k].append(_batch_major(cv, gk["bm"]))

    y_prompt = ys["p"]
    y_sample = _batch_major(ys["s"], bs)
    return (y_prompt, y_sample,
            jnp.stack(pool_out["p"]), jnp.stack(pool_out["s"]),
            jnp.stack([a for a, _ in ssm_out["p"]]), jnp.stack([b for _, b in ssm_out["p"]]),
            jnp.stack([a for a, _ in ssm_out["s"]]), jnp.stack([b for _, b in ssm_out["s"]]),
            jnp.stack(conv_out["p"]), jnp.stack(conv_out["s"]))
```

```python
import functools
import math

import jax
import jax.numpy as jnp
from jax import lax
from jax.experimental import pallas as pl
from jax.experimental.pallas import tpu as pltpu

POOL_WINDOWS = (2, 4, 8, 16)
POOL_STATE = max(POOL_WINDOWS) - 1
SSM_GC = 16
SSM_P = 64
CONV_W = 3
EPS = 1e-6
PAST_LEN = 16384

V7X_SUBLANES = 8
V7X_LANES = 128
V7X_MXU_DIM = 256
VMEM_LIMIT_BYTES = 56 * 1024 * 1024

_F32 = jnp.float32
_BF16 = jnp.bfloat16


def _resident(block_shape, index_map):
    return pl.BlockSpec(block_shape, index_map, pipeline_mode=pl.Buffered(1))


def _whole(shape):
    nd = len(shape)
    return _resident(shape, lambda i: (0,) * nd)


def _layer_slab(shape, l):
    nd = len(shape)
    return _resident((None,) + tuple(shape[1:]), lambda i: (l,) + (0,) * (nd - 1))


def _unit_rms(x):
    return x * lax.rsqrt(jnp.mean(x * x, axis=-1, keepdims=True) + EPS)


def _pre_mod(x, g, scale, shift, bm):
    t, d = x.shape
    r = _unit_rms(x).reshape(t // bm, bm, d)
    return (r * (g * (1.0 + scale))[None] + shift[None]).reshape(t, d)


def _gated_residual(x, m, g, gate, bm):
    t, d = x.shape
    r = _unit_rms(m).reshape(t // bm, bm, d)
    return x + (r * (gate * g)[None]).reshape(t, d)


def _gelu(x):
    return 0.5 * x * (1.0 + lax.erf(x * math.sqrt(0.5)))


def _dot(a, b):
    return jnp.dot(a, b, preferred_element_type=_F32)


def _load_rows(x_ref, batch_major):
    if not batch_major:
        return x_ref[...]
    b, tt, d = x_ref.shape
    return jnp.swapaxes(x_ref[...], 0, 1).reshape(tt * b, d)


def _store_rows(y_ref, y, batch_major):
    if not batch_major:
        y_ref[...] = y
    else:
        b, tt, d = y_ref.shape
        y_ref[...] = jnp.swapaxes(y.reshape(tt, b, d), 0, 1)


def _mod3(mod_ref, k, d):
    return tuple(mod_ref[:, (3 * k + n) * d:(3 * k + n + 1) * d] for n in range(3))


def _ada_kernel(cp_ref, cs_ref, w_ref, b_ref, mp_ref, ms_ref):
    bp = cp_ref.shape[0]
    c = jnp.concatenate([cp_ref[...], cs_ref[...]], axis=0)
    s = (c * jax.nn.sigmoid(c)).astype(_BF16)
    o = _dot(s, w_ref[...].astype(_BF16)) + b_ref[...]
    mp_ref[...] = o[:bp]
    ms_ref[...] = o[bp:]


def _ada_mod(c_prompt, c_sample, ada_w, ada_b):
    depth, d, n = ada_w.shape
    tn = 1536
    bp, bs = c_prompt.shape[0], c_sample.shape[0]
    return pl.pallas_call(
        _ada_kernel,
        grid=(depth, n // tn),
        in_specs=[
            pl.BlockSpec((bp, d), lambda l, j: (0, 0)),
            pl.BlockSpec((bs, d), lambda l, j: (0, 0)),
            pl.BlockSpec((None, d, tn), lambda l, j: (l, 0, j)),
            pl.BlockSpec((None, 1, tn), lambda l, j: (l, 0, j)),
        ],
        out_specs=[
            pl.BlockSpec((None, bp, tn), lambda l, j: (l, 0, j)),
            pl.BlockSpec((None, bs, tn), lambda l, j: (l, 0, j)),
        ],
        out_shape=[
            jax.ShapeDtypeStruct((depth, bp, n), _F32),
            jax.ShapeDtypeStruct((depth, bs, n), _F32),
        ],
        compiler_params=pltpu.CompilerParams(
            dimension_semantics=("arbitrary", "arbitrary"),
            vmem_limit_bytes=VMEM_LIMIT_BYTES),
        name="ada_mod",
    )(c_prompt, c_sample, ada_w, ada_b.reshape(depth, 1, n))


def _pool_mix(h, st, pw_ref, ps_ref, i, *, bm, pos_base):
    t, d = h.shape
    tt = t // bm
    gc = d // len(POOL_WINDOWS)
    ext = jnp.concatenate([st, h], axis=0)
    pos = None
    if pos_base + 1 < max(POOL_WINDOWS):
        row = lax.broadcasted_iota(jnp.int32, (t, V7X_LANES), 0)
        pos = pos_base + i * tt + lax.shift_right_logical(row, bm.bit_length() - 1)
    parts = []
    for gi, w in enumerate(POOL_WINDOWS):
        cols = slice(gi * gc, (gi + 1) * gc)
        s = ext[:, cols]
        span = 1
        while span < w:
            n = s.shape[0]
            s = s[span * bm:] + s[:n - span * bm]
            span *= 2
        k0 = (POOL_STATE - (w - 1)) * bm
        wsum = s[k0:k0 + t]
        if pos is None:
            inv = 1.0 / w
        else:
            inv = 1.0 / jnp.minimum(pos + 1, w).astype(_F32)
            inv = jnp.concatenate([inv] * (gc // V7X_LANES), axis=1)
        pooled = wsum * inv - h[:, cols]
        parts.append(_dot(pooled.astype(_BF16), pw_ref[gi]))
    return jnp.concatenate(parts, axis=1) * ps_ref[...], ext[t:, :]


def _conv_ffn(f, cv_ref, wup_ref, cw_ref, cb_ref, wdn_ref, h_ref, *, bm):
    t = f.shape[0]
    hid = wdn_ref.shape[0]
    hc = V7X_MXU_DIM

    def conv_cols(c0):
        cols = slice(c0, c0 + hc)
        up = _dot(f, wup_ref[:, cols])
        ext = jnp.concatenate([cv_ref[:, cols], up], axis=0)
        cv_ref[:, cols] = ext[t:, :]
        conv = cb_ref[:, cols]
        for k in range(CONV_W):
            conv = conv + ext[k * bm:k * bm + t, :] * cw_ref[k:k + 1, cols]
        return conv

    for c in range(hid // hc):
        gate_c = conv_cols(c * hc)
        val_c = conv_cols(hid + c * hc)
        h_ref[:, c * hc:(c + 1) * hc] = (_gelu(gate_c) * val_c).astype(_BF16)
    return _dot(h_ref[...], wdn_ref[...])


def _layer_kernel(*refs, layer, bm, pos_base, with_pool, with_ffn, in_batch_major, out_batch_major,
                  state_batch_major):
    refs = list(refs)
    x_ref, mod_ref = refs[:2]
    del refs[:2]
    if with_pool:
        st0_ref, mpre_ref, mpost_ref, pw_ref, ps_ref = refs[:5]
        del refs[:5]
    if with_ffn:
        cv0_ref, fpre_ref, fpost_ref, wup_ref, cw_ref, cb_ref, wdn_ref = refs[:7]
        del refs[:7]
    y_ref = refs.pop(0)
    if with_pool:
        st_ref = refs.pop(0)
    if with_ffn:
        cv_ref, h_ref = refs
    i = pl.program_id(0)
    d = mod_ref.shape[1] // 6
    lrow = slice(layer, layer + 1)

    @pl.when(i == 0)
    def _():
        if with_pool and not state_batch_major:
            st_ref[...] = st0_ref[...]
        if with_ffn:
            cv_ref[...] = cv0_ref[...]

    x = _load_rows(x_ref, in_batch_major)
    if with_pool:
        shift, scale, gate = _mod3(mod_ref, 0, d)
        h = _pre_mod(x, mpre_ref[lrow, :], scale, shift, bm)
        st = _load_rows(st0_ref, True) if state_batch_major else st_ref[...]
        m, st = _pool_mix(h, st, pw_ref, ps_ref, i, bm=bm, pos_base=pos_base)
        _store_rows(st_ref, st, state_batch_major)
        x = _gated_residual(x, m, mpost_ref[lrow, :], gate, bm)
    if with_ffn:
        shift, scale, gate = _mod3(mod_ref, 1, d)
        f = _pre_mod(x, fpre_ref[lrow, :], scale, shift, bm).astype(_BF16)
        o = _conv_ffn(f, cv_ref, wup_ref, cw_ref, cb_ref.at[lrow, :], wdn_ref, h_ref, bm=bm)
        x = _gated_residual(x, o, fpost_ref[lrow, :], gate, bm)
    _store_rows(y_ref, x, out_batch_major)


def _token_layer(x, mod, gains, *, layer, bm, tt, pos_base=0, pool=None, ffn=None,
                 in_batch_major=False, out_batch_major=False, state_batch_major=False):
    mix_pre_g, mix_post_g, ffn_pre_g, ffn_post_g = gains
    d = x.shape[-1]
    nrows = x.shape[0] * x.shape[1] if in_batch_major else x.shape[0]
    t = tt * bm
    assert not state_batch_major or nrows == t
    tm_spec = pl.BlockSpec((t, d), lambda i: (i, 0))
    bm_spec = pl.BlockSpec((bm, tt, d), lambda i: (0, i, 0))

    args = [x, mod]
    in_specs = [bm_spec if in_batch_major else tm_spec, _layer_slab(mod.shape, layer)]
    out_specs = [bm_spec if out_batch_major else tm_spec]
    out_shape = [jax.ShapeDtypeStruct((bm, nrows // bm, d) if out_batch_major else (nrows, d), _F32)]
    scratch = []
    if pool is not None:
        st0, pool_w, pool_scale = pool
        args += [st0, mix_pre_g, mix_post_g, pool_w, pool_scale]
        in_specs += [_whole(a.shape) for a in args[-5:]]
        out_specs += [_whole(st0.shape)]
        out_shape += [jax.ShapeDtypeStruct(st0.shape, _F32)]
    if ffn is not None:
        cv0, w_up, conv_w, conv_b, w_down = ffn
        hid = w_down.shape[1]
        assert hid % V7X_MXU_DIM == 0 and cv0.shape[1] == 2 * hid
        args += [cv0, ffn_pre_g, ffn_post_g, w_up, conv_w, conv_b, w_down]
        in_specs += [_whole(cv0.shape), _whole(ffn_pre_g.shape), _whole(ffn_post_g.shape),
                     _layer_slab(w_up.shape, layer), _layer_slab(conv_w.shape, layer),
                     _whole(conv_b.shape), _layer_slab(w_down.shape, layer)]
        out_specs += [_whole(cv0.shape)]
        out_shape += [jax.ShapeDtypeStruct(cv0.shape, _F32)]
        scratch += [pltpu.VMEM((t, hid), _BF16)]
    return pl.pallas_call(
        functools.partial(_layer_kernel, layer=layer, bm=bm, pos_base=pos_base,
                          with_pool=pool is not None, with_ffn=ffn is not None,
                          in_batch_major=in_batch_major, out_batch_major=out_batch_major,
                          state_batch_major=state_batch_major),
        grid=(nrows // t,),
        in_specs=in_specs,
        out_specs=out_specs,
        out_shape=out_shape,
        scratch_shapes=scratch,
        compiler_params=pltpu.CompilerParams(
            dimension_semantics=("arbitrary",), vmem_limit_bytes=VMEM_LIMIT_BYTES),
        name="_".join(n for n, on in (("pool", pool), ("ffn", ffn)) if on is not None) + "_layer",
    )(*args)


def _s5_prep_kernel(are_ref, aim_ref, ldt_ref, btr_ref, bti_ref, ctr_ref, cti_ref,
                    lbr_ref, lbi_ref, wb_ref, wc_ref):
    a_re, a_im = are_ref[...], aim_ref[...]
    dt = jnp.exp(ldt_ref[...])
    mag = jnp.exp(a_re * dt)
    ang = a_im * dt
    lb_re = mag * jnp.cos(ang)
    lb_im = mag * jnp.sin(ang)
    n_re = lb_re - 1.0
    n_im = lb_im
    den = a_re * a_re + a_im * a_im
    f_re = ((n_re * a_re + n_im * a_im) / den)[:, None, :]
    f_im = ((n_im * a_re - n_re * a_im) / den)[:, None, :]
    lbr_ref[...] = lb_re
    lbi_ref[...] = lb_im
    b_re, b_im = btr_ref[...], bti_ref[...]
    bb_re = (f_re * b_re - f_im * b_im).astype(wb_ref.dtype)
    bb_im = (f_re * b_im + f_im * b_re).astype(wb_ref.dtype)
    c_re = ctr_ref[...].astype(wc_ref.dtype)
    nc_im = (-cti_ref[...]).astype(wc_ref.dtype)
    wb_ref[...] = jnp.zeros(wb_ref.shape, wb_ref.dtype)
    wc_ref[...] = jnp.zeros(wc_ref.shape, wc_ref.dtype)
    g, gc, p = b_re.shape
    gpt = wb_ref.shape[1] // gc
    half = gpt * p
    for gi in range(g):
        j, gl = divmod(gi, gpt)
        rows, cols = slice(gl * gc, (gl + 1) * gc), slice(gl * p, (gl + 1) * p)
        icols = slice(half + gl * p, half + (gl + 1) * p)
        wb_ref[j, rows, cols] = bb_re[gi]
        wb_ref[j, rows, icols] = bb_im[gi]
        wc_ref[j, cols, rows] = c_re[gi]
        wc_ref[j, icols, rows] = nc_im[gi]


def _s5_prep(a_re, a_im, log_dt, b_re, b_im, c_re, c_im):
    g, p = a_re.shape
    gc = b_re.shape[-1]
    gpt = V7X_MXU_DIM // gc
    tr = lambda a: jnp.swapaxes(a, 1, 2)
    return pl.pallas_call(
        _s5_prep_kernel,
        out_shape=[
            jax.ShapeDtypeStruct((g, p), _F32),
            jax.ShapeDtypeStruct((g, p), _F32),
            jax.ShapeDtypeStruct((g // gpt, gpt * gc, 2 * gpt * p), _BF16),
            jax.ShapeDtypeStruct((g // gpt, 2 * gpt * p, gpt * gc), _BF16),
        ],
        compiler_params=pltpu.CompilerParams(vmem_limit_bytes=VMEM_LIMIT_BYTES),
        name="s5_prep",
    )(a_re, a_im, log_dt.reshape(g, 1), tr(b_re), tr(b_im), tr(c_re), tr(c_im))


def _s5_kernel(x_ref, mod_ref, xr0_ref, xi0_ref, gpre_ref, gpost_ref, wb_ref, wc_ref,
               lbr_ref, lbi_ref, dsk_ref, ga_ref, gb_ref,
               y_ref, xr_ref, xi_ref, *, layer, bm):
    i = pl.program_id(0)
    t, d = x_ref.shape
    tt = t // bm
    ntile, kin, ncol2 = wb_ref.shape
    ncol = ncol2 // 2
    sub = V7X_SUBLANES
    lrow = slice(layer, layer + 1)

    @pl.when(i == 0)
    def _():
        xr_ref[...] = xr0_ref[...]
        xi_ref[...] = xi0_ref[...]

    x = x_ref[...]
    shift, scale, gate = _mod3(mod_ref, 0, d)
    h = _pre_mod(x, gpre_ref[lrow, :], scale, shift, bm)
    u = h.astype(_BF16)

    ys = []
    for j in range(ntile):
        scol = slice(j * ncol, (j + 1) * ncol)
        bu = _dot(u[:, j * kin:(j + 1) * kin], wb_ref[j])
        lr = jnp.broadcast_to(lbr_ref[:, scol], (sub, ncol))
        li = jnp.broadcast_to(lbi_ref[:, scol], (sub, ncol))
        blocks = [None] * (t // sub)
        for rb in range(bm // sub):
            rows = slice(rb * sub, (rb + 1) * sub)
            pr, pi = xr_ref[rows, scol], xi_ref[rows, scol]
            for ts in range(tt):
                r0 = ts * bm + rb * sub
                nr = lr * pr - li * pi + bu[r0:r0 + sub, :ncol]
                ni = lr * pi + li * pr + bu[r0:r0 + sub, ncol:]
                blocks[r0 // sub] = jnp.concatenate([nr, ni], axis=1)
                pr, pi = nr, ni
            xr_ref[rows, scol] = pr
            xi_ref[rows, scol] = pi
        ys.append(_dot(jnp.concatenate(blocks, axis=0).astype(_BF16), wc_ref[j]))
    y = jnp.concatenate(ys, axis=1) + dsk_ref[...] * h
    g = _gelu(y).astype(_BF16)
    out = _dot(g, ga_ref[...]) * jax.nn.sigmoid(_dot(g, gb_ref[...]))
    y_ref[...] = _gated_residual(x, out, gpost_ref[lrow, :], gate, bm)


def _s5_layer(x, mod, xr0, xi0, g_pre, g_post, wb, wc, lbr, lbi, dskip, glu_a, glu_b, *, layer, bm, tt):
    r, d = x.shape
    t = tt * bm
    nst = xr0.shape[1]
    return pl.pallas_call(
        functools.partial(_s5_kernel, layer=layer, bm=bm),
        grid=(r // t,),
        in_specs=[
            pl.BlockSpec((t, d), lambda i: (i, 0)),
            _layer_slab(mod.shape, layer),
            _whole((bm, nst)),
            _whole((bm, nst)),
            _whole(g_pre.shape),
            _whole(g_post.shape),
            _whole(wb.shape),
            _whole(wc.shape),
            _whole((1, nst)),
            _whole((1, nst)),
            _whole((1, d)),
            _whole(glu_a.shape),
            _whole(glu_b.shape),
        ],
        out_specs=[
            pl.BlockSpec((t, d), lambda i: (i, 0)),
            _whole((bm, nst)),
            _whole((bm, nst)),
        ],
        out_shape=[
            jax.ShapeDtypeStruct((r, d), _F32),
            jax.ShapeDtypeStruct((bm, nst), _F32),
            jax.ShapeDtypeStruct((bm, nst), _F32),
        ],
        compiler_params=pltpu.CompilerParams(
            dimension_semantics=("arbitrary",), vmem_limit_bytes=VMEM_LIMIT_BYTES),
        name="s5_layer",
    )(x, mod, xr0, xi0, g_pre, g_post, wb, wc, lbr, lbi, dskip, glu_a, glu_b)


def _time_major(a):
    b, l, c = a.shape
    return jnp.swapaxes(a, 0, 1).reshape(l * b, c)


def _batch_major(a, b):
    lb, c = a.shape
    return jnp.swapaxes(a.reshape(lb // b, b, c), 0, 1)


def kernel(x_prompt, x_sample, c_prompt, c_sample, state_pool, state_ssm_re, state_ssm_im, state_ffn_conv, ada_w, ada_b, mix_pre_g, mix_post_g, ffn_pre_g, ffn_post_g, pool_w, pool_scale, ssm_A_re, ssm_A_im, ssm_log_dt, ssm_B_re, ssm_B_im, ssm_C_re, ssm_C_im, ssm_D, ssm_glu_a, ssm_glu_b, ffn_w_up, ffn_conv_w, ffn_conv_b, ffn_w_down):
    depth = ada_w.shape[0]
    bp, lp, d = x_prompt.shape
    bs, ls, _ = x_sample.shape
    nup = ffn_w_up.shape[-1]
    groups = ssm_A_re.shape[1]
    nst = groups * SSM_P

    mod_p, mod_s = _ada_mod(c_prompt, c_sample, ada_w, ada_b)
    gains = (mix_pre_g, mix_post_g, ffn_pre_g, ffn_post_g)
    w_up = ffn_w_up.astype(_BF16)
    w_dn = ffn_w_down.astype(_BF16)

    geo = {
        "p": dict(bm=bp, pool_ffn_tt=64, ffn_tt=128, s5_tt=64, pos_base=0, fuse_pool=True),
        "s": dict(bm=bs, pool_ffn_tt=ls // 2, ffn_tt=ls // 2, s5_tt=ls // 4, pos_base=PAST_LEN, fuse_pool=False,
                  pool_tt=ls),
    }
    ys = {"p": x_prompt, "s": x_sample}
    mods = {"p": mod_p, "s": mod_s}
    pool_out = {"p": [], "s": []}
    ssm_out = {"p": [], "s": []}
    conv_out = {"p": [], "s": []}

    for l in range(depth):
        j = l // 2
        pool = {"p": None, "s": None}
        if l % 2 == 0:
            pw = pool_w[j].astype(_BF16)
            ps = pool_scale[j].reshape(1, d)
            pool["p"] = (jnp.zeros((POOL_STATE * bp, d), _F32), pw, ps)
            pool["s"] = (state_pool[j], pw, ps)
        else:
            lb_re, lb_im, wb, wc = _s5_prep(ssm_A_re[j], ssm_A_im[j], ssm_log_dt[j], ssm_B_re[j], ssm_B_im[j],
                                            ssm_C_re[j], ssm_C_im[j])
            lbr = lb_re.reshape(1, nst)
            lbi = lb_im.reshape(1, nst)
            ga = ssm_glu_a[j].astype(_BF16)
            gb = ssm_glu_b[j].astype(_BF16)
            for k in ("p", "s"):
                gk = geo[k]
                if k == "p":
                    xr0 = jnp.zeros((bp, nst), _F32)
                    xi0 = xr0
                else:
                    xr0 = state_ssm_re[j].reshape(bs, nst)
                    xi0 = state_ssm_im[j].reshape(bs, nst)
                ys[k], xr, xi = _s5_layer(ys[k], mods[k], xr0, xi0, mix_pre_g, mix_post_g,
                                          wb, wc, lbr, lbi, ssm_D[j].reshape(1, d), ga, gb,
                                          layer=l, bm=gk["bm"], tt=gk["s5_tt"])
                ssm_out[k].append((xr.reshape(gk["bm"], groups, SSM_P), xi.reshape(gk["bm"], groups, SSM_P)))

        for k in ("p", "s"):
            gk = geo[k]
            if k == "p":
                cv0 = jnp.zeros(((CONV_W - 1) * bp, nup), _F32)
            else:
                cv0 = _time_major(state_ffn_conv[l])
            ffn = (cv0, w_up, ffn_conv_w, ffn_conv_b, w_dn)
            common = dict(layer=l, bm=gk["bm"], pos_base=gk["pos_base"])
            if pool[k] is not None and not gk["fuse_pool"]:
                ys[k], st = _token_layer(ys[k], mods[k], gains, tt=gk["pool_tt"], pool=pool[k],
                                         in_batch_major=True, state_batch_major=True, **common)
                pool_out[k].append(st)
                pool[k] = None
            tt = gk["ffn_tt"] if pool[k] is None else gk["pool_ffn_tt"]
            outs = _token_layer(ys[k], mods[k], gains, tt=tt, pool=pool[k], ffn=ffn,
                                in_batch_major=(l == 0 and pool[k] is not None),
                                out_batch_major=(k == "p" and l == depth - 1), **common)
            ys[k], cv = outs[0], outs[-1]
            if pool[k] is not None:
                pool_out[k].append(_batch_major(outs[1], gk["bm"]))
            conv_out[k].append(_batch_major(cv, gk["bm"]))

    y_prompt = ys["p"]
    y_sample = _batch_major(ys["s"], bs)
    return (y_prompt, y_sample,
            jnp.stack(pool_out["p"]), jnp.stack(pool_out["s"]),
            jnp.stack([a for a, _ in ssm_out["p"]]), jnp.stack([b for _, b in ssm_out["p"]]),
            jnp.stack([a for a, _ in ssm_out["s"]]), jnp.stack([b for _, b in ssm_out["s"]]),
            jnp.stack(conv_out["p"]), jnp.stack(conv_out["s"]))
```

```python
import functools
import math

import jax
import jax.numpy as jnp
from jax import lax
from jax.experimental import pallas as pl
from jax.experimental.pallas import tpu as pltpu

POOL_WINDOWS = (2, 4, 8, 16)
POOL_STATE = max(POOL_WINDOWS) - 1
SSM_GC = 16
SSM_P = 64
CONV_W = 3
EPS = 1e-6
PAST_LEN = 16384

V7X_SUBLANES = 8
V7X_LANES = 128
V7X_MXU_DIM = 256
VMEM_LIMIT_BYTES = 56 * 1024 * 1024

_F32 = jnp.float32
_BF16 = jnp.bfloat16


def _resident(block_shape, index_map):
    return pl.BlockSpec(block_shape, index_map, pipeline_mode=pl.Buffered(1))


def _whole(shape):
    nd = len(shape)
    return _resident(shape, lambda i: (0,) * nd)


def _layer_slab(shape, l):
    nd = len(shape)
    return _resident((None,) + tuple(shape[1:]), lambda i: (l,) + (0,) * (nd - 1))


def _unit_rms(x):
    return x * lax.rsqrt(jnp.mean(x * x, axis=-1, keepdims=True) + EPS)


def _pre_mod(x, g, scale, shift, bm):
    t, d = x.shape
    r = _unit_rms(x).reshape(t // bm, bm, d)
    return (r * (g * (1.0 + scale))[None] + shift[None]).reshape(t, d)


def _gated_residual(x, m, g, gate, bm):
    t, d = x.shape
    r = _unit_rms(m).reshape(t // bm, bm, d)
    return x + (r * (gate * g)[None]).reshape(t, d)


def _gelu(x):
    return 0.5 * x * (1.0 + lax.erf(x * math.sqrt(0.5)))


def _dot(a, b):
    return jnp.dot(a, b, preferred_element_type=_F32)


def _load_rows(x_ref, batch_major):
    if not batch_major:
        return x_ref[...]
    b, tt, d = x_ref.shape
    return jnp.swapaxes(x_ref[...], 0, 1).reshape(tt * b, d)


def _store_rows(y_ref, y, batch_major):
    if not batch_major:
        y_ref[...] = y
    else:
        b, tt, d = y_ref.shape
        y_ref[...] = jnp.swapaxes(y.reshape(tt, b, d), 0, 1)


def _mod3(mod_ref, k, d):
    return tuple(mod_ref[:, (3 * k + n) * d:(3 * k + n + 1) * d] for n in range(3))


def _ada_kernel(cp_ref, cs_ref, w_ref, b_ref, mp_ref, ms_ref):
    bp = cp_ref.shape[0]
    c = jnp.concatenate([cp_ref[...], cs_ref[...]], axis=0)
    s = (c * jax.nn.sigmoid(c)).astype(_BF16)
    o = _dot(s, w_ref[...].astype(_BF16)) + b_ref[...]
    mp_ref[...] = o[:bp]
    ms_ref[...] = o[bp:]


def _ada_mod(c_prompt, c_sample, ada_w, ada_b):
    depth, d, n = ada_w.shape
    tn = 1536
    bp, bs = c_prompt.shape[0], c_sample.shape[0]
    return pl.pallas_call(
        _ada_kernel,
        grid=(depth, n // tn),
        in_specs=[
            pl.BlockSpec((bp, d), lambda l, j: (0, 0)),
            pl.BlockSpec((bs, d), lambda l, j: (0, 0)),
            pl.BlockSpec((None, d, tn), lambda l, j: (l, 0, j)),
            pl.BlockSpec((None, 1, tn), lambda l, j: (l, 0, j)),
        ],
        out_specs=[
            pl.BlockSpec((None, bp, tn), lambda l, j: (l, 0, j)),
            pl.BlockSpec((None, bs, tn), lambda l, j: (l, 0, j)),
        ],
        out_shape=[
            jax.ShapeDtypeStruct((depth, bp, n), _F32),
            jax.ShapeDtypeStruct((depth, bs, n), _F32),
        ],
        compiler_params=pltpu.CompilerParams(
            dimension_semantics=("arbitrary", "arbitrary"),
            vmem_limit_bytes=VMEM_LIMIT_BYTES),
        name="ada_mod",
    )(c_prompt, c_sample, ada_w, ada_b.reshape(depth, 1, n))


def _pool_mix(h, st, pw_ref, ps_ref, i, *, bm, pos_base):
    t, d = h.shape
    tt = t // bm
    gc = d // len(POOL_WINDOWS)
    ext = jnp.concatenate([st, h], axis=0)
    pos = None
    if pos_base + 1 < max(POOL_WINDOWS):
        row = lax.broadcasted_iota(jnp.int32, (t, V7X_LANES), 0)
        pos = pos_base + i * tt + lax.shift_right_logical(row, bm.bit_length() - 1)
    parts = []
    for gi, w in enumerate(POOL_WINDOWS):
        cols = slice(gi * gc, (gi + 1) * gc)
        s = ext[:, cols]
        span = 1
        while span < w:
            n = s.shape[0]
            s = s[span * bm:] + s[:n - span * bm]
            span *= 2
        k0 = (POOL_STATE - (w - 1)) * bm
        wsum = s[k0:k0 + t]
        if pos is None:
            inv = 1.0 / w
        else:
            inv = 1.0 / jnp.minimum(pos + 1, w).astype(_F32)
            inv = jnp.concatenate([inv] * (gc // V7X_LANES), axis=1)
        pooled = wsum * inv - h[:, cols]
        parts.append(_dot(pooled.astype(_BF16), pw_ref[gi]))
    return jnp.concatenate(parts, axis=1) * ps_ref[...], ext[t:, :]


def _conv_ffn(f, cv_ref, wup_ref, cw_ref, cb_ref, wdn_ref, h_ref, *, bm):
    t = f.shape[0]
    hid = wdn_ref.shape[0]
    hc = V7X_MXU_DIM

    def conv_cols(c0):
        cols = slice(c0, c0 + hc)
        up = _dot(f, wup_ref[:, cols])
        ext = jnp.concatenate([cv_ref[:, cols], up], axis=0)
        cv_ref[:, cols] = ext[t:, :]
        conv = cb_ref[:, cols]
        for k in range(CONV_W):
            conv = conv + ext[k * bm:k * bm + t, :] * cw_ref[k:k + 1, cols]
        return conv

    for c in range(hid // hc):
        gate_c = conv_cols(c * hc)
        val_c = conv_cols(hid + c * hc)
        h_ref[:, c * hc:(c + 1) * hc] = (_gelu(gate_c) * val_c).astype(_BF16)
    return _dot(h_ref[...], wdn_ref[...])


def _layer_kernel(*refs, layer, bm, pos_base, with_pool, with_ffn, in_batch_major, out_batch_major,
                  state_batch_major):
    refs = list(refs)
    x_ref, mod_ref = refs[:2]
    del refs[:2]
    if with_pool:
        st0_ref, mpre_ref, mpost_ref, pw_ref, ps_ref = refs[:5]
        del refs[:5]
    if with_ffn:
        cv0_ref, fpre_ref, fpost_ref, wup_ref, cw_ref, cb_ref, wdn_ref = refs[:7]
        del refs[:7]
    y_ref = refs.pop(0)
    if with_pool:
        st_ref = refs.pop(0)
    if with_ffn:
        cv_ref, h_ref = refs
    i = pl.program_id(0)
    d = mod_ref.shape[1] // 6
    lrow = slice(layer, layer + 1)

    @pl.when(i == 0)
    def _():
        if with_pool and not state_batch_major:
            st_ref[...] = st0_ref[...]
        if with_ffn:
            cv_ref[...] = cv0_ref[...]

    x = _load_rows(x_ref, in_batch_major)
    if with_pool:
        shift, scale, gate = _mod3(mod_ref, 0, d)
        h = _pre_mod(x, mpre_ref[lrow, :], scale, shift, bm)
        st = _load_rows(st0_ref, True) if state_batch_major else st_ref[...]
        m, st = _pool_mix(h, st, pw_ref, ps_ref, i, bm=bm, pos_base=pos_base)
        _store_rows(st_ref, st, state_batch_major)
        x = _gated_residual(x, m, mpost_ref[lrow, :], gate, bm)
    if with_ffn:
        shift, scale, gate = _mod3(mod_ref, 1, d)
        f = _pre_mod(x, fpre_ref[lrow, :], scale, shift, bm).astype(_BF16)
        o = _conv_ffn(f, cv_ref, wup_ref, cw_ref, cb_ref.at[lrow, :], wdn_ref, h_ref, bm=bm)
        x = _gated_residual(x, o, fpost_ref[lrow, :], gate, bm)
    _store_rows(y_ref, x, out_batch_major)


def _token_layer(x, mod, gains, *, layer, bm, tt, pos_base=0, pool=None, ffn=None,
                 in_batch_major=False, out_batch_major=False, state_batch_major=False):
    mix_pre_g, mix_post_g, ffn_pre_g, ffn_post_g = gains
    d = x.shape[-1]
    nrows = x.shape[0] * x.shape[1] if in_batch_major else x.shape[0]
    t = tt * bm
    assert not state_batch_major or nrows == t
    tm_spec = pl.BlockSpec((t, d), lambda i: (i, 0))
    bm_spec = pl.BlockSpec((bm, tt, d), lambda i: (0, i, 0))

    args = [x, mod]
    in_specs = [bm_spec if in_batch_major else tm_spec, _layer_slab(mod.shape, layer)]
    out_specs = [bm_spec if out_batch_major else tm_spec]
    out_shape = [jax.ShapeDtypeStruct((bm, nrows // bm, d) if out_batch_major else (nrows, d), _F32)]
    scratch = []
    if pool is not None:
        st0, st_slab, pool_w, pool_scale = pool
        st_shape = st0.shape if st_slab is None else st0.shape[1:]
        args += [st0, mix_pre_g, mix_post_g, pool_w, pool_scale]
        in_specs += [_whole(st0.shape) if st_slab is None else _layer_slab(st0.shape, st_slab)]
        in_specs += [_whole(a.shape) for a in args[-4:]]
        out_specs += [_whole(st_shape)]
        out_shape += [jax.ShapeDtypeStruct(st_shape, _F32)]
    if ffn is not None:
        cv0, w_up, conv_w, conv_b, w_down = ffn
        hid = w_down.shape[1]
        assert hid % V7X_MXU_DIM == 0 and cv0.shape[1] == 2 * hid
        args += [cv0, ffn_pre_g, ffn_post_g, w_up, conv_w, conv_b, w_down]
        in_specs += [_whole(cv0.shape), _whole(ffn_pre_g.shape), _whole(ffn_post_g.shape),
                     _layer_slab(w_up.shape, layer), _layer_slab(conv_w.shape, layer),
                     _whole(conv_b.shape), _layer_slab(w_down.shape, layer)]
        out_specs += [_whole(cv0.shape)]
        out_shape += [jax.ShapeDtypeStruct(cv0.shape, _F32)]
        scratch += [pltpu.VMEM((t, hid), _BF16)]
    return pl.pallas_call(
        functools.partial(_layer_kernel, layer=layer, bm=bm, pos_base=pos_base,
                          with_pool=pool is not None, with_ffn=ffn is not None,
                          in_batch_major=in_batch_major, out_batch_major=out_batch_major,
                          state_batch_major=state_batch_major),
        grid=(nrows // t,),
        in_specs=in_specs,
        out_specs=out_specs,
        out_shape=out_shape,
        scratch_shapes=scratch,
        compiler_params=pltpu.CompilerParams(
            dimension_semantics=("arbitrary",), vmem_limit_bytes=VMEM_LIMIT_BYTES),
        name="_".join(n for n, on in (("pool", pool), ("ffn", ffn)) if on is not None) + "_layer",
    )(*args)


def _s5_prep_kernel(are_ref, aim_ref, ldt_ref, btr_ref, bti_ref, ctr_ref, cti_ref,
                    lbr_ref, lbi_ref, wb_ref, wc_ref):
    a_re, a_im = are_ref[...], aim_ref[...]
    dt = jnp.exp(ldt_ref[...])
    mag = jnp.exp(a_re * dt)
    ang = a_im * dt
    lb_re = mag * jnp.cos(ang)
    lb_im = mag * jnp.sin(ang)
    n_re = lb_re - 1.0
    n_im = lb_im
    den = a_re * a_re + a_im * a_im
    f_re = ((n_re * a_re + n_im * a_im) / den)[:, None, :]
    f_im = ((n_im * a_re - n_re * a_im) / den)[:, None, :]
    lbr_ref[...] = lb_re
    lbi_ref[...] = lb_im
    b_re, b_im = btr_ref[...], bti_ref[...]
    bb_re = (f_re * b_re - f_im * b_im).astype(wb_ref.dtype)
    bb_im = (f_re * b_im + f_im * b_re).astype(wb_ref.dtype)
    c_re = ctr_ref[...].astype(wc_ref.dtype)
    nc_im = (-cti_ref[...]).astype(wc_ref.dtype)
    wb_ref[...] = jnp.zeros(wb_ref.shape, wb_ref.dtype)
    wc_ref[...] = jnp.zeros(wc_ref.shape, wc_ref.dtype)
    g, gc, p = b_re.shape
    gpt = wb_ref.shape[1] // gc
    half = gpt * p
    for gi in range(g):
        j, gl = divmod(gi, gpt)
        rows, cols = slice(gl * gc, (gl + 1) * gc), slice(gl * p, (gl + 1) * p)
        icols = slice(half + gl * p, half + (gl + 1) * p)
        wb_ref[j, rows, cols] = bb_re[gi]
        wb_ref[j, rows, icols] = bb_im[gi]
        wc_ref[j, cols, rows] = c_re[gi]
        wc_ref[j, icols, rows] = nc_im[gi]


def _s5_prep(a_re, a_im, log_dt, b_re, b_im, c_re, c_im):
    g, p = a_re.shape
    gc = b_re.shape[-1]
    gpt = V7X_MXU_DIM // gc
    tr = lambda a: jnp.swapaxes(a, 1, 2)
    return pl.pallas_call(
        _s5_prep_kernel,
        out_shape=[
            jax.ShapeDtypeStruct((g, p), _F32),
            jax.ShapeDtypeStruct((g, p), _F32),
            jax.ShapeDtypeStruct((g // gpt, gpt * gc, 2 * gpt * p), _BF16),
            jax.ShapeDtypeStruct((g // gpt, 2 * gpt * p, gpt * gc), _BF16),
        ],
        compiler_params=pltpu.CompilerParams(vmem_limit_bytes=VMEM_LIMIT_BYTES),
        name="s5_prep",
    )(a_re, a_im, log_dt.reshape(g, 1), tr(b_re), tr(b_im), tr(c_re), tr(c_im))


def _s5_kernel(x_ref, mod_ref, xr0_ref, xi0_ref, gpre_ref, gpost_ref, wb_ref, wc_ref,
               lbr_ref, lbi_ref, dsk_ref, ga_ref, gb_ref,
               y_ref, xr_ref, xi_ref, *, layer, bm):
    i = pl.program_id(0)
    t, d = x_ref.shape
    tt = t // bm
    ntile, kin, ncol2 = wb_ref.shape
    ncol = ncol2 // 2
    sub = V7X_SUBLANES
    lrow = slice(layer, layer + 1)

    @pl.when(i == 0)
    def _():
        xr_ref[...] = xr0_ref[...]
        xi_ref[...] = xi0_ref[...]

    x = x_ref[...]
    shift, scale, gate = _mod3(mod_ref, 0, d)
    h = _pre_mod(x, gpre_ref[lrow, :], scale, shift, bm)
    u = h.astype(_BF16)

    ys = []
    for j in range(ntile):
        scol = slice(j * ncol, (j + 1) * ncol)
        bu = _dot(u[:, j * kin:(j + 1) * kin], wb_ref[j])
        lr = jnp.broadcast_to(lbr_ref[:, scol], (sub, ncol))
        li = jnp.broadcast_to(lbi_ref[:, scol], (sub, ncol))
        blocks = [None] * (t // sub)
        for rb in range(bm // sub):
            rows = slice(rb * sub, (rb + 1) * sub)
            pr, pi = xr_ref[rows, scol], xi_ref[rows, scol]
            for ts in range(tt):
                r0 = ts * bm + rb * sub
                nr = lr * pr - li * pi + bu[r0:r0 + sub, :ncol]
                ni = lr * pi + li * pr + bu[r0:r0 + sub, ncol:]
                blocks[r0 // sub] = jnp.concatenate([nr, ni], axis=1)
                pr, pi = nr, ni
            xr_ref[rows, scol] = pr
            xi_ref[rows, scol] = pi
        ys.append(_dot(jnp.concatenate(blocks, axis=0).astype(_BF16), wc_ref[j]))
    y = jnp.concatenate(ys, axis=1) + dsk_ref[...] * h
    g = _gelu(y).astype(_BF16)
    out = _dot(g, ga_ref[...]) * jax.nn.sigmoid(_dot(g, gb_ref[...]))
    y_ref[...] = _gated_residual(x, out, gpost_ref[lrow, :], gate, bm)


def _s5_layer(x, mod, xr0, xi0, g_pre, g_post, wb, wc, lbr, lbi, dskip, glu_a, glu_b, *, layer, bm, tt):
    r, d = x.shape
    t = tt * bm
    nst = xr0.shape[1]
    return pl.pallas_call(
        functools.partial(_s5_kernel, layer=layer, bm=bm),
        grid=(r // t,),
        in_specs=[
            pl.BlockSpec((t, d), lambda i: (i, 0)),
            _layer_slab(mod.shape, layer),
            _whole((bm, nst)),
            _whole((bm, nst)),
            _whole(g_pre.shape),
            _whole(g_post.shape),
            _whole(wb.shape),
            _whole(wc.shape),
            _whole((1, nst)),
            _whole((1, nst)),
            _whole((1, d)),
            _whole(glu_a.shape),
            _whole(glu_b.shape),
        ],
        out_specs=[
            pl.BlockSpec((t, d), lambda i: (i, 0)),
            _whole((bm, nst)),
            _whole((bm, nst)),
        ],
        out_shape=[
            jax.ShapeDtypeStruct((r, d), _F32),
            jax.ShapeDtypeStruct((bm, nst), _F32),
            jax.ShapeDtypeStruct((bm, nst), _F32),
        ],
        compiler_params=pltpu.CompilerParams(
            dimension_semantics=("arbitrary",), vmem_limit_bytes=VMEM_LIMIT_BYTES),
        name="s5_layer",
    )(x, mod, xr0, xi0, g_pre, g_post, wb, wc, lbr, lbi, dskip, glu_a, glu_b)


def _time_major(a):
    b, l, c = a.shape
    return jnp.swapaxes(a, 0, 1).reshape(l * b, c)


def _batch_major(a, b):
    lb, c = a.shape
    return jnp.swapaxes(a.reshape(lb // b, b, c), 0, 1)


def _stacked_batch_major(states, b):
    st = jnp.stack(states)
    n, tb, c = st.shape
    return jnp.swapaxes(st.reshape(n, tb // b, b, c), 1, 2)


def kernel(x_prompt, x_sample, c_prompt, c_sample, state_pool, state_ssm_re, state_ssm_im, state_ffn_conv, ada_w, ada_b, mix_pre_g, mix_post_g, ffn_pre_g, ffn_post_g, pool_w, pool_scale, ssm_A_re, ssm_A_im, ssm_log_dt, ssm_B_re, ssm_B_im, ssm_C_re, ssm_C_im, ssm_D, ssm_glu_a, ssm_glu_b, ffn_w_up, ffn_conv_w, ffn_conv_b, ffn_w_down):
    depth = ada_w.shape[0]
    bp, lp, d = x_prompt.shape
    bs, ls, _ = x_sample.shape
    nup = ffn_w_up.shape[-1]
    groups = ssm_A_re.shape[1]
    nst = groups * SSM_P

    mod_p, mod_s = _ada_mod(c_prompt, c_sample, ada_w, ada_b)
    gains = (mix_pre_g, mix_post_g, ffn_pre_g, ffn_post_g)
    w_up = ffn_w_up.astype(_BF16)
    w_dn = ffn_w_down.astype(_BF16)

    geo = {
        "p": dict(bm=bp, pool_ffn_tt=64, ffn_tt=128, s5_tt=64, pos_base=0, fuse_pool=True),
        "s": dict(bm=bs, pool_ffn_tt=ls // 2, ffn_tt=ls // 2, s5_tt=ls // 4, pos_base=PAST_LEN, fuse_pool=False,
                  pool_tt=ls),
    }
    ys = {"p": x_prompt, "s": x_sample}
    mods = {"p": mod_p, "s": mod_s}
    pool_out = {"p": [], "s": []}
    ssm_out = {"p": [], "s": []}
    conv_out = {"p": [], "s": []}

    for l in range(depth):
        j = l // 2
        pool = {"p": None, "s": None}
        if l % 2 == 0:
            pw = pool_w[j].astype(_BF16)
            ps = pool_scale[j].reshape(1, d)
            pool["p"] = (jnp.zeros((POOL_STATE * bp, d), _F32), None, pw, ps)
            pool["s"] = (state_pool, j, pw, ps)
        else:
            lb_re, lb_im, wb, wc = _s5_prep(ssm_A_re[j], ssm_A_im[j], ssm_log_dt[j], ssm_B_re[j], ssm_B_im[j],
                                            ssm_C_re[j], ssm_C_im[j])
            lbr = lb_re.reshape(1, nst)
            lbi = lb_im.reshape(1, nst)
            ga = ssm_glu_a[j].astype(_BF16)
            gb = ssm_glu_b[j].astype(_BF16)
            for k in ("p", "s"):
                gk = geo[k]
                if k == "p":
                    xr0 = jnp.zeros((bp, nst), _F32)
                    xi0 = xr0
                else:
                    xr0 = state_ssm_re[j].reshape(bs, nst)
                    xi0 = state_ssm_im[j].reshape(bs, nst)
                ys[k], xr, xi = _s5_layer(ys[k], mods[k], xr0, xi0, mix_pre_g, mix_post_g,
                                          wb, wc, lbr, lbi, ssm_D[j].reshape(1, d), ga, gb,
                                          layer=l, bm=gk["bm"], tt=gk["s5_tt"])
                ssm_out[k].append((xr.reshape(gk["bm"], groups, SSM_P), xi.reshape(gk["bm"], groups, SSM_P)))

        for k in ("p", "s"):
            gk = geo[k]
            if k == "p":
                cv0 = jnp.zeros(((CONV_W - 1) * bp, nup), _F32)
            else:
                cv0 = _time_major(state_ffn_conv[l])
            ffn = (cv0, w_up, ffn_conv_w, ffn_conv_b, w_dn)
            common = dict(layer=l, bm=gk["bm"], pos_base=gk["pos_base"])
            if pool[k] is not None and not gk["fuse_pool"]:
                ys[k], st = _token_layer(ys[k], mods[k], gains, tt=gk["pool_tt"], pool=pool[k],
                                         in_batch_major=True, state_batch_major=True, **common)
                pool_out[k].append(st)
                pool[k] = None
            tt = gk["ffn_tt"] if pool[k] is None else gk["pool_ffn_tt"]
            outs = _token_layer(ys[k], mods[k], gains, tt=tt, pool=pool[k], ffn=ffn,
                                in_batch_major=(l == 0 and pool[k] is not None),
                                out_batch_major=(k == "p" and l == depth - 1), **common)
            ys[k], cv = outs[0], outs[-1]
            if pool[k] is not None:
                pool_out[k].append(_batch_major(outs[1], gk["bm"]))
            conv_out[k].append(cv)

    y_prompt = ys["p"]
    y_sample = _batch_major(ys["s"], bs)
    return (y_prompt, y_sample,
            jnp.stack(pool_out["p"]), jnp.stack(pool_out["s"]),
            jnp.stack([a for a, _ in ssm_out["p"]]), jnp.stack([b for _, b in ssm_out["p"]]),
            jnp.stack([a for a, _ in ssm_out["s"]]), jnp.stack([b for _, b in ssm_out["s"]]),
            _stacked_batch_major(conv_out["p"], bp), _stacked_batch_major(conv_out["s"], bs))
```

```python
import functools
import math

import jax
import jax.numpy as jnp
from jax import lax
from jax.experimental import pallas as pl
from jax.experimental.pallas import tpu as pltpu

POOL_WINDOWS = (2, 4, 8, 16)
POOL_STATE = max(POOL_WINDOWS) - 1
SSM_GC = 16
SSM_P = 64
CONV_W = 3
EPS = 1e-6
PAST_LEN = 16384

V7X_SUBLANES = 8
V7X_LANES = 128
V7X_MXU_DIM = 256
VMEM_LIMIT_BYTES = 56 * 1024 * 1024

_F32 = jnp.float32
_BF16 = jnp.bfloat16


def _resident(block_shape, index_map):
    return pl.BlockSpec(block_shape, index_map, pipeline_mode=pl.Buffered(1))


def _whole(shape):
    nd = len(shape)
    return _resident(shape, lambda i: (0,) * nd)


def _layer_slab(shape, l):
    nd = len(shape)
    return _resident((None,) + tuple(shape[1:]), lambda i: (l,) + (0,) * (nd - 1))


def _unit_rms(x):
    return x * lax.rsqrt(jnp.mean(x * x, axis=-1, keepdims=True) + EPS)


def _pre_mod(x, g, scale, shift, bm):
    t, d = x.shape
    r = _unit_rms(x).reshape(t // bm, bm, d)
    return (r * (g * (1.0 + scale))[None] + shift[None]).reshape(t, d)


def _gated_residual(x, m, g, gate, bm):
    t, d = x.shape
    r = _unit_rms(m).reshape(t // bm, bm, d)
    return x + (r * (gate * g)[None]).reshape(t, d)


def _gelu(x):
    return 0.5 * x * (1.0 + lax.erf(x * math.sqrt(0.5)))


def _dot(a, b):
    return jnp.dot(a, b, preferred_element_type=_F32)


def _load_rows(x_ref, batch_major):
    if not batch_major:
        return x_ref[...]
    b, tt, d = x_ref.shape
    return jnp.swapaxes(x_ref[...], 0, 1).reshape(tt * b, d)


def _store_rows(y_ref, y, batch_major):
    if not batch_major:
        y_ref[...] = y
    else:
        b, tt, d = y_ref.shape
        y_ref[...] = jnp.swapaxes(y.reshape(tt, b, d), 0, 1)


def _mod3(mod_ref, k, d):
    return tuple(mod_ref[:, (3 * k + n) * d:(3 * k + n + 1) * d] for n in range(3))


def _cast_kernel(w_ref, o_ref):
    o_ref[...] = w_ref[...].astype(o_ref.dtype)


def _to_bf16(w, rows):
    l, r, c = w.shape
    spec = pl.BlockSpec((None, rows, c), lambda i, j: (i, j, 0))
    return pl.pallas_call(
        _cast_kernel,
        grid=(l, r // rows),
        in_specs=[spec],
        out_specs=spec,
        out_shape=jax.ShapeDtypeStruct(w.shape, _BF16),
        compiler_params=pltpu.CompilerParams(
            dimension_semantics=("arbitrary", "arbitrary"), vmem_limit_bytes=VMEM_LIMIT_BYTES),
        name="to_bf16",
    )(w)


def _ada_kernel(cp_ref, cs_ref, w_ref, b_ref, mp_ref, ms_ref):
    bp = cp_ref.shape[0]
    c = jnp.concatenate([cp_ref[...], cs_ref[...]], axis=0)
    s = (c * jax.nn.sigmoid(c)).astype(_BF16)
    o = _dot(s, w_ref[...].astype(_BF16)) + b_ref[...]
    mp_ref[...] = o[:bp]
    ms_ref[...] = o[bp:]


def _ada_mod(c_prompt, c_sample, ada_w, ada_b):
    depth, d, n = ada_w.shape
    tn = 1536
    bp, bs = c_prompt.shape[0], c_sample.shape[0]
    return pl.pallas_call(
        _ada_kernel,
        grid=(depth, n // tn),
        in_specs=[
            pl.BlockSpec((bp, d), lambda l, j: (0, 0)),
            pl.BlockSpec((bs, d), lambda l, j: (0, 0)),
            pl.BlockSpec((None, d, tn), lambda l, j: (l, 0, j)),
            pl.BlockSpec((None, 1, tn), lambda l, j: (l, 0, j)),
        ],
        out_specs=[
            pl.BlockSpec((None, bp, tn), lambda l, j: (l, 0, j)),
            pl.BlockSpec((None, bs, tn), lambda l, j: (l, 0, j)),
        ],
        out_shape=[
            jax.ShapeDtypeStruct((depth, bp, n), _F32),
            jax.ShapeDtypeStruct((depth, bs, n), _F32),
        ],
        compiler_params=pltpu.CompilerParams(
            dimension_semantics=("arbitrary", "arbitrary"),
            vmem_limit_bytes=VMEM_LIMIT_BYTES),
        name="ada_mod",
    )(c_prompt, c_sample, ada_w, ada_b.reshape(depth, 1, n))


def _pool_mix(h, st, pw_ref, ps_ref, i, *, bm, pos_base):
    t, d = h.shape
    tt = t // bm
    gc = d // len(POOL_WINDOWS)
    ext = jnp.concatenate([st, h], axis=0)
    pos = None
    if pos_base + 1 < max(POOL_WINDOWS):
        row = lax.broadcasted_iota(jnp.int32, (t, V7X_LANES), 0)
        pos = pos_base + i * tt + lax.shift_right_logical(row, bm.bit_length() - 1)
    parts = []
    for gi, w in enumerate(POOL_WINDOWS):
        cols = slice(gi * gc, (gi + 1) * gc)
        s = ext[:, cols]
        span = 1
        while span < w:
            n = s.shape[0]
            s = s[span * bm:] + s[:n - span * bm]
            span *= 2
        k0 = (POOL_STATE - (w - 1)) * bm
        wsum = s[k0:k0 + t]
        if pos is None:
            inv = 1.0 / w
        else:
            inv = 1.0 / jnp.minimum(pos + 1, w).astype(_F32)
            inv = jnp.concatenate([inv] * (gc // V7X_LANES), axis=1)
        pooled = wsum * inv - h[:, cols]
        parts.append(_dot(pooled.astype(_BF16), pw_ref[gi].astype(_BF16)))
    return jnp.concatenate(parts, axis=1) * ps_ref[...], ext[t:, :]


def _conv_ffn(f, cv_ref, wup_ref, cw_ref, cb_ref, wdn_ref, h_ref, *, bm):
    t = f.shape[0]
    hid = wdn_ref.shape[0]
    hc = V7X_MXU_DIM

    def conv_cols(c0):
        cols = slice(c0, c0 + hc)
        up = _dot(f, wup_ref[:, cols])
        ext = jnp.concatenate([cv_ref[:, cols], up], axis=0)
        cv_ref[:, cols] = ext[t:, :]
        conv = cb_ref[:, cols]
        for k in range(CONV_W):
            conv = conv + ext[k * bm:k * bm + t, :] * cw_ref[k:k + 1, cols]
        return conv

    for c in range(hid // hc):
        gate_c = conv_cols(c * hc)
        val_c = conv_cols(hid + c * hc)
        h_ref[:, c * hc:(c + 1) * hc] = (_gelu(gate_c) * val_c).astype(_BF16)
    return _dot(h_ref[...], wdn_ref[...].astype(_BF16))


def _layer_kernel(*refs, layer, bm, pos_base, with_pool, with_ffn, in_batch_major, out_batch_major,
                  state_batch_major):
    refs = list(refs)
    x_ref, mod_ref = refs[:2]
    del refs[:2]
    if with_pool:
        st0_ref, mpre_ref, mpost_ref, pw_ref, ps_ref = refs[:5]
        del refs[:5]
    if with_ffn:
        cv0_ref, fpre_ref, fpost_ref, wup_ref, cw_ref, cb_ref, wdn_ref = refs[:7]
        del refs[:7]
    y_ref = refs.pop(0)
    if with_pool:
        st_ref = refs.pop(0)
    if with_ffn:
        cv_ref, h_ref = refs
    i = pl.program_id(0)
    d = mod_ref.shape[1] // 6
    lrow = slice(layer, layer + 1)

    @pl.when(i == 0)
    def _():
        if with_pool and not state_batch_major:
            st_ref[...] = st0_ref[...]
        if with_ffn:
            cv_ref[...] = cv0_ref[...]

    x = _load_rows(x_ref, in_batch_major)
    if with_pool:
        shift, scale, gate = _mod3(mod_ref, 0, d)
        h = _pre_mod(x, mpre_ref[lrow, :], scale, shift, bm)
        st = _load_rows(st0_ref, True) if state_batch_major else st_ref[...]
        m, st = _pool_mix(h, st, pw_ref, ps_ref, i, bm=bm, pos_base=pos_base)
        _store_rows(st_ref, st, state_batch_major)
        x = _gated_residual(x, m, mpost_ref[lrow, :], gate, bm)
    if with_ffn:
        shift, scale, gate = _mod3(mod_ref, 1, d)
        f = _pre_mod(x, fpre_ref[lrow, :], scale, shift, bm).astype(_BF16)
        o = _conv_ffn(f, cv_ref, wup_ref, cw_ref, cb_ref.at[lrow, :], wdn_ref, h_ref, bm=bm)
        x = _gated_residual(x, o, fpost_ref[lrow, :], gate, bm)
    _store_rows(y_ref, x, out_batch_major)


def _token_layer(x, mod, gains, *, layer, bm, tt, pos_base=0, pool=None, ffn=None,
                 in_batch_major=False, out_batch_major=False, state_batch_major=False):
    mix_pre_g, mix_post_g, ffn_pre_g, ffn_post_g = gains
    d = x.shape[-1]
    nrows = x.shape[0] * x.shape[1] if in_batch_major else x.shape[0]
    t = tt * bm
    assert not state_batch_major or nrows == t
    tm_spec = pl.BlockSpec((t, d), lambda i: (i, 0))
    bm_spec = pl.BlockSpec((bm, tt, d), lambda i: (0, i, 0))

    args = [x, mod]
    in_specs = [bm_spec if in_batch_major else tm_spec, _layer_slab(mod.shape, layer)]
    out_specs = [bm_spec if out_batch_major else tm_spec]
    out_shape = [jax.ShapeDtypeStruct((bm, nrows // bm, d) if out_batch_major else (nrows, d), _F32)]
    scratch = []
    if pool is not None:
        st0, st_slab, pool_w, pool_scale = pool
        st_shape = st0.shape if st_slab is None else st0.shape[1:]
        args += [st0, mix_pre_g, mix_post_g, pool_w, pool_scale]
        in_specs += [_whole(st0.shape) if st_slab is None else _layer_slab(st0.shape, st_slab)]
        in_specs += [_whole(a.shape) for a in args[-4:]]
        out_specs += [_whole(st_shape)]
        out_shape += [jax.ShapeDtypeStruct(st_shape, _F32)]
    if ffn is not None:
        cv0, w_up, conv_w, conv_b, w_down = ffn
        hid = w_down.shape[1]
        assert hid % V7X_MXU_DIM == 0 and cv0.shape[1] == 2 * hid
        args += [cv0, ffn_pre_g, ffn_post_g, w_up, conv_w, conv_b, w_down]
        in_specs += [_whole(cv0.shape), _whole(ffn_pre_g.shape), _whole(ffn_post_g.shape),
                     _layer_slab(w_up.shape, layer), _layer_slab(conv_w.shape, layer),
                     _whole(conv_b.shape), _layer_slab(w_down.shape, layer)]
        out_specs += [_whole(cv0.shape)]
        out_shape += [jax.ShapeDtypeStruct(cv0.shape, _F32)]
        scratch += [pltpu.VMEM((t, hid), _BF16)]
    return pl.pallas_call(
        functools.partial(_layer_kernel, layer=layer, bm=bm, pos_base=pos_base,
                          with_pool=pool is not None, with_ffn=ffn is not None,
                          in_batch_major=in_batch_major, out_batch_major=out_batch_major,
                          state_batch_major=state_batch_major),
        grid=(nrows // t,),
        in_specs=in_specs,
        out_specs=out_specs,
        out_shape=out_shape,
        scratch_shapes=scratch,
        compiler_params=pltpu.CompilerParams(
            dimension_semantics=("arbitrary",), vmem_limit_bytes=VMEM_LIMIT_BYTES),
        name="_".join(n for n, on in (("pool", pool), ("ffn", ffn)) if on is not None) + "_layer",
    )(*args)


def _s5_prep_kernel(are_ref, aim_ref, ldt_ref, btr_ref, bti_ref, ctr_ref, cti_ref,
                    lbr_ref, lbi_ref, wb_ref, wc_ref):
    a_re, a_im = are_ref[...], aim_ref[...]
    dt = jnp.exp(ldt_ref[...])
    mag = jnp.exp(a_re * dt)
    ang = a_im * dt
    lb_re = mag * jnp.cos(ang)
    lb_im = mag * jnp.sin(ang)
    n_re = lb_re - 1.0
    n_im = lb_im
    den = a_re * a_re + a_im * a_im
    f_re = ((n_re * a_re + n_im * a_im) / den)[:, None, :]
    f_im = ((n_im * a_re - n_re * a_im) / den)[:, None, :]
    lbr_ref[...] = lb_re
    lbi_ref[...] = lb_im
    b_re, b_im = btr_ref[...], bti_ref[...]
    bb_re = (f_re * b_re - f_im * b_im).astype(wb_ref.dtype)
    bb_im = (f_re * b_im + f_im * b_re).astype(wb_ref.dtype)
    c_re = ctr_ref[...].astype(wc_ref.dtype)
    nc_im = (-cti_ref[...]).astype(wc_ref.dtype)
    wb_ref[...] = jnp.zeros(wb_ref.shape, wb_ref.dtype)
    wc_ref[...] = jnp.zeros(wc_ref.shape, wc_ref.dtype)
    g, gc, p = b_re.shape
    gpt = wb_ref.shape[1] // gc
    half = gpt * p
    for gi in range(g):
        j, gl = divmod(gi, gpt)
        rows, cols = slice(gl * gc, (gl + 1) * gc), slice(gl * p, (gl + 1) * p)
        icols = slice(half + gl * p, half + (gl + 1) * p)
        wb_ref[j, rows, cols] = bb_re[gi]
        wb_ref[j, rows, icols] = bb_im[gi]
        wc_ref[j, cols, rows] = c_re[gi]
        wc_ref[j, icols, rows] = nc_im[gi]


def _s5_prep(a_re, a_im, log_dt, b_re, b_im, c_re, c_im):
    g, p = a_re.shape
    gc = b_re.shape[-1]
    gpt = V7X_MXU_DIM // gc
    tr = lambda a: jnp.swapaxes(a, 1, 2)
    return pl.pallas_call(
        _s5_prep_kernel,
        out_shape=[
            jax.ShapeDtypeStruct((g, p), _F32),
            jax.ShapeDtypeStruct((g, p), _F32),
            jax.ShapeDtypeStruct((g // gpt, gpt * gc, 2 * gpt * p), _BF16),
            jax.ShapeDtypeStruct((g // gpt, 2 * gpt * p, gpt * gc), _BF16),
        ],
        compiler_params=pltpu.CompilerParams(vmem_limit_bytes=VMEM_LIMIT_BYTES),
        name="s5_prep",
    )(a_re, a_im, log_dt.reshape(g, 1), tr(b_re), tr(b_im), tr(c_re), tr(c_im))


def _s5_kernel(x_ref, mod_ref, xr0_ref, xi0_ref, gpre_ref, gpost_ref, wb_ref, wc_ref,
               lbr_ref, lbi_ref, dsk_ref, ga_ref, gb_ref,
               y_ref, xr_ref, xi_ref, *, layer, bm):
    i = pl.program_id(0)
    t, d = x_ref.shape
    tt = t // bm
    ntile, kin, ncol2 = wb_ref.shape
    ncol = ncol2 // 2
    sub = V7X_SUBLANES
    lrow = slice(layer, layer + 1)

    @pl.when(i == 0)
    def _():
        xr_ref[...] = xr0_ref[...]
        xi_ref[...] = xi0_ref[...]

    x = x_ref[...]
    shift, scale, gate = _mod3(mod_ref, 0, d)
    h = _pre_mod(x, gpre_ref[lrow, :], scale, shift, bm)
    u = h.astype(_BF16)

    ys = []
    for j in range(ntile):
        scol = slice(j * ncol, (j + 1) * ncol)
        bu = _dot(u[:, j * kin:(j + 1) * kin], wb_ref[j])
        lr = jnp.broadcast_to(lbr_ref[:, scol], (sub, ncol))
        li = jnp.broadcast_to(lbi_ref[:, scol], (sub, ncol))
        blocks = [None] * (t // sub)
        for rb in range(bm // sub):
            rows = slice(rb * sub, (rb + 1) * sub)
            pr, pi = xr_ref[rows, scol], xi_ref[rows, scol]
            for ts in range(tt):
                r0 = ts * bm + rb * sub
                nr = lr * pr - li * pi + bu[r0:r0 + sub, :ncol]
                ni = lr * pi + li * pr + bu[r0:r0 + sub, ncol:]
                blocks[r0 // sub] = jnp.concatenate([nr, ni], axis=1)
                pr, pi = nr, ni
            xr_ref[rows, scol] = pr
            xi_ref[rows, scol] = pi
        ys.append(_dot(jnp.concatenate(blocks, axis=0).astype(_BF16), wc_ref[j]))
    y = jnp.concatenate(ys, axis=1) + dsk_ref[...] * h
    g = _gelu(y).astype(_BF16)
    out = _dot(g, ga_ref[...].astype(_BF16)) * jax.nn.sigmoid(_dot(g, gb_ref[...].astype(_BF16)))
    y_ref[...] = _gated_residual(x, out, gpost_ref[lrow, :], gate, bm)


def _s5_layer(x, mod, xr0, xi0, g_pre, g_post, wb, wc, lbr, lbi, dskip, glu_a, glu_b, *, layer, bm, tt):
    r, d = x.shape
    t = tt * bm
    nst = xr0.shape[1]
    return pl.pallas_call(
        functools.partial(_s5_kernel, layer=layer, bm=bm),
        grid=(r // t,),
        in_specs=[
            pl.BlockSpec((t, d), lambda i: (i, 0)),
            _layer_slab(mod.shape, layer),
            _whole((bm, nst)),
            _whole((bm, nst)),
            _whole(g_pre.shape),
            _whole(g_post.shape),
            _whole(wb.shape),
            _whole(wc.shape),
            _whole((1, nst)),
            _whole((1, nst)),
            _whole((1, d)),
            _whole(glu_a.shape),
            _whole(glu_b.shape),
        ],
        out_specs=[
            pl.BlockSpec((t, d), lambda i: (i, 0)),
            _whole((bm, nst)),
            _whole((bm, nst)),
        ],
        out_shape=[
            jax.ShapeDtypeStruct((r, d), _F32),
            jax.ShapeDtypeStruct((bm, nst), _F32),
            jax.ShapeDtypeStruct((bm, nst), _F32),
        ],
        compiler_params=pltpu.CompilerParams(
            dimension_semantics=("arbitrary",), vmem_limit_bytes=VMEM_LIMIT_BYTES),
        name="s5_layer",
    )(x, mod, xr0, xi0, g_pre, g_post, wb, wc, lbr, lbi, dskip, glu_a, glu_b)


def _time_major(a):
    b, l, c = a.shape
    return jnp.swapaxes(a, 0, 1).reshape(l * b, c)


def _batch_major(a, b):
    lb, c = a.shape
    return jnp.swapaxes(a.reshape(lb // b, b, c), 0, 1)


def _stacked_batch_major(states, b):
    st = jnp.stack(states)
    n, tb, c = st.shape
    return jnp.swapaxes(st.reshape(n, tb // b, b, c), 1, 2)


def kernel(x_prompt, x_sample, c_prompt, c_sample, state_pool, state_ssm_re, state_ssm_im, state_ffn_conv, ada_w, ada_b, mix_pre_g, mix_post_g, ffn_pre_g, ffn_post_g, pool_w, pool_scale, ssm_A_re, ssm_A_im, ssm_log_dt, ssm_B_re, ssm_B_im, ssm_C_re, ssm_C_im, ssm_D, ssm_glu_a, ssm_glu_b, ffn_w_up, ffn_conv_w, ffn_conv_b, ffn_w_down):
    depth = ada_w.shape[0]
    bp, lp, d = x_prompt.shape
    bs, ls, _ = x_sample.shape
    nup = ffn_w_up.shape[-1]
    groups = ssm_A_re.shape[1]
    nst = groups * SSM_P

    mod_p, mod_s = _ada_mod(c_prompt, c_sample, ada_w, ada_b)
    gains = (mix_pre_g, mix_post_g, ffn_pre_g, ffn_post_g)
    w_up = _to_bf16(ffn_w_up, 512)
    w_dn = ffn_w_down

    geo = {
        "p": dict(bm=bp, pool_ffn_tt=64, ffn_tt=128, s5_tt=64, pos_base=0, fuse_pool=True),
        "s": dict(bm=bs, pool_ffn_tt=ls // 2, ffn_tt=ls // 2, s5_tt=ls // 4, pos_base=PAST_LEN, fuse_pool=False,
                  pool_tt=ls),
    }
    ys = {"p": x_prompt, "s": x_sample}
    mods = {"p": mod_p, "s": mod_s}
    pool_out = {"p": [], "s": []}
    ssm_out = {"p": [], "s": []}
    conv_out = {"p": [], "s": []}

    for l in range(depth):
        j = l // 2
        pool = {"p": None, "s": None}
        if l % 2 == 0:
            pw = pool_w[j]
            ps = pool_scale[j].reshape(1, d)
            pool["p"] = (jnp.zeros((POOL_STATE * bp, d), _F32), None, pw, ps)
            pool["s"] = (state_pool, j, pw, ps)
        else:
            lb_re, lb_im, wb, wc = _s5_prep(ssm_A_re[j], ssm_A_im[j], ssm_log_dt[j], ssm_B_re[j], ssm_B_im[j],
                                            ssm_C_re[j], ssm_C_im[j])
            lbr = lb_re.reshape(1, nst)
            lbi = lb_im.reshape(1, nst)
            ga = ssm_glu_a[j]
            gb = ssm_glu_b[j]
            for k in ("p", "s"):
                gk = geo[k]
                if k == "p":
                    xr0 = jnp.zeros((bp, nst), _F32)
                    xi0 = xr0
                else:
                    xr0 = state_ssm_re[j].reshape(bs, nst)
                    xi0 = state_ssm_im[j].reshape(bs, nst)
                ys[k], xr, xi = _s5_layer(ys[k], mods[k], xr0, xi0, mix_pre_g, mix_post_g,
                                          wb, wc, lbr, lbi, ssm_D[j].reshape(1, d), ga, gb,
                                          layer=l, bm=gk["bm"], tt=gk["s5_tt"])
                ssm_out[k].append((xr.reshape(gk["bm"], groups, SSM_P), xi.reshape(gk["bm"], groups, SSM_P)))

        for k in ("p", "s"):
            gk = geo[k]
            if k == "p":
                cv0 = jnp.zeros(((CONV_W - 1) * bp, nup), _F32)
            else:
                cv0 = _time_major(state_ffn_conv[l])
            ffn = (cv0, w_up, ffn_conv_w, ffn_conv_b, w_dn)
            common = dict(layer=l, bm=gk["bm"], pos_base=gk["pos_base"])
            if pool[k] is not None and not gk["fuse_pool"]:
                ys[k], st = _token_layer(ys[k], mods[k], gains, tt=gk["pool_tt"], pool=pool[k],
                                         in_batch_major=True, state_batch_major=True, **common)
                pool_out[k].append(st)
                pool[k] = None
            tt = gk["ffn_tt"] if pool[k] is None else gk["pool_ffn_tt"]
            outs = _token_layer(ys[k], mods[k], gains, tt=tt, pool=pool[k], ffn=ffn,
                                in_batch_major=(l == 0 and pool[k] is not None),
                                out_batch_major=(k == "p" and l == depth - 1), **common)
            ys[k], cv = outs[0], outs[-1]
            if pool[k] is not None:
                pool_out[k].append(_batch_major(outs[1], gk["bm"]))
            conv_out[k].append(cv)

    y_prompt = ys["p"]
    y_sample = _batch_major(ys["s"], bs)
    return (y_prompt, y_sample,
            jnp.stack(pool_out["p"]), jnp.stack(pool_out["s"]),
            jnp.stack([a for a, _ in ssm_out["p"]]), jnp.stack([b for _, b in ssm_out["p"]]),
            jnp.stack([a for a, _ in ssm_out["s"]]), jnp.stack([b for _, b in ssm_out["s"]]),
            _stacked_batch_major(conv_out["p"], bp), _stacked_batch_major(conv_out["s"], bs))
```

```python
import functools
import math

import jax
import jax.numpy as jnp
from jax import lax
from jax.experimental import pallas as pl
from jax.experimental.pallas import tpu as pltpu

POOL_WINDOWS = (2, 4, 8, 16)
POOL_STATE = max(POOL_WINDOWS) - 1
SSM_GC = 16
SSM_P = 64
CONV_W = 3
EPS = 1e-6
PAST_LEN = 16384

V7X_SUBLANES = 8
V7X_LANES = 128
V7X_MXU_DIM = 256
VMEM_LIMIT_BYTES = 56 * 1024 * 1024

PROMPT_POOL_FFN_TT = 64
PROMPT_FFN_TT = 128
PROMPT_S5_TT = 64
ADA_TILE_N = 1536
CAST_TILE_ROWS = 256

_F32 = jnp.float32
_BF16 = jnp.bfloat16


def _resident(block_shape, index_map):
    return pl.BlockSpec(block_shape, index_map, pipeline_mode=pl.Buffered(1))


def _whole(shape):
    nd = len(shape)
    return _resident(shape, lambda i: (0,) * nd)


def _layer_slab(shape, l):
    nd = len(shape)
    return _resident((None,) + tuple(shape[1:]), lambda i: (l,) + (0,) * (nd - 1))


def _unit_rms(x):
    return x * lax.rsqrt(jnp.mean(x * x, axis=-1, keepdims=True) + EPS)


def _pre_mod(x, g, scale, shift, bm):
    t, d = x.shape
    r = _unit_rms(x).reshape(t // bm, bm, d)
    return (r * (g * (1.0 + scale))[None] + shift[None]).reshape(t, d)


def _gated_residual(x, m, g, gate, bm):
    t, d = x.shape
    r = _unit_rms(m).reshape(t // bm, bm, d)
    return x + (r * (gate * g)[None]).reshape(t, d)


def _gelu(x):
    return 0.5 * x * (1.0 + lax.erf(x * math.sqrt(0.5)))


def _dot(a, b):
    return jnp.dot(a, b, preferred_element_type=_F32)


def _load_rows(x_ref, batch_major):
    if not batch_major:
        return x_ref[...]
    b, tt, d = x_ref.shape
    return jnp.swapaxes(x_ref[...], 0, 1).reshape(tt * b, d)


def _store_rows(y_ref, y, batch_major):
    if not batch_major:
        y_ref[...] = y
    else:
        b, tt, d = y_ref.shape
        y_ref[...] = jnp.swapaxes(y.reshape(tt, b, d), 0, 1)


def _mod3(mod_ref, k, d):
    return tuple(mod_ref[:, (3 * k + n) * d:(3 * k + n + 1) * d] for n in range(3))


def _cast_kernel(w_ref, o_ref):
    o_ref[...] = w_ref[...].astype(o_ref.dtype)


def _to_bf16(w, rows):
    l, r, c = w.shape
    spec = pl.BlockSpec((None, rows, c), lambda i, j: (i, j, 0))
    return pl.pallas_call(
        _cast_kernel,
        grid=(l, r // rows),
        in_specs=[spec],
        out_specs=spec,
        out_shape=jax.ShapeDtypeStruct(w.shape, _BF16),
        compiler_params=pltpu.CompilerParams(
            dimension_semantics=("arbitrary", "arbitrary"), vmem_limit_bytes=VMEM_LIMIT_BYTES),
        name="to_bf16",
    )(w)


def _ada_kernel(cp_ref, cs_ref, w_ref, b_ref, mp_ref, ms_ref):
    bp = cp_ref.shape[0]
    c = jnp.concatenate([cp_ref[...], cs_ref[...]], axis=0)
    s = (c * jax.nn.sigmoid(c)).astype(_BF16)
    o = _dot(s, w_ref[...].astype(_BF16)) + b_ref[...]
    mp_ref[...] = o[:bp]
    ms_ref[...] = o[bp:]


def _ada_mod(c_prompt, c_sample, ada_w, ada_b):
    depth, d, n = ada_w.shape
    tn = ADA_TILE_N
    bp, bs = c_prompt.shape[0], c_sample.shape[0]
    return pl.pallas_call(
        _ada_kernel,
        grid=(depth, n // tn),
        in_specs=[
            pl.BlockSpec((bp, d), lambda l, j: (0, 0)),
            pl.BlockSpec((bs, d), lambda l, j: (0, 0)),
            pl.BlockSpec((None, d, tn), lambda l, j: (l, 0, j)),
            pl.BlockSpec((None, 1, tn), lambda l, j: (l, 0, j)),
        ],
        out_specs=[
            pl.BlockSpec((None, bp, tn), lambda l, j: (l, 0, j)),
            pl.BlockSpec((None, bs, tn), lambda l, j: (l, 0, j)),
        ],
        out_shape=[
            jax.ShapeDtypeStruct((depth, bp, n), _F32),
            jax.ShapeDtypeStruct((depth, bs, n), _F32),
        ],
        compiler_params=pltpu.CompilerParams(
            dimension_semantics=("arbitrary", "arbitrary"),
            vmem_limit_bytes=VMEM_LIMIT_BYTES),
        name="ada_mod",
    )(c_prompt, c_sample, ada_w, ada_b.reshape(depth, 1, n))


def _pool_mix(h, st, pw_ref, ps_ref, i, *, bm, pos_base):
    t, d = h.shape
    tt = t // bm
    gc = d // len(POOL_WINDOWS)
    ext = jnp.concatenate([st, h], axis=0)
    pos = None
    if pos_base + 1 < max(POOL_WINDOWS):
        row = lax.broadcasted_iota(jnp.int32, (t, V7X_LANES), 0)
        pos = pos_base + i * tt + lax.shift_right_logical(row, bm.bit_length() - 1)
    parts = []
    for gi, w in enumerate(POOL_WINDOWS):
        cols = slice(gi * gc, (gi + 1) * gc)
        s = ext[:, cols]
        span = 1
        while span < w:
            n = s.shape[0]
            s = s[span * bm:] + s[:n - span * bm]
            span *= 2
        k0 = (POOL_STATE - (w - 1)) * bm
        wsum = s[k0:k0 + t]
        if pos is None:
            inv = 1.0 / w
        else:
            inv = 1.0 / jnp.minimum(pos + 1, w).astype(_F32)
            inv = jnp.concatenate([inv] * (gc // V7X_LANES), axis=1)
        pooled = wsum * inv - h[:, cols]
        parts.append(_dot(pooled.astype(_BF16), pw_ref[gi].astype(_BF16)))
    return jnp.concatenate(parts, axis=1) * ps_ref[...], ext[t:, :]


def _conv_ffn(f, cv_ref, wup_ref, cw_ref, cb_ref, wdn_ref, h_ref, *, bm):
    t = f.shape[0]
    hid = wdn_ref.shape[0]
    hc = V7X_MXU_DIM

    def conv_cols(c0):
        cols = slice(c0, c0 + hc)
        up = _dot(f, wup_ref[:, cols])
        ext = jnp.concatenate([cv_ref[:, cols], up], axis=0)
        cv_ref[:, cols] = ext[t:, :]
        conv = cb_ref[:, cols]
        for k in range(CONV_W):
            conv = conv + ext[k * bm:k * bm + t, :] * cw_ref[k:k + 1, cols]
        return conv

    for c in range(hid // hc):
        gate_c = conv_cols(c * hc)
        val_c = conv_cols(hid + c * hc)
        h_ref[:, c * hc:(c + 1) * hc] = (_gelu(gate_c) * val_c).astype(_BF16)
    return _dot(h_ref[...], wdn_ref[...].astype(_BF16))


def _layer_kernel(*refs, layer, bm, pos_base, with_pool, with_ffn, in_batch_major, out_batch_major,
                  state_batch_major):
    refs = list(refs)
    x_ref, mod_ref = refs[:2]
    del refs[:2]
    if with_pool:
        st0_ref, mpre_ref, mpost_ref, pw_ref, ps_ref = refs[:5]
        del refs[:5]
    if with_ffn:
        cv0_ref, fpre_ref, fpost_ref, wup_ref, cw_ref, cb_ref, wdn_ref = refs[:7]
        del refs[:7]
    y_ref = refs.pop(0)
    if with_pool:
        st_ref = refs.pop(0)
    if with_ffn:
        cv_ref, h_ref = refs
    i = pl.program_id(0)
    d = mod_ref.shape[1] // 6
    lrow = slice(layer, layer + 1)

    @pl.when(i == 0)
    def _():
        if with_pool and not state_batch_major:
            st_ref[...] = st0_ref[...]
        if with_ffn:
            cv_ref[...] = cv0_ref[...]

    x = _load_rows(x_ref, in_batch_major)
    if with_pool:
        shift, scale, gate = _mod3(mod_ref, 0, d)
        h = _pre_mod(x, mpre_ref[lrow, :], scale, shift, bm)
        st = _load_rows(st0_ref, True) if state_batch_major else st_ref[...]
        m, st = _pool_mix(h, st, pw_ref, ps_ref, i, bm=bm, pos_base=pos_base)
        _store_rows(st_ref, st, state_batch_major)
        x = _gated_residual(x, m, mpost_ref[lrow, :], gate, bm)
    if with_ffn:
        shift, scale, gate = _mod3(mod_ref, 1, d)
        f = _pre_mod(x, fpre_ref[lrow, :], scale, shift, bm).astype(_BF16)
        o = _conv_ffn(f, cv_ref, wup_ref, cw_ref, cb_ref.at[lrow, :], wdn_ref, h_ref, bm=bm)
        x = _gated_residual(x, o, fpost_ref[lrow, :], gate, bm)
    _store_rows(y_ref, x, out_batch_major)


def _token_layer(x, mod, gains, *, layer, bm, tt, pos_base=0, pool=None, ffn=None,
                 in_batch_major=False, out_batch_major=False, state_batch_major=False):
    mix_pre_g, mix_post_g, ffn_pre_g, ffn_post_g = gains
    d = x.shape[-1]
    nrows = x.shape[0] * x.shape[1] if in_batch_major else x.shape[0]
    t = tt * bm
    assert not state_batch_major or nrows == t
    tm_spec = pl.BlockSpec((t, d), lambda i: (i, 0))
    bm_spec = pl.BlockSpec((bm, tt, d), lambda i: (0, i, 0))

    args = [x, mod]
    in_specs = [bm_spec if in_batch_major else tm_spec, _layer_slab(mod.shape, layer)]
    out_specs = [bm_spec if out_batch_major else tm_spec]
    out_shape = [jax.ShapeDtypeStruct((bm, nrows // bm, d) if out_batch_major else (nrows, d), _F32)]
    scratch = []
    if pool is not None:
        st0, st_slab, pool_w, pool_scale = pool
        st_shape = st0.shape if st_slab is None else st0.shape[1:]
        args += [st0, mix_pre_g, mix_post_g, pool_w, pool_scale]
        in_specs += [_whole(st0.shape) if st_slab is None else _layer_slab(st0.shape, st_slab)]
        in_specs += [_whole(a.shape) for a in args[-4:]]
        out_specs += [_whole(st_shape)]
        out_shape += [jax.ShapeDtypeStruct(st_shape, _F32)]
    if ffn is not None:
        cv0, w_up, conv_w, conv_b, w_down = ffn
        hid = w_down.shape[1]
        assert hid % V7X_MXU_DIM == 0 and cv0.shape[1] == 2 * hid
        args += [cv0, ffn_pre_g, ffn_post_g, w_up, conv_w, conv_b, w_down]
        in_specs += [_whole(cv0.shape), _whole(ffn_pre_g.shape), _whole(ffn_post_g.shape),
                     _layer_slab(w_up.shape, layer), _layer_slab(conv_w.shape, layer),
                     _whole(conv_b.shape), _layer_slab(w_down.shape, layer)]
        out_specs += [_whole(cv0.shape)]
        out_shape += [jax.ShapeDtypeStruct(cv0.shape, _F32)]
        scratch += [pltpu.VMEM((t, hid), _BF16)]
    return pl.pallas_call(
        functools.partial(_layer_kernel, layer=layer, bm=bm, pos_base=pos_base,
                          with_pool=pool is not None, with_ffn=ffn is not None,
                          in_batch_major=in_batch_major, out_batch_major=out_batch_major,
                          state_batch_major=state_batch_major),
        grid=(nrows // t,),
        in_specs=in_specs,
        out_specs=out_specs,
        out_shape=out_shape,
        scratch_shapes=scratch,
        compiler_params=pltpu.CompilerParams(
            dimension_semantics=("arbitrary",), vmem_limit_bytes=VMEM_LIMIT_BYTES),
        name="_".join(n for n, on in (("pool", pool), ("ffn", ffn)) if on is not None) + "_layer",
    )(*args)


def _s5_prep_kernel(are_ref, aim_ref, ldt_ref, btr_ref, bti_ref, ctr_ref, cti_ref,
                    lbr_ref, lbi_ref, wb_ref, wc_ref):
    a_re, a_im = are_ref[...], aim_ref[...]
    dt = jnp.exp(ldt_ref[...])
    mag = jnp.exp(a_re * dt)
    ang = a_im * dt
    lb_re = mag * jnp.cos(ang)
    lb_im = mag * jnp.sin(ang)
    n_re = lb_re - 1.0
    n_im = lb_im
    den = a_re * a_re + a_im * a_im
    f_re = ((n_re * a_re + n_im * a_im) / den)[:, None, :]
    f_im = ((n_im * a_re - n_re * a_im) / den)[:, None, :]
    lbr_ref[...] = lb_re
    lbi_ref[...] = lb_im
    b_re, b_im = btr_ref[...], bti_ref[...]
    bb_re = (f_re * b_re - f_im * b_im).astype(wb_ref.dtype)
    bb_im = (f_re * b_im + f_im * b_re).astype(wb_ref.dtype)
    c_re = ctr_ref[...].astype(wc_ref.dtype)
    nc_im = (-cti_ref[...]).astype(wc_ref.dtype)
    wb_ref[...] = jnp.zeros(wb_ref.shape, wb_ref.dtype)
    wc_ref[...] = jnp.zeros(wc_ref.shape, wc_ref.dtype)
    g, gc, p = b_re.shape
    gpt = wb_ref.shape[1] // gc
    half = gpt * p
    for gi in range(g):
        j, gl = divmod(gi, gpt)
        rows, cols = slice(gl * gc, (gl + 1) * gc), slice(gl * p, (gl + 1) * p)
        icols = slice(half + gl * p, half + (gl + 1) * p)
        wb_ref[j, rows, cols] = bb_re[gi]
        wb_ref[j, rows, icols] = bb_im[gi]
        wc_ref[j, cols, rows] = c_re[gi]
        wc_ref[j, icols, rows] = nc_im[gi]


def _s5_prep(a_re, a_im, log_dt, b_re, b_im, c_re, c_im):
    g, p = a_re.shape
    gc = b_re.shape[-1]
    gpt = V7X_MXU_DIM // gc
    tr = lambda a: jnp.swapaxes(a, 1, 2)
    return pl.pallas_call(
        _s5_prep_kernel,
        out_shape=[
            jax.ShapeDtypeStruct((g, p), _F32),
            jax.ShapeDtypeStruct((g, p), _F32),
            jax.ShapeDtypeStruct((g // gpt, gpt * gc, 2 * gpt * p), _BF16),
            jax.ShapeDtypeStruct((g // gpt, 2 * gpt * p, gpt * gc), _BF16),
        ],
        compiler_params=pltpu.CompilerParams(vmem_limit_bytes=VMEM_LIMIT_BYTES),
        name="s5_prep",
    )(a_re, a_im, log_dt.reshape(g, 1), tr(b_re), tr(b_im), tr(c_re), tr(c_im))


def _s5_kernel(x_ref, mod_ref, xr0_ref, xi0_ref, gpre_ref, gpost_ref, wb_ref, wc_ref,
               lbr_ref, lbi_ref, dsk_ref, ga_ref, gb_ref,
               y_ref, xr_ref, xi_ref, *, layer, bm):
    i = pl.program_id(0)
    t, d = x_ref.shape
    tt = t // bm
    ntile, kin, ncol2 = wb_ref.shape
    ncol = ncol2 // 2
    sub = V7X_SUBLANES
    lrow = slice(layer, layer + 1)

    @pl.when(i == 0)
    def _():
        xr_ref[...] = xr0_ref[...]
        xi_ref[...] = xi0_ref[...]

    x = x_ref[...]
    shift, scale, gate = _mod3(mod_ref, 0, d)
    h = _pre_mod(x, gpre_ref[lrow, :], scale, shift, bm)
    u = h.astype(_BF16)

    ys = []
    for j in range(ntile):
        scol = slice(j * ncol, (j + 1) * ncol)
        bu = _dot(u[:, j * kin:(j + 1) * kin], wb_ref[j])
        lr = jnp.broadcast_to(lbr_ref[:, scol], (sub, ncol))
        li = jnp.broadcast_to(lbi_ref[:, scol], (sub, ncol))
        blocks = [None] * (t // sub)
        for rb in range(bm // sub):
            rows = slice(rb * sub, (rb + 1) * sub)
            pr, pi = xr_ref[rows, scol], xi_ref[rows, scol]
            for ts in range(tt):
                r0 = ts * bm + rb * sub
                nr = lr * pr - li * pi + bu[r0:r0 + sub, :ncol]
                ni = lr * pi + li * pr + bu[r0:r0 + sub, ncol:]
                blocks[r0 // sub] = jnp.concatenate([nr, ni], axis=1)
                pr, pi = nr, ni
            xr_ref[rows, scol] = pr
            xi_ref[rows, scol] = pi
        ys.append(_dot(jnp.concatenate(blocks, axis=0).astype(_BF16), wc_ref[j]))
    y = jnp.concatenate(ys, axis=1) + dsk_ref[...] * h
    g = _gelu(y).astype(_BF16)
    out = _dot(g, ga_ref[...].astype(_BF16)) * jax.nn.sigmoid(_dot(g, gb_ref[...].astype(_BF16)))
    y_ref[...] = _gated_residual(x, out, gpost_ref[lrow, :], gate, bm)


def _s5_layer(x, mod, xr0, xi0, g_pre, g_post, wb, wc, lbr, lbi, dskip, glu_a, glu_b, *, layer, bm, tt):
    r, d = x.shape
    t = tt * bm
    nst = xr0.shape[1]
    return pl.pallas_call(
        functools.partial(_s5_kernel, layer=layer, bm=bm),
        grid=(r // t,),
        in_specs=[
            pl.BlockSpec((t, d), lambda i: (i, 0)),
            _layer_slab(mod.shape, layer),
            _whole((bm, nst)),
            _whole((bm, nst)),
            _whole(g_pre.shape),
            _whole(g_post.shape),
            _whole(wb.shape),
            _whole(wc.shape),
            _whole((1, nst)),
            _whole((1, nst)),
            _whole((1, d)),
            _whole(glu_a.shape),
            _whole(glu_b.shape),
        ],
        out_specs=[
            pl.BlockSpec((t, d), lambda i: (i, 0)),
            _whole((bm, nst)),
            _whole((bm, nst)),
        ],
        out_shape=[
            jax.ShapeDtypeStruct((r, d), _F32),
            jax.ShapeDtypeStruct((bm, nst), _F32),
            jax.ShapeDtypeStruct((bm, nst), _F32),
        ],
        compiler_params=pltpu.CompilerParams(
            dimension_semantics=("arbitrary",), vmem_limit_bytes=VMEM_LIMIT_BYTES),
        name="s5_layer",
    )(x, mod, xr0, xi0, g_pre, g_post, wb, wc, lbr, lbi, dskip, glu_a, glu_b)


def _time_major(a):
    b, l, c = a.shape
    return jnp.swapaxes(a, 0, 1).reshape(l * b, c)


def _batch_major(a, b):
    lb, c = a.shape
    return jnp.swapaxes(a.reshape(lb // b, b, c), 0, 1)


def _stacked_batch_major(states, b):
    st = jnp.stack(states)
    n, tb, c = st.shape
    return jnp.swapaxes(st.reshape(n, tb // b, b, c), 1, 2)


def kernel(x_prompt, x_sample, c_prompt, c_sample, state_pool, state_ssm_re, state_ssm_im, state_ffn_conv, ada_w, ada_b, mix_pre_g, mix_post_g, ffn_pre_g, ffn_post_g, pool_w, pool_scale, ssm_A_re, ssm_A_im, ssm_log_dt, ssm_B_re, ssm_B_im, ssm_C_re, ssm_C_im, ssm_D, ssm_glu_a, ssm_glu_b, ffn_w_up, ffn_conv_w, ffn_conv_b, ffn_w_down):
    depth = ada_w.shape[0]
    bp, lp, d = x_prompt.shape
    bs, ls, _ = x_sample.shape
    nup = ffn_w_up.shape[-1]
    groups = ssm_A_re.shape[1]
    nst = groups * SSM_P

    mod_p, mod_s = _ada_mod(c_prompt, c_sample, ada_w, ada_b)
    gains = (mix_pre_g, mix_post_g, ffn_pre_g, ffn_post_g)
    w_up = _to_bf16(ffn_w_up, CAST_TILE_ROWS)
    w_dn = ffn_w_down

    geo = {
        "p": dict(bm=bp, pool_ffn_tt=PROMPT_POOL_FFN_TT, ffn_tt=PROMPT_FFN_TT, s5_tt=PROMPT_S5_TT, pos_base=0,
                  fuse_pool=True),
        "s": dict(bm=bs, pool_ffn_tt=ls // 2, ffn_tt=ls // 2, s5_tt=ls // 2, pos_base=PAST_LEN, fuse_pool=False,
                  pool_tt=ls),
    }
    ys = {"p": x_prompt, "s": x_sample}
    mods = {"p": mod_p, "s": mod_s}
    pool_out = {"p": [], "s": []}
    ssm_out = {"p": [], "s": []}
    conv_out = {"p": [], "s": []}

    for l in range(depth):
        j = l // 2
        pool = {"p": None, "s": None}
        if l % 2 == 0:
            pw = pool_w[j]
            ps = pool_scale[j].reshape(1, d)
            pool["p"] = (jnp.zeros((POOL_STATE * bp, d), _F32), None, pw, ps)
            pool["s"] = (state_pool, j, pw, ps)
        else:
            lb_re, lb_im, wb, wc = _s5_prep(ssm_A_re[j], ssm_A_im[j], ssm_log_dt[j], ssm_B_re[j], ssm_B_im[j],
                                            ssm_C_re[j], ssm_C_im[j])
            lbr = lb_re.reshape(1, nst)
            lbi = lb_im.reshape(1, nst)
            ga = ssm_glu_a[j]
            gb = ssm_glu_b[j]
            for k in ("p", "s"):
                gk = geo[k]
                if k == "p":
                    xr0 = jnp.zeros((bp, nst), _F32)
                    xi0 = xr0
                else:
                    xr0 = state_ssm_re[j].reshape(bs, nst)
                    xi0 = state_ssm_im[j].reshape(bs, nst)
                ys[k], xr, xi = _s5_layer(ys[k], mods[k], xr0, xi0, mix_pre_g, mix_post_g,
                                          wb, wc, lbr, lbi, ssm_D[j].reshape(1, d), ga, gb,
                                          layer=l, bm=gk["bm"], tt=gk["s5_tt"])
                ssm_out[k].append((xr.reshape(gk["bm"], groups, SSM_P), xi.reshape(gk["bm"], groups, SSM_P)))

        for k in ("p", "s"):
            gk = geo[k]
            if k == "p":
                cv0 = jnp.zeros(((CONV_W - 1) * bp, nup), _F32)
            else:
                cv0 = _time_major(state_ffn_conv[l])
            ffn = (cv0, w_up, ffn_conv_w, ffn_conv_b, w_dn)
            common = dict(layer=l, bm=gk["bm"], pos_base=gk["pos_base"])
            if pool[k] is not None and not gk["fuse_pool"]:
                ys[k], st = _token_layer(ys[k], mods[k], gains, tt=gk["pool_tt"], pool=pool[k],
                                         in_batch_major=True, state_batch_major=True, **common)
                pool_out[k].append(st)
                pool[k] = None
            tt = gk["ffn_tt"] if pool[k] is None else gk["pool_ffn_tt"]
            outs = _token_layer(ys[k], mods[k], gains, tt=tt, pool=pool[k], ffn=ffn,
                                in_batch_major=(l == 0 and pool[k] is not None),
                                out_batch_major=(k == "p" and l == depth - 1), **common)
            ys[k], cv = outs[0], outs[-1]
            if pool[k] is not None:
                pool_out[k].append(_batch_major(outs[1], gk["bm"]))
            conv_out[k].append(cv)

    y_prompt = ys["p"]
    y_sample = _batch_major(ys["s"], bs)
    return (y_prompt, y_sample,
            jnp.stack(pool_out["p"]), jnp.stack(pool_out["s"]),
            jnp.stack([a for a, _ in ssm_out["p"]]), jnp.stack([b for _, b in ssm_out["p"]]),
            jnp.stack([a for a, _ in ssm_out["s"]]), jnp.stack([b for _, b in ssm_out["s"]]),
            _stacked_batch_major(conv_out["p"], bp), _stacked_batch_major(conv_out["s"], bs))
```

```python
import functools
import math

import jax
import jax.numpy as jnp
from jax import lax
from jax.experimental import pallas as pl
from jax.experimental.pallas import tpu as pltpu

POOL_WINDOWS = (2, 4, 8, 16)
POOL_STATE = max(POOL_WINDOWS) - 1
SSM_GC = 16
SSM_P = 64
CONV_W = 3
EPS = 1e-6
PAST_LEN = 16384

V7X_SUBLANES = 8
V7X_LANES = 128
V7X_MXU_DIM = 256
VMEM_LIMIT_BYTES = 56 * 1024 * 1024

PROMPT_POOL_FFN_TT = 64
PROMPT_FFN_TT = 128
PROMPT_S5_TT = 128
ADA_TILE_N = 1536
CAST_TILE_ROWS = 512

_F32 = jnp.float32
_BF16 = jnp.bfloat16


def _resident(block_shape, index_map):
    return pl.BlockSpec(block_shape, index_map, pipeline_mode=pl.Buffered(1))


def _whole(shape):
    nd = len(shape)
    return _resident(shape, lambda i: (0,) * nd)


def _layer_slab(shape, l):
    nd = len(shape)
    return _resident((None,) + tuple(shape[1:]), lambda i: (l,) + (0,) * (nd - 1))


def _unit_rms(x):
    return x * lax.rsqrt(jnp.mean(x * x, axis=-1, keepdims=True) + EPS)


def _pre_mod(x, g, scale, shift, bm):
    t, d = x.shape
    r = _unit_rms(x).reshape(t // bm, bm, d)
    return (r * (g * (1.0 + scale))[None] + shift[None]).reshape(t, d)


def _gated_residual(x, m, g, gate, bm):
    t, d = x.shape
    r = _unit_rms(m).reshape(t // bm, bm, d)
    return x + (r * (gate * g)[None]).reshape(t, d)


def _gelu(x):
    return 0.5 * x * (1.0 + lax.erf(x * math.sqrt(0.5)))


def _dot(a, b):
    return jnp.dot(a, b, preferred_element_type=_F32)


def _load_rows(x_ref, batch_major):
    if not batch_major:
        return x_ref[...]
    b, tt, d = x_ref.shape
    return jnp.swapaxes(x_ref[...], 0, 1).reshape(tt * b, d)


def _store_rows(y_ref, y, batch_major):
    if not batch_major:
        y_ref[...] = y
    else:
        b, tt, d = y_ref.shape
        y_ref[...] = jnp.swapaxes(y.reshape(tt, b, d), 0, 1)


def _mod3(mod_ref, k, d):
    return tuple(mod_ref[:, (3 * k + n) * d:(3 * k + n + 1) * d] for n in range(3))


def _cast_kernel(w_ref, o_ref):
    o_ref[...] = w_ref[...].astype(o_ref.dtype)


def _to_bf16(w, rows):
    l, r, c = w.shape
    spec = pl.BlockSpec((None, rows, c), lambda i, j: (i, j, 0))
    return pl.pallas_call(
        _cast_kernel,
        grid=(l, r // rows),
        in_specs=[spec],
        out_specs=spec,
        out_shape=jax.ShapeDtypeStruct(w.shape, _BF16),
        compiler_params=pltpu.CompilerParams(
            dimension_semantics=("arbitrary", "arbitrary"), vmem_limit_bytes=VMEM_LIMIT_BYTES),
        name="to_bf16",
    )(w)


def _ada_kernel(cp_ref, cs_ref, w_ref, b_ref, mp_ref, ms_ref):
    bp = cp_ref.shape[0]
    c = jnp.concatenate([cp_ref[...], cs_ref[...]], axis=0)
    s = (c * jax.nn.sigmoid(c)).astype(_BF16)
    o = _dot(s, w_ref[...].astype(_BF16)) + b_ref[...]
    mp_ref[...] = o[:bp]
    ms_ref[...] = o[bp:]


def _ada_mod(c_prompt, c_sample, ada_w, ada_b):
    depth, d, n = ada_w.shape
    tn = ADA_TILE_N
    bp, bs = c_prompt.shape[0], c_sample.shape[0]
    return pl.pallas_call(
        _ada_kernel,
        grid=(depth, n // tn),
        in_specs=[
            pl.BlockSpec((bp, d), lambda l, j: (0, 0)),
            pl.BlockSpec((bs, d), lambda l, j: (0, 0)),
            pl.BlockSpec((None, d, tn), lambda l, j: (l, 0, j)),
            pl.BlockSpec((None, 1, tn), lambda l, j: (l, 0, j)),
        ],
        out_specs=[
            pl.BlockSpec((None, bp, tn), lambda l, j: (l, 0, j)),
            pl.BlockSpec((None, bs, tn), lambda l, j: (l, 0, j)),
        ],
        out_shape=[
            jax.ShapeDtypeStruct((depth, bp, n), _F32),
            jax.ShapeDtypeStruct((depth, bs, n), _F32),
        ],
        compiler_params=pltpu.CompilerParams(
            dimension_semantics=("arbitrary", "arbitrary"),
            vmem_limit_bytes=VMEM_LIMIT_BYTES),
        name="ada_mod",
    )(c_prompt, c_sample, ada_w, ada_b.reshape(depth, 1, n))


def _pool_mix(h, st, pw_ref, ps_ref, i, *, bm, pos_base):
    t, d = h.shape
    tt = t // bm
    gc = d // len(POOL_WINDOWS)
    ext = jnp.concatenate([st, h], axis=0)
    pos = None
    if pos_base + 1 < max(POOL_WINDOWS):
        row = lax.broadcasted_iota(jnp.int32, (t, V7X_LANES), 0)
        pos = pos_base + i * tt + lax.shift_right_logical(row, bm.bit_length() - 1)
    parts = []
    for gi, w in enumerate(POOL_WINDOWS):
        cols = slice(gi * gc, (gi + 1) * gc)
        s = ext[:, cols]
        span = 1
        while span < w:
            n = s.shape[0]
            s = s[span * bm:] + s[:n - span * bm]
            span *= 2
        k0 = (POOL_STATE - (w - 1)) * bm
        wsum = s[k0:k0 + t]
        if pos is None:
            inv = 1.0 / w
        else:
            inv = 1.0 / jnp.minimum(pos + 1, w).astype(_F32)
            inv = jnp.concatenate([inv] * (gc // V7X_LANES), axis=1)
        pooled = wsum * inv - h[:, cols]
        parts.append(_dot(pooled.astype(_BF16), pw_ref[gi].astype(_BF16)))
    return jnp.concatenate(parts, axis=1) * ps_ref[...], ext[t:, :]


def _conv_ffn(f, cv_ref, wup_ref, cw_ref, cb_ref, wdn_ref, h_ref, *, bm):
    t = f.shape[0]
    hid = wdn_ref.shape[0]
    hc = V7X_MXU_DIM

    def conv_cols(c0):
        cols = slice(c0, c0 + hc)
        up = _dot(f, wup_ref[:, cols])
        ext = jnp.concatenate([cv_ref[:, cols], up], axis=0)
        cv_ref[:, cols] = ext[t:, :]
        conv = cb_ref[:, cols]
        for k in range(CONV_W):
            conv = conv + ext[k * bm:k * bm + t, :] * cw_ref[k:k + 1, cols]
        return conv

    for c in range(hid // hc):
        gate_c = conv_cols(c * hc)
        val_c = conv_cols(hid + c * hc)
        h_ref[:, c * hc:(c + 1) * hc] = (_gelu(gate_c) * val_c).astype(_BF16)
    return _dot(h_ref[...], wdn_ref[...].astype(_BF16))


def _layer_kernel(*refs, layer, bm, pos_base, with_pool, with_ffn, in_batch_major, out_batch_major,
                  state_batch_major):
    refs = list(refs)
    x_ref, mod_ref = refs[:2]
    del refs[:2]
    if with_pool:
        st0_ref, mpre_ref, mpost_ref, pw_ref, ps_ref = refs[:5]
        del refs[:5]
    if with_ffn:
        cv0_ref, fpre_ref, fpost_ref, wup_ref, cw_ref, cb_ref, wdn_ref = refs[:7]
        del refs[:7]
    y_ref = refs.pop(0)
    if with_pool:
        st_ref = refs.pop(0)
    if with_ffn:
        cv_ref, h_ref = refs
    i = pl.program_id(0)
    d = mod_ref.shape[1] // 6
    lrow = slice(layer, layer + 1)

    @pl.when(i == 0)
    def _():
        if with_pool and not state_batch_major:
            st_ref[...] = st0_ref[...]
        if with_ffn:
            cv_ref[...] = cv0_ref[...]

    x = _load_rows(x_ref, in_batch_major)
    if with_pool:
        shift, scale, gate = _mod3(mod_ref, 0, d)
        h = _pre_mod(x, mpre_ref[lrow, :], scale, shift, bm)
        st = _load_rows(st0_ref, True) if state_batch_major else st_ref[...]
        m, st = _pool_mix(h, st, pw_ref, ps_ref, i, bm=bm, pos_base=pos_base)
        _store_rows(st_ref, st, state_batch_major)
        x = _gated_residual(x, m, mpost_ref[lrow, :], gate, bm)
    if with_ffn:
        shift, scale, gate = _mod3(mod_ref, 1, d)
        f = _pre_mod(x, fpre_ref[lrow, :], scale, shift, bm).astype(_BF16)
        o = _conv_ffn(f, cv_ref, wup_ref, cw_ref, cb_ref.at[lrow, :], wdn_ref, h_ref, bm=bm)
        x = _gated_residual(x, o, fpost_ref[lrow, :], gate, bm)
    _store_rows(y_ref, x, out_batch_major)


def _token_layer(x, mod, gains, *, layer, bm, tt, pos_base=0, pool=None, ffn=None,
                 in_batch_major=False, out_batch_major=False, state_batch_major=False):
    mix_pre_g, mix_post_g, ffn_pre_g, ffn_post_g = gains
    d = x.shape[-1]
    nrows = x.shape[0] * x.shape[1] if in_batch_major else x.shape[0]
    t = tt * bm
    assert not state_batch_major or nrows == t
    tm_spec = pl.BlockSpec((t, d), lambda i: (i, 0))
    bm_spec = pl.BlockSpec((bm, tt, d), lambda i: (0, i, 0))

    args = [x, mod]
    in_specs = [bm_spec if in_batch_major else tm_spec, _layer_slab(mod.shape, layer)]
    out_specs = [bm_spec if out_batch_major else tm_spec]
    out_shape = [jax.ShapeDtypeStruct((bm, nrows // bm, d) if out_batch_major else (nrows, d), _F32)]
    scratch = []
    if pool is not None:
        st0, st_slab, pool_w, pool_scale = pool
        st_shape = st0.shape if st_slab is None else st0.shape[1:]
        args += [st0, mix_pre_g, mix_post_g, pool_w, pool_scale]
        in_specs += [_whole(st0.shape) if st_slab is None else _layer_slab(st0.shape, st_slab)]
        in_specs += [_whole(a.shape) for a in args[-4:]]
        out_specs += [_whole(st_shape)]
        out_shape += [jax.ShapeDtypeStruct(st_shape, _F32)]
    if ffn is not None:
        cv0, w_up, conv_w, conv_b, w_down = ffn
        hid = w_down.shape[1]
        assert hid % V7X_MXU_DIM == 0 and cv0.shape[1] == 2 * hid
        args += [cv0, ffn_pre_g, ffn_post_g, w_up, conv_w, conv_b, w_down]
        in_specs += [_whole(cv0.shape), _whole(ffn_pre_g.shape), _whole(ffn_post_g.shape),
                     _layer_slab(w_up.shape, layer), _layer_slab(conv_w.shape, layer),
                     _whole(conv_b.shape), _layer_slab(w_down.shape, layer)]
        out_specs += [_whole(cv0.shape)]
        out_shape += [jax.ShapeDtypeStruct(cv0.shape, _F32)]
        scratch += [pltpu.VMEM((t, hid), _BF16)]
    return pl.pallas_call(
        functools.partial(_layer_kernel, layer=layer, bm=bm, pos_base=pos_base,
                          with_pool=pool is not None, with_ffn=ffn is not None,
                          in_batch_major=in_batch_major, out_batch_major=out_batch_major,
                          state_batch_major=state_batch_major),
        grid=(nrows // t,),
        in_specs=in_specs,
        out_specs=out_specs,
        out_shape=out_shape,
        scratch_shapes=scratch,
        compiler_params=pltpu.CompilerParams(
            dimension_semantics=("arbitrary",), vmem_limit_bytes=VMEM_LIMIT_BYTES),
        name="_".join(n for n, on in (("pool", pool), ("ffn", ffn)) if on is not None) + "_layer",
    )(*args)


def _s5_prep_kernel(are_ref, aim_ref, ldt_ref, btr_ref, bti_ref, ctr_ref, cti_ref,
                    lbr_ref, lbi_ref, wb_ref, wc_ref):
    a_re, a_im = are_ref[...], aim_ref[...]
    dt = jnp.exp(ldt_ref[...])
    mag = jnp.exp(a_re * dt)
    ang = a_im * dt
    lb_re = mag * jnp.cos(ang)
    lb_im = mag * jnp.sin(ang)
    n_re = lb_re - 1.0
    n_im = lb_im
    den = a_re * a_re + a_im * a_im
    f_re = ((n_re * a_re + n_im * a_im) / den)[:, None, :]
    f_im = ((n_im * a_re - n_re * a_im) / den)[:, None, :]
    lbr_ref[...] = lb_re
    lbi_ref[...] = lb_im
    b_re, b_im = btr_ref[...], bti_ref[...]
    bb_re = (f_re * b_re - f_im * b_im).astype(wb_ref.dtype)
    bb_im = (f_re * b_im + f_im * b_re).astype(wb_ref.dtype)
    c_re = ctr_ref[...].astype(wc_ref.dtype)
    nc_im = (-cti_ref[...]).astype(wc_ref.dtype)
    wb_ref[...] = jnp.zeros(wb_ref.shape, wb_ref.dtype)
    wc_ref[...] = jnp.zeros(wc_ref.shape, wc_ref.dtype)
    g, gc, p = b_re.shape
    gpt = wb_ref.shape[1] // gc
    half = gpt * p
    for gi in range(g):
        j, gl = divmod(gi, gpt)
        rows, cols = slice(gl * gc, (gl + 1) * gc), slice(gl * p, (gl + 1) * p)
        icols = slice(half + gl * p, half + (gl + 1) * p)
        wb_ref[j, rows, cols] = bb_re[gi]
        wb_ref[j, rows, icols] = bb_im[gi]
        wc_ref[j, cols, rows] = c_re[gi]
        wc_ref[j, icols, rows] = nc_im[gi]


def _s5_prep(a_re, a_im, log_dt, b_re, b_im, c_re, c_im):
    g, p = a_re.shape
    gc = b_re.shape[-1]
    gpt = V7X_MXU_DIM // gc
    tr = lambda a: jnp.swapaxes(a, 1, 2)
    return pl.pallas_call(
        _s5_prep_kernel,
        out_shape=[
            jax.ShapeDtypeStruct((g, p), _F32),
            jax.ShapeDtypeStruct((g, p), _F32),
            jax.ShapeDtypeStruct((g // gpt, gpt * gc, 2 * gpt * p), _BF16),
            jax.ShapeDtypeStruct((g // gpt, 2 * gpt * p, gpt * gc), _BF16),
        ],
        compiler_params=pltpu.CompilerParams(vmem_limit_bytes=VMEM_LIMIT_BYTES),
        name="s5_prep",
    )(a_re, a_im, log_dt.reshape(g, 1), tr(b_re), tr(b_im), tr(c_re), tr(c_im))


def _s5_kernel(x_ref, mod_ref, xr0_ref, xi0_ref, gpre_ref, gpost_ref, wb_ref, wc_ref,
               lbr_ref, lbi_ref, dsk_ref, ga_ref, gb_ref,
               y_ref, xr_ref, xi_ref, *, layer, bm):
    i = pl.program_id(0)
    t, d = x_ref.shape
    tt = t // bm
    ntile, kin, ncol2 = wb_ref.shape
    ncol = ncol2 // 2
    sub = V7X_SUBLANES
    lrow = slice(layer, layer + 1)

    @pl.when(i == 0)
    def _():
        xr_ref[...] = xr0_ref[...]
        xi_ref[...] = xi0_ref[...]

    x = x_ref[...]
    shift, scale, gate = _mod3(mod_ref, 0, d)
    h = _pre_mod(x, gpre_ref[lrow, :], scale, shift, bm)
    u = h.astype(_BF16)

    ys = []
    for j in range(ntile):
        scol = slice(j * ncol, (j + 1) * ncol)
        bu = _dot(u[:, j * kin:(j + 1) * kin], wb_ref[j])
        lr = jnp.broadcast_to(lbr_ref[:, scol], (sub, ncol))
        li = jnp.broadcast_to(lbi_ref[:, scol], (sub, ncol))
        blocks = [None] * (t // sub)
        for rb in range(bm // sub):
            rows = slice(rb * sub, (rb + 1) * sub)
            pr, pi = xr_ref[rows, scol], xi_ref[rows, scol]
            for ts in range(tt):
                r0 = ts * bm + rb * sub
                nr = lr * pr - li * pi + bu[r0:r0 + sub, :ncol]
                ni = lr * pi + li * pr + bu[r0:r0 + sub, ncol:]
                blocks[r0 // sub] = jnp.concatenate([nr, ni], axis=1)
                pr, pi = nr, ni
            xr_ref[rows, scol] = pr
            xi_ref[rows, scol] = pi
        ys.append(_dot(jnp.concatenate(blocks, axis=0).astype(_BF16), wc_ref[j]))
    y = jnp.concatenate(ys, axis=1) + dsk_ref[...] * h
    g = _gelu(y).astype(_BF16)
    out = _dot(g, ga_ref[...].astype(_BF16)) * jax.nn.sigmoid(_dot(g, gb_ref[...].astype(_BF16)))
    y_ref[...] = _gated_residual(x, out, gpost_ref[lrow, :], gate, bm)


def _s5_layer(x, mod, xr0, xi0, g_pre, g_post, wb, wc, lbr, lbi, dskip, glu_a, glu_b, *, layer, bm, tt):
    r, d = x.shape
    t = tt * bm
    nst = xr0.shape[1]
    return pl.pallas_call(
        functools.partial(_s5_kernel, layer=layer, bm=bm),
        grid=(r // t,),
        in_specs=[
            pl.BlockSpec((t, d), lambda i: (i, 0)),
            _layer_slab(mod.shape, layer),
            _whole((bm, nst)),
            _whole((bm, nst)),
            _whole(g_pre.shape),
            _whole(g_post.shape),
            _whole(wb.shape),
            _whole(wc.shape),
            _whole((1, nst)),
            _whole((1, nst)),
            _whole((1, d)),
            _whole(glu_a.shape),
            _whole(glu_b.shape),
        ],
        out_specs=[
            pl.BlockSpec((t, d), lambda i: (i, 0)),
            _whole((bm, nst)),
            _whole((bm, nst)),
        ],
        out_shape=[
            jax.ShapeDtypeStruct((r, d), _F32),
            jax.ShapeDtypeStruct((bm, nst), _F32),
            jax.ShapeDtypeStruct((bm, nst), _F32),
        ],
        compiler_params=pltpu.CompilerParams(
            dimension_semantics=("arbitrary",), vmem_limit_bytes=VMEM_LIMIT_BYTES),
        name="s5_layer",
    )(x, mod, xr0, xi0, g_pre, g_post, wb, wc, lbr, lbi, dskip, glu_a, glu_b)


def _time_major(a):
    b, l, c = a.shape
    return jnp.swapaxes(a, 0, 1).reshape(l * b, c)


def _batch_major(a, b):
    lb, c = a.shape
    return jnp.swapaxes(a.reshape(lb // b, b, c), 0, 1)


def _stacked_batch_major(states, b):
    st = jnp.stack(states)
    n, tb, c = st.shape
    return jnp.swapaxes(st.reshape(n, tb // b, b, c), 1, 2)


def kernel(x_prompt, x_sample, c_prompt, c_sample, state_pool, state_ssm_re, state_ssm_im, state_ffn_conv, ada_w, ada_b, mix_pre_g, mix_post_g, ffn_pre_g, ffn_post_g, pool_w, pool_scale, ssm_A_re, ssm_A_im, ssm_log_dt, ssm_B_re, ssm_B_im, ssm_C_re, ssm_C_im, ssm_D, ssm_glu_a, ssm_glu_b, ffn_w_up, ffn_conv_w, ffn_conv_b, ffn_w_down):
    depth = ada_w.shape[0]
    bp, lp, d = x_prompt.shape
    bs, ls, _ = x_sample.shape
    nup = ffn_w_up.shape[-1]
    groups = ssm_A_re.shape[1]
    nst = groups * SSM_P

    mod_p, mod_s = _ada_mod(c_prompt, c_sample, ada_w, ada_b)
    gains = (mix_pre_g, mix_post_g, ffn_pre_g, ffn_post_g)
    w_up = _to_bf16(ffn_w_up, CAST_TILE_ROWS)
    w_dn = ffn_w_down

    geo = {
        "p": dict(bm=bp, pool_ffn_tt=PROMPT_POOL_FFN_TT, ffn_tt=PROMPT_FFN_TT, s5_tt=PROMPT_S5_TT, pos_base=0,
                  fuse_pool=True),
        "s": dict(bm=bs, pool_ffn_tt=ls // 2, ffn_tt=ls // 2, s5_tt=ls // 2, pos_base=PAST_LEN, fuse_pool=False,
                  pool_tt=ls),
    }
    ys = {"p": x_prompt, "s": x_sample}
    mods = {"p": mod_p, "s": mod_s}
    pool_out = {"p": [], "s": []}
    ssm_out = {"p": [], "s": []}
    conv_out = {"p": [], "s": []}

    for l in range(depth):
        j = l // 2
        pool = {"p": None, "s": None}
        if l % 2 == 0:
            pw = pool_w[j]
            ps = pool_scale[j].reshape(1, d)
            pool["p"] = (jnp.zeros((POOL_STATE * bp, d), _F32), None, pw, ps)
            pool["s"] = (state_pool, j, pw, ps)
        else:
            lb_re, lb_im, wb, wc = _s5_prep(ssm_A_re[j], ssm_A_im[j], ssm_log_dt[j], ssm_B_re[j], ssm_B_im[j],
                                            ssm_C_re[j], ssm_C_im[j])
            lbr = lb_re.reshape(1, nst)
            lbi = lb_im.reshape(1, nst)
            ga = ssm_glu_a[j]
            gb = ssm_glu_b[j]
            for k in ("p", "s"):
                gk = geo[k]
                if k == "p":
                    xr0 = jnp.zeros((bp, nst), _F32)
                    xi0 = xr0
                else:
                    xr0 = state_ssm_re[j].reshape(bs, nst)
                    xi0 = state_ssm_im[j].reshape(bs, nst)
                ys[k], xr, xi = _s5_layer(ys[k], mods[k], xr0, xi0, mix_pre_g, mix_post_g,
                                          wb, wc, lbr, lbi, ssm_D[j].reshape(1, d), ga, gb,
                                          layer=l, bm=gk["bm"], tt=gk["s5_tt"])
                ssm_out[k].append((xr.reshape(gk["bm"], groups, SSM_P), xi.reshape(gk["bm"], groups, SSM_P)))

        for k in ("p", "s"):
            gk = geo[k]
            if k == "p":
                cv0 = jnp.zeros(((CONV_W - 1) * bp, nup), _F32)
            else:
                cv0 = _time_major(state_ffn_conv[l])
            ffn = (cv0, w_up, ffn_conv_w, ffn_conv_b, w_dn)
            common = dict(layer=l, bm=gk["bm"], pos_base=gk["pos_base"])
            if pool[k] is not None and not gk["fuse_pool"]:
                ys[k], st = _token_layer(ys[k], mods[k], gains, tt=gk["pool_tt"], pool=pool[k],
                                         in_batch_major=True, state_batch_major=True, **common)
                pool_out[k].append(st)
                pool[k] = None
            tt = gk["ffn_tt"] if pool[k] is None else gk["pool_ffn_tt"]
            outs = _token_layer(ys[k], mods[k], gains, tt=tt, pool=pool[k], ffn=ffn,
                                in_batch_major=(l == 0 and pool[k] is not None),
                                out_batch_major=(k == "p" and l == depth - 1), **common)
            ys[k], cv = outs[0], outs[-1]
            if pool[k] is not None:
                pool_out[k].append(_batch_major(outs[1], gk["bm"]))
            conv_out[k].append(cv)

    y_prompt = ys["p"]
    y_sample = _batch_major(ys["s"], bs)
    return (y_prompt, y_sample,
            jnp.stack(pool_out["p"]), jnp.stack(pool_out["s"]),
            jnp.stack([a for a, _ in ssm_out["p"]]), jnp.stack([b for _, b in ssm_out["p"]]),
            jnp.stack([a for a, _ in ssm_out["s"]]), jnp.stack([b for _, b in ssm_out["s"]]),
            _stacked_batch_major(conv_out["p"], bp), _stacked_batch_major(conv_out["s"], bs))
```

```python
import functools
import math

import jax
import jax.numpy as jnp
from jax import lax
from jax.experimental import pallas as pl
from jax.experimental.pallas import tpu as pltpu

POOL_WINDOWS = (2, 4, 8, 16)
POOL_STATE = max(POOL_WINDOWS) - 1
SSM_GC = 16
SSM_P = 64
CONV_W = 3
EPS = 1e-6
PAST_LEN = 16384

V7X_SUBLANES = 8
V7X_LANES = 128
V7X_MXU_DIM = 256
VMEM_LIMIT_BYTES = 56 * 1024 * 1024

PROMPT_POOL_FFN_TT = 64
PROMPT_FFN_TT = 128
PROMPT_S5_TT = 128
ADA_TILE_N = 1536
CAST_TILE_ROWS = 64

_F32 = jnp.float32
_BF16 = jnp.bfloat16


def _resident(block_shape, index_map):
    return pl.BlockSpec(block_shape, index_map, pipeline_mode=pl.Buffered(1))


def _whole(shape):
    nd = len(shape)
    return _resident(shape, lambda i: (0,) * nd)


def _layer_slab(shape, l):
    nd = len(shape)
    return _resident((None,) + tuple(shape[1:]), lambda i: (l,) + (0,) * (nd - 1))


def _unit_rms(x):
    return x * lax.rsqrt(jnp.mean(x * x, axis=-1, keepdims=True) + EPS)


def _pre_mod(x, g, scale, shift, bm):
    t, d = x.shape
    r = _unit_rms(x).reshape(t // bm, bm, d)
    return (r * (g * (1.0 + scale))[None] + shift[None]).reshape(t, d)


def _gated_residual(x, m, g, gate, bm):
    t, d = x.shape
    r = _unit_rms(m).reshape(t // bm, bm, d)
    return x + (r * (gate * g)[None]).reshape(t, d)


def _gelu(x):
    return 0.5 * x * (1.0 + lax.erf(x * math.sqrt(0.5)))


def _dot(a, b):
    return jnp.dot(a, b, preferred_element_type=_F32)


def _load_rows(x_ref, batch_major):
    if not batch_major:
        return x_ref[...]
    b, tt, d = x_ref.shape
    return jnp.swapaxes(x_ref[...], 0, 1).reshape(tt * b, d)


def _store_rows(y_ref, y, batch_major):
    if not batch_major:
        y_ref[...] = y
    else:
        b, tt, d = y_ref.shape
        y_ref[...] = jnp.swapaxes(y.reshape(tt, b, d), 0, 1)


def _mod3(mod_ref, k, d):
    return tuple(mod_ref[:, (3 * k + n) * d:(3 * k + n + 1) * d] for n in range(3))


def _ada_kernel(cp_ref, cs_ref, w_ref, b_ref, mp_ref, ms_ref):
    bp = cp_ref.shape[0]
    c = jnp.concatenate([cp_ref[...], cs_ref[...]], axis=0)
    s = (c * jax.nn.sigmoid(c)).astype(_BF16)
    o = _dot(s, w_ref[...].astype(_BF16)) + b_ref[...]
    mp_ref[...] = o[:bp]
    ms_ref[...] = o[bp:]


def _ada_mod(c_prompt, c_sample, ada_w, ada_b):
    depth, d, n = ada_w.shape
    tn = ADA_TILE_N
    bp, bs = c_prompt.shape[0], c_sample.shape[0]
    return pl.pallas_call(
        _ada_kernel,
        grid=(depth, n // tn),
        in_specs=[
            pl.BlockSpec((bp, d), lambda l, j: (0, 0)),
            pl.BlockSpec((bs, d), lambda l, j: (0, 0)),
            pl.BlockSpec((None, d, tn), lambda l, j: (l, 0, j)),
            pl.BlockSpec((None, 1, tn), lambda l, j: (l, 0, j)),
        ],
        out_specs=[
            pl.BlockSpec((None, bp, tn), lambda l, j: (l, 0, j)),
            pl.BlockSpec((None, bs, tn), lambda l, j: (l, 0, j)),
        ],
        out_shape=[
            jax.ShapeDtypeStruct((depth, bp, n), _F32),
            jax.ShapeDtypeStruct((depth, bs, n), _F32),
        ],
        compiler_params=pltpu.CompilerParams(
            dimension_semantics=("arbitrary", "arbitrary"),
            vmem_limit_bytes=VMEM_LIMIT_BYTES),
        name="ada_mod",
    )(c_prompt, c_sample, ada_w, ada_b.reshape(depth, 1, n))


def _pool_mix(h, st, pw_ref, ps_ref, i, *, bm, pos_base):
    t, d = h.shape
    tt = t // bm
    gc = d // len(POOL_WINDOWS)
    ext = jnp.concatenate([st, h], axis=0)
    pos = None
    if pos_base + 1 < max(POOL_WINDOWS):
        row = lax.broadcasted_iota(jnp.int32, (t, V7X_LANES), 0)
        pos = pos_base + i * tt + lax.shift_right_logical(row, bm.bit_length() - 1)
    parts = []
    for gi, w in enumerate(POOL_WINDOWS):
        cols = slice(gi * gc, (gi + 1) * gc)
        s = ext[:, cols]
        span = 1
        while span < w:
            n = s.shape[0]
            s = s[span * bm:] + s[:n - span * bm]
            span *= 2
        k0 = (POOL_STATE - (w - 1)) * bm
        wsum = s[k0:k0 + t]
        if pos is None:
            inv = 1.0 / w
        else:
            inv = 1.0 / jnp.minimum(pos + 1, w).astype(_F32)
            inv = jnp.concatenate([inv] * (gc // V7X_LANES), axis=1)
        pooled = wsum * inv - h[:, cols]
        parts.append(_dot(pooled.astype(_BF16), pw_ref[gi].astype(_BF16)))
    return jnp.concatenate(parts, axis=1) * ps_ref[...], ext[t:, :]


def _conv_ffn(f, cv_ref, wup_ref, cw_ref, cb_ref, wdn_ref, h_ref, *, bm):
    t = f.shape[0]
    hid = wdn_ref.shape[0]
    hc = V7X_MXU_DIM

    def conv_cols(c0):
        cols = slice(c0, c0 + hc)
        up = _dot(f, wup_ref[:, cols])
        ext = jnp.concatenate([cv_ref[:, cols], up], axis=0)
        cv_ref[:, cols] = ext[t:, :]
        conv = cb_ref[:, cols]
        for k in range(CONV_W):
            conv = conv + ext[k * bm:k * bm + t, :] * cw_ref[k:k + 1, cols]
        return conv

    for c in range(hid // hc):
        gate_c = conv_cols(c * hc)
        val_c = conv_cols(hid + c * hc)
        h_ref[:, c * hc:(c + 1) * hc] = (_gelu(gate_c) * val_c).astype(_BF16)
    return _dot(h_ref[...], wdn_ref[...].astype(_BF16))


def _w_up_copies(w_hbm, w_vm, w_out, stage_ref, sem_ref):
    rows = stage_ref.shape[1]
    slab = lambda c: pltpu.make_async_copy(w_hbm.at[pl.ds(c * rows, rows), :], stage_ref.at[c % 2], sem_ref.at[c % 2])
    return slab, pltpu.make_async_copy(w_vm, w_out, sem_ref.at[2])


def _stage_w_up(w_hbm, w_vm, w_out, stage_ref, sem_ref):
    i = pl.program_id(0)
    slab, out_copy = _w_up_copies(w_hbm, w_vm, w_out, stage_ref, sem_ref)
    rows = stage_ref.shape[1]
    n = w_vm.shape[0] // rows

    @pl.when(i == 0)
    def _():
        slab(0).start()
        for c in range(n):
            if c + 1 < n:
                slab(c + 1).start()
            slab(c).wait()
            w_vm[c * rows:(c + 1) * rows, :] = stage_ref[c % 2].astype(w_vm.dtype)
        out_copy.start()

    @pl.when(i == pl.num_programs(0) - 1)
    def _():
        out_copy.wait()


def _layer_kernel(*refs, layer, bm, pos_base, with_pool, with_ffn, in_batch_major, out_batch_major,
                  state_batch_major, cast_w_up):
    refs = list(refs)
    x_ref, mod_ref = refs[:2]
    del refs[:2]
    if with_pool:
        st0_ref, mpre_ref, mpost_ref, pw_ref, ps_ref = refs[:5]
        del refs[:5]
    if with_ffn:
        cv0_ref, fpre_ref, fpost_ref, wup_ref, cw_ref, cb_ref, wdn_ref = refs[:7]
        del refs[:7]
    y_ref = refs.pop(0)
    if with_pool:
        st_ref = refs.pop(0)
    if with_ffn:
        cv_ref = refs.pop(0)
        if cast_w_up:
            wup_out_ref, h_ref, wup_vm_ref, stage_ref, sem_ref = refs
            _stage_w_up(wup_ref.at[layer], wup_vm_ref, wup_out_ref, stage_ref, sem_ref)
            wup_ref = wup_vm_ref
        else:
            (h_ref,) = refs
    i = pl.program_id(0)
    d = mod_ref.shape[1] // 6
    lrow = slice(layer, layer + 1)

    @pl.when(i == 0)
    def _():
        if with_pool and not state_batch_major:
            st_ref[...] = st0_ref[...]
        if with_ffn:
            cv_ref[...] = cv0_ref[...]

    x = _load_rows(x_ref, in_batch_major)
    if with_pool:
        shift, scale, gate = _mod3(mod_ref, 0, d)
        h = _pre_mod(x, mpre_ref[lrow, :], scale, shift, bm)
        st = _load_rows(st0_ref, True) if state_batch_major else st_ref[...]
        m, st = _pool_mix(h, st, pw_ref, ps_ref, i, bm=bm, pos_base=pos_base)
        _store_rows(st_ref, st, state_batch_major)
        x = _gated_residual(x, m, mpost_ref[lrow, :], gate, bm)
    if with_ffn:
        shift, scale, gate = _mod3(mod_ref, 1, d)
        f = _pre_mod(x, fpre_ref[lrow, :], scale, shift, bm).astype(_BF16)
        o = _conv_ffn(f, cv_ref, wup_ref, cw_ref, cb_ref.at[lrow, :], wdn_ref, h_ref, bm=bm)
        x = _gated_residual(x, o, fpost_ref[lrow, :], gate, bm)
    _store_rows(y_ref, x, out_batch_major)


def _token_layer(x, mod, gains, *, layer, bm, tt, pos_base=0, pool=None, ffn=None,
                 in_batch_major=False, out_batch_major=False, state_batch_major=False):
    mix_pre_g, mix_post_g, ffn_pre_g, ffn_post_g = gains
    d = x.shape[-1]
    nrows = x.shape[0] * x.shape[1] if in_batch_major else x.shape[0]
    t = tt * bm
    assert not state_batch_major or nrows == t
    tm_spec = pl.BlockSpec((t, d), lambda i: (i, 0))
    bm_spec = pl.BlockSpec((bm, tt, d), lambda i: (0, i, 0))

    args = [x, mod]
    in_specs = [bm_spec if in_batch_major else tm_spec, _layer_slab(mod.shape, layer)]
    out_specs = [bm_spec if out_batch_major else tm_spec]
    out_shape = [jax.ShapeDtypeStruct((bm, nrows // bm, d) if out_batch_major else (nrows, d), _F32)]
    scratch = []
    if pool is not None:
        st0, st_slab, pool_w, pool_scale = pool
        st_shape = st0.shape if st_slab is None else st0.shape[1:]
        args += [st0, mix_pre_g, mix_post_g, pool_w, pool_scale]
        in_specs += [_whole(st0.shape) if st_slab is None else _layer_slab(st0.shape, st_slab)]
        in_specs += [_whole(a.shape) for a in args[-4:]]
        out_specs += [_whole(st_shape)]
        out_shape += [jax.ShapeDtypeStruct(st_shape, _F32)]
    cast_w_up = ffn is not None and ffn[1].dtype != _BF16
    if ffn is not None:
        cv0, w_up, conv_w, conv_b, w_down = ffn
        hid = w_down.shape[1]
        assert hid % V7X_MXU_DIM == 0 and cv0.shape[1] == 2 * hid
        args += [cv0, ffn_pre_g, ffn_post_g, w_up, conv_w, conv_b, w_down]
        in_specs += [_whole(cv0.shape), _whole(ffn_pre_g.shape), _whole(ffn_post_g.shape),
                     pl.BlockSpec(memory_space=pl.ANY) if cast_w_up else _whole(w_up.shape),
                     _layer_slab(conv_w.shape, layer), _whole(conv_b.shape), _layer_slab(w_down.shape, layer)]
        out_specs += [_whole(cv0.shape)]
        out_shape += [jax.ShapeDtypeStruct(cv0.shape, _F32)]
        if cast_w_up:
            k_up, n_up = w_up.shape[1:]
            out_specs += [pl.BlockSpec(memory_space=pl.ANY)]
            out_shape += [jax.ShapeDtypeStruct((k_up, n_up), _BF16)]
        scratch += [pltpu.VMEM((t, hid), _BF16)]
        if cast_w_up:
            scratch += [pltpu.VMEM((k_up, n_up), _BF16), pltpu.VMEM((2, CAST_TILE_ROWS, n_up), _F32),
                        pltpu.SemaphoreType.DMA((3,))]
    return pl.pallas_call(
        functools.partial(_layer_kernel, layer=layer, bm=bm, pos_base=pos_base,
                          with_pool=pool is not None, with_ffn=ffn is not None,
                          in_batch_major=in_batch_major, out_batch_major=out_batch_major,
                          state_batch_major=state_batch_major, cast_w_up=cast_w_up),
        grid=(nrows // t,),
        in_specs=in_specs,
        out_specs=out_specs,
        out_shape=out_shape,
        scratch_shapes=scratch,
        compiler_params=pltpu.CompilerParams(
            dimension_semantics=("arbitrary",), vmem_limit_bytes=VMEM_LIMIT_BYTES),
        name="_".join(n for n, on in (("pool", pool), ("ffn", ffn)) if on is not None) + "_layer",
    )(*args)


def _s5_prep_kernel(are_ref, aim_ref, ldt_ref, btr_ref, bti_ref, ctr_ref, cti_ref,
                    lbr_ref, lbi_ref, wb_ref, wc_ref):
    a_re, a_im = are_ref[...], aim_ref[...]
    dt = jnp.exp(ldt_ref[...])
    mag = jnp.exp(a_re * dt)
    ang = a_im * dt
    lb_re = mag * jnp.cos(ang)
    lb_im = mag * jnp.sin(ang)
    n_re = lb_re - 1.0
    n_im = lb_im
    den = a_re * a_re + a_im * a_im
    f_re = ((n_re * a_re + n_im * a_im) / den)[:, None, :]
    f_im = ((n_im * a_re - n_re * a_im) / den)[:, None, :]
    lbr_ref[...] = lb_re
    lbi_ref[...] = lb_im
    b_re, b_im = btr_ref[...], bti_ref[...]
    bb_re = (f_re * b_re - f_im * b_im).astype(wb_ref.dtype)
    bb_im = (f_re * b_im + f_im * b_re).astype(wb_ref.dtype)
    c_re = ctr_ref[...].astype(wc_ref.dtype)
    nc_im = (-cti_ref[...]).astype(wc_ref.dtype)
    wb_ref[...] = jnp.zeros(wb_ref.shape, wb_ref.dtype)
    wc_ref[...] = jnp.zeros(wc_ref.shape, wc_ref.dtype)
    g, gc, p = b_re.shape
    gpt = wb_ref.shape[1] // gc
    half = gpt * p
    for gi in range(g):
        j, gl = divmod(gi, gpt)
        rows, cols = slice(gl * gc, (gl + 1) * gc), slice(gl * p, (gl + 1) * p)
        icols = slice(half + gl * p, half + (gl + 1) * p)
        wb_ref[j, rows, cols] = bb_re[gi]
        wb_ref[j, rows, icols] = bb_im[gi]
        wc_ref[j, cols, rows] = c_re[gi]
        wc_ref[j, icols, rows] = nc_im[gi]


def _s5_prep(a_re, a_im, log_dt, b_re, b_im, c_re, c_im):
    g, p = a_re.shape
    gc = b_re.shape[-1]
    gpt = V7X_MXU_DIM // gc
    tr = lambda a: jnp.swapaxes(a, 1, 2)
    return pl.pallas_call(
        _s5_prep_kernel,
        out_shape=[
            jax.ShapeDtypeStruct((g, p), _F32),
            jax.ShapeDtypeStruct((g, p), _F32),
            jax.ShapeDtypeStruct((g // gpt, gpt * gc, 2 * gpt * p), _BF16),
            jax.ShapeDtypeStruct((g // gpt, 2 * gpt * p, gpt * gc), _BF16),
        ],
        compiler_params=pltpu.CompilerParams(vmem_limit_bytes=VMEM_LIMIT_BYTES),
        name="s5_prep",
    )(a_re, a_im, log_dt.reshape(g, 1), tr(b_re), tr(b_im), tr(c_re), tr(c_im))


def _s5_kernel(x_ref, mod_ref, xr0_ref, xi0_ref, gpre_ref, gpost_ref, wb_ref, wc_ref,
               lbr_ref, lbi_ref, dsk_ref, ga_ref, gb_ref,
               y_ref, xr_ref, xi_ref, *, layer, bm):
    i = pl.program_id(0)
    t, d = x_ref.shape
    tt = t // bm
    ntile, kin, ncol2 = wb_ref.shape
    ncol = ncol2 // 2
    sub = V7X_SUBLANES
    lrow = slice(layer, layer + 1)

    @pl.when(i == 0)
    def _():
        xr_ref[...] = xr0_ref[...]
        xi_ref[...] = xi0_ref[...]

    x = x_ref[...]
    shift, scale, gate = _mod3(mod_ref, 0, d)
    h = _pre_mod(x, gpre_ref[lrow, :], scale, shift, bm)
    u = h.astype(_BF16)

    ys = []
    for j in range(ntile):
        scol = slice(j * ncol, (j + 1) * ncol)
        bu = _dot(u[:, j * kin:(j + 1) * kin], wb_ref[j])
        lr = jnp.broadcast_to(lbr_ref[:, scol], (sub, ncol))
        li = jnp.broadcast_to(lbi_ref[:, scol], (sub, ncol))
        blocks = [None] * (t // sub)
        for rb in range(bm // sub):
            rows = slice(rb * sub, (rb + 1) * sub)
            pr, pi = xr_ref[rows, scol], xi_ref[rows, scol]
            for ts in range(tt):
                r0 = ts * bm + rb * sub
                nr = lr * pr - li * pi + bu[r0:r0 + sub, :ncol]
                ni = lr * pi + li * pr + bu[r0:r0 + sub, ncol:]
                blocks[r0 // sub] = jnp.concatenate([nr, ni], axis=1)
                pr, pi = nr, ni
            xr_ref[rows, scol] = pr
            xi_ref[rows, scol] = pi
        ys.append(_dot(jnp.concatenate(blocks, axis=0).astype(_BF16), wc_ref[j]))
    y = jnp.concatenate(ys, axis=1) + dsk_ref[...] * h
    g = _gelu(y).astype(_BF16)
    out = _dot(g, ga_ref[...].astype(_BF16)) * jax.nn.sigmoid(_dot(g, gb_ref[...].astype(_BF16)))
    y_ref[...] = _gated_residual(x, out, gpost_ref[lrow, :], gate, bm)


def _s5_layer(x, mod, xr0, xi0, g_pre, g_post, wb, wc, lbr, lbi, dskip, glu_a, glu_b, *, layer, bm, tt):
    r, d = x.shape
    t = tt * bm
    nst = xr0.shape[1]
    return pl.pallas_call(
        functools.partial(_s5_kernel, layer=layer, bm=bm),
        grid=(r // t,),
        in_specs=[
            pl.BlockSpec((t, d), lambda i: (i, 0)),
            _layer_slab(mod.shape, layer),
            _whole((bm, nst)),
            _whole((bm, nst)),
            _whole(g_pre.shape),
            _whole(g_post.shape),
            _whole(wb.shape),
            _whole(wc.shape),
            _whole((1, nst)),
            _whole((1, nst)),
            _whole((1, d)),
            _whole(glu_a.shape),
            _whole(glu_b.shape),
        ],
        out_specs=[
            pl.BlockSpec((t, d), lambda i: (i, 0)),
            _whole((bm, nst)),
            _whole((bm, nst)),
        ],
        out_shape=[
            jax.ShapeDtypeStruct((r, d), _F32),
            jax.ShapeDtypeStruct((bm, nst), _F32),
            jax.ShapeDtypeStruct((bm, nst), _F32),
        ],
        compiler_params=pltpu.CompilerParams(
            dimension_semantics=("arbitrary",), vmem_limit_bytes=VMEM_LIMIT_BYTES),
        name="s5_layer",
    )(x, mod, xr0, xi0, g_pre, g_post, wb, wc, lbr, lbi, dskip, glu_a, glu_b)


def _time_major(a):
    b, l, c = a.shape
    return jnp.swapaxes(a, 0, 1).reshape(l * b, c)


def _batch_major(a, b):
    lb, c = a.shape
    return jnp.swapaxes(a.reshape(lb // b, b, c), 0, 1)


def _stacked_batch_major(states, b):
    st = jnp.stack(states)
    n, tb, c = st.shape
    return jnp.swapaxes(st.reshape(n, tb // b, b, c), 1, 2)


def kernel(x_prompt, x_sample, c_prompt, c_sample, state_pool, state_ssm_re, state_ssm_im, state_ffn_conv, ada_w, ada_b, mix_pre_g, mix_post_g, ffn_pre_g, ffn_post_g, pool_w, pool_scale, ssm_A_re, ssm_A_im, ssm_log_dt, ssm_B_re, ssm_B_im, ssm_C_re, ssm_C_im, ssm_D, ssm_glu_a, ssm_glu_b, ffn_w_up, ffn_conv_w, ffn_conv_b, ffn_w_down):
    depth = ada_w.shape[0]
    bp, lp, d = x_prompt.shape
    bs, ls, _ = x_sample.shape
    nup = ffn_w_up.shape[-1]
    groups = ssm_A_re.shape[1]
    nst = groups * SSM_P

    mod_p, mod_s = _ada_mod(c_prompt, c_sample, ada_w, ada_b)
    gains = (mix_pre_g, mix_post_g, ffn_pre_g, ffn_post_g)
    w_dn = ffn_w_down

    geo = {
        "p": dict(bm=bp, pool_ffn_tt=PROMPT_POOL_FFN_TT, ffn_tt=PROMPT_FFN_TT, s5_tt=PROMPT_S5_TT, pos_base=0,
                  fuse_pool=True),
        "s": dict(bm=bs, pool_ffn_tt=ls // 2, ffn_tt=ls // 2, s5_tt=ls // 2, pos_base=PAST_LEN, fuse_pool=False,
                  pool_tt=ls),
    }
    ys = {"p": x_prompt, "s": x_sample}
    mods = {"p": mod_p, "s": mod_s}
    pool_out = {"p": [], "s": []}
    ssm_out = {"p": [], "s": []}
    conv_out = {"p": [], "s": []}

    for l in range(depth):
        j = l // 2
        pool = {"p": None, "s": None}
        if l % 2 == 0:
            pw = pool_w[j]
            ps = pool_scale[j].reshape(1, d)
            pool["p"] = (jnp.zeros((POOL_STATE * bp, d), _F32), None, pw, ps)
            pool["s"] = (state_pool, j, pw, ps)
        else:
            lb_re, lb_im, wb, wc = _s5_prep(ssm_A_re[j], ssm_A_im[j], ssm_log_dt[j], ssm_B_re[j], ssm_B_im[j],
                                            ssm_C_re[j], ssm_C_im[j])
            lbr = lb_re.reshape(1, nst)
            lbi = lb_im.reshape(1, nst)
            ga = ssm_glu_a[j]
            gb = ssm_glu_b[j]
            for k in ("p", "s"):
                gk = geo[k]
                if k == "p":
                    xr0 = jnp.zeros((bp, nst), _F32)
                    xi0 = xr0
                else:
                    xr0 = state_ssm_re[j].reshape(bs, nst)
                    xi0 = state_ssm_im[j].reshape(bs, nst)
                ys[k], xr, xi = _s5_layer(ys[k], mods[k], xr0, xi0, mix_pre_g, mix_post_g,
                                          wb, wc, lbr, lbi, ssm_D[j].reshape(1, d), ga, gb,
                                          layer=l, bm=gk["bm"], tt=gk["s5_tt"])
                ssm_out[k].append((xr.reshape(gk["bm"], groups, SSM_P), xi.reshape(gk["bm"], groups, SSM_P)))

        for k in ("p", "s"):
            gk = geo[k]
            if k == "p":
                cv0 = jnp.zeros(((CONV_W - 1) * bp, nup), _F32)
            else:
                cv0 = _time_major(state_ffn_conv[l])
            ffn = (cv0, ffn_w_up if k == "p" else w_up_bf16, ffn_conv_w, ffn_conv_b, w_dn)
            common = dict(layer=l, bm=gk["bm"], pos_base=gk["pos_base"])
            if pool[k] is not None and not gk["fuse_pool"]:
                ys[k], st = _token_layer(ys[k], mods[k], gains, tt=gk["pool_tt"], pool=pool[k],
                                         in_batch_major=True, state_batch_major=True, **common)
                pool_out[k].append(st)
                pool[k] = None
            tt = gk["ffn_tt"] if pool[k] is None else gk["pool_ffn_tt"]
            outs = list(_token_layer(ys[k], mods[k], gains, tt=tt, pool=pool[k], ffn=ffn,
                                in_batch_major=(l == 0 and pool[k] is not None),
                                out_batch_major=(k == "p" and l == depth - 1), **common))
            if k == "p":
                w_up_bf16 = outs.pop()
            ys[k], cv = outs[0], outs[-1]
            if pool[k] is not None:
                pool_out[k].append(_batch_major(outs[1], gk["bm"]))
            conv_out[k].append(cv)

    y_prompt = ys["p"]
    y_sample = _batch_major(ys["s"], bs)
    return (y_prompt, y_sample,
            jnp.stack(pool_out["p"]), jnp.stack(pool_out["s"]),
            jnp.stack([a for a, _ in ssm_out["p"]]), jnp.stack([b for _, b in ssm_out["p"]]),
            jnp.stack([a for a, _ in ssm_out["s"]]), jnp.stack([b for _, b in ssm_out["s"]]),
            _stacked_batch_major(conv_out["p"], bp), _stacked_batch_major(conv_out["s"], bs))
```

```python
import functools
import math

import jax
import jax.numpy as jnp
from jax import lax
from jax.experimental import pallas as pl
from jax.experimental.pallas import tpu as pltpu

POOL_WINDOWS = (2, 4, 8, 16)
POOL_STATE = max(POOL_WINDOWS) - 1
SSM_GC = 16
SSM_P = 64
CONV_W = 3
EPS = 1e-6
PAST_LEN = 16384

V7X_SUBLANES = 8
V7X_LANES = 128
V7X_MXU_DIM = 256
VMEM_LIMIT_BYTES = 56 * 1024 * 1024

PROMPT_POOL_FFN_TT = 64
PROMPT_FFN_TT = 128
PROMPT_S5_TT = 128
ADA_TILE_N = 1536
CAST_TILE_ROWS = 256

_F32 = jnp.float32
_BF16 = jnp.bfloat16


def _resident(block_shape, index_map):
    return pl.BlockSpec(block_shape, index_map, pipeline_mode=pl.Buffered(1))


def _whole(shape):
    nd = len(shape)
    return _resident(shape, lambda i: (0,) * nd)


def _layer_slab(shape, l):
    nd = len(shape)
    return _resident((None,) + tuple(shape[1:]), lambda i: (l,) + (0,) * (nd - 1))


def _unit_rms(x):
    return x * lax.rsqrt(jnp.mean(x * x, axis=-1, keepdims=True) + EPS)


def _pre_mod(x, g, scale, shift, bm):
    t, d = x.shape
    r = _unit_rms(x).reshape(t // bm, bm, d)
    return (r * (g * (1.0 + scale))[None] + shift[None]).reshape(t, d)


def _gated_residual(x, m, g, gate, bm):
    t, d = x.shape
    r = _unit_rms(m).reshape(t // bm, bm, d)
    return x + (r * (gate * g)[None]).reshape(t, d)


def _gelu(x):
    return 0.5 * x * (1.0 + lax.erf(x * math.sqrt(0.5)))


def _dot(a, b):
    return jnp.dot(a, b, preferred_element_type=_F32)


def _load_rows(x_ref, batch_major):
    if not batch_major:
        return x_ref[...]
    b, tt, d = x_ref.shape
    return jnp.swapaxes(x_ref[...], 0, 1).reshape(tt * b, d)


def _store_rows(y_ref, y, batch_major):
    if not batch_major:
        y_ref[...] = y
    else:
        b, tt, d = y_ref.shape
        y_ref[...] = jnp.swapaxes(y.reshape(tt, b, d), 0, 1)


def _mod3(mod_ref, k, d):
    return tuple(mod_ref[:, (3 * k + n) * d:(3 * k + n + 1) * d] for n in range(3))


def _cast_kernel(w_ref, o_ref):
    o_ref[...] = w_ref[...].astype(o_ref.dtype)


def _slab_to_bf16(w, slab, rows):
    _, r, c = w.shape
    return pl.pallas_call(
        _cast_kernel,
        grid=(r // rows,),
        in_specs=[pl.BlockSpec((None, rows, c), lambda i: (slab, i, 0))],
        out_specs=pl.BlockSpec((rows, c), lambda i: (i, 0)),
        out_shape=jax.ShapeDtypeStruct((r, c), _BF16),
        compiler_params=pltpu.CompilerParams(
            dimension_semantics=("arbitrary",), vmem_limit_bytes=VMEM_LIMIT_BYTES),
        name="to_bf16",
    )(w)


def _ada_kernel(cp_ref, cs_ref, w_ref, b_ref, mp_ref, ms_ref):
    bp = cp_ref.shape[0]
    c = jnp.concatenate([cp_ref[...], cs_ref[...]], axis=0)
    s = (c * jax.nn.sigmoid(c)).astype(_BF16)
    o = _dot(s, w_ref[...].astype(_BF16)) + b_ref[...]
    mp_ref[...] = o[:bp]
    ms_ref[...] = o[bp:]


def _ada_mod(c_prompt, c_sample, ada_w, ada_b):
    depth, d, n = ada_w.shape
    tn = ADA_TILE_N
    bp, bs = c_prompt.shape[0], c_sample.shape[0]
    return pl.pallas_call(
        _ada_kernel,
        grid=(depth, n // tn),
        in_specs=[
            pl.BlockSpec((bp, d), lambda l, j: (0, 0)),
            pl.BlockSpec((bs, d), lambda l, j: (0, 0)),
            pl.BlockSpec((None, d, tn), lambda l, j: (l, 0, j)),
            pl.BlockSpec((None, 1, tn), lambda l, j: (l, 0, j)),
        ],
        out_specs=[
            pl.BlockSpec((None, bp, tn), lambda l, j: (l, 0, j)),
            pl.BlockSpec((None, bs, tn), lambda l, j: (l, 0, j)),
        ],
        out_shape=[
            jax.ShapeDtypeStruct((depth, bp, n), _F32),
            jax.ShapeDtypeStruct((depth, bs, n), _F32),
        ],
        compiler_params=pltpu.CompilerParams(
            dimension_semantics=("arbitrary", "arbitrary"),
            vmem_limit_bytes=VMEM_LIMIT_BYTES),
        name="ada_mod",
    )(c_prompt, c_sample, ada_w, ada_b.reshape(depth, 1, n))


def _pool_mix(h, st, pw_ref, ps_ref, i, *, bm, pos_base):
    t, d = h.shape
    tt = t // bm
    gc = d // len(POOL_WINDOWS)
    ext = jnp.concatenate([st, h], axis=0)
    pos = None
    if pos_base + 1 < max(POOL_WINDOWS):
        row = lax.broadcasted_iota(jnp.int32, (t, V7X_LANES), 0)
        pos = pos_base + i * tt + lax.shift_right_logical(row, bm.bit_length() - 1)
    parts = []
    for gi, w in enumerate(POOL_WINDOWS):
        cols = slice(gi * gc, (gi + 1) * gc)
        s = ext[:, cols]
        span = 1
        while span < w:
            n = s.shape[0]
            s = s[span * bm:] + s[:n - span * bm]
            span *= 2
        k0 = (POOL_STATE - (w - 1)) * bm
        wsum = s[k0:k0 + t]
        if pos is None:
            inv = 1.0 / w
        else:
            inv = 1.0 / jnp.minimum(pos + 1, w).astype(_F32)
            inv = jnp.concatenate([inv] * (gc // V7X_LANES), axis=1)
        pooled = wsum * inv - h[:, cols]
        parts.append(_dot(pooled.astype(_BF16), pw_ref[gi].astype(_BF16)))
    return jnp.concatenate(parts, axis=1) * ps_ref[...], ext[t:, :]


def _conv_ffn(f, cv_ref, wup_ref, cw_ref, cb_ref, wdn_ref, h_ref, *, bm):
    t = f.shape[0]
    hid = wdn_ref.shape[0]
    hc = V7X_MXU_DIM

    def conv_cols(c0):
        cols = slice(c0, c0 + hc)
        up = _dot(f, wup_ref[:, cols])
        ext = jnp.concatenate([cv_ref[:, cols], up], axis=0)
        cv_ref[:, cols] = ext[t:, :]
        conv = cb_ref[:, cols]
        for k in range(CONV_W):
            conv = conv + ext[k * bm:k * bm + t, :] * cw_ref[k:k + 1, cols]
        return conv

    for c in range(hid // hc):
        gate_c = conv_cols(c * hc)
        val_c = conv_cols(hid + c * hc)
        h_ref[:, c * hc:(c + 1) * hc] = (_gelu(gate_c) * val_c).astype(_BF16)
    return _dot(h_ref[...], wdn_ref[...].astype(_BF16))


def _layer_kernel(*refs, layer, bm, pos_base, with_pool, with_ffn, in_batch_major, out_batch_major,
                  state_batch_major, cast_next):
    refs = list(refs)
    x_ref, mod_ref = refs[:2]
    del refs[:2]
    if with_pool:
        st0_ref, mpre_ref, mpost_ref, pw_ref, ps_ref = refs[:5]
        del refs[:5]
    if with_ffn:
        cv0_ref, fpre_ref, fpost_ref, wup_ref, cw_ref, cb_ref, wdn_ref = refs[:7]
        del refs[:7]
        if cast_next:
            wnext_ref = refs.pop(0)
    y_ref = refs.pop(0)
    if with_pool:
        st_ref = refs.pop(0)
    if with_ffn:
        cv_ref = refs.pop(0)
        if cast_next:
            wnext_out_ref = refs.pop(0)
            wnext_out_ref[...] = wnext_ref[...].astype(wnext_out_ref.dtype)
        (h_ref,) = refs
    i = pl.program_id(0)
    d = mod_ref.shape[1] // 6
    lrow = slice(layer, layer + 1)

    @pl.when(i == 0)
    def _():
        if with_pool and not state_batch_major:
            st_ref[...] = st0_ref[...]
        if with_ffn:
            cv_ref[...] = cv0_ref[...]

    x = _load_rows(x_ref, in_batch_major)
    if with_pool:
        shift, scale, gate = _mod3(mod_ref, 0, d)
        h = _pre_mod(x, mpre_ref[lrow, :], scale, shift, bm)
        st = _load_rows(st0_ref, True) if state_batch_major else st_ref[...]
        m, st = _pool_mix(h, st, pw_ref, ps_ref, i, bm=bm, pos_base=pos_base)
        _store_rows(st_ref, st, state_batch_major)
        x = _gated_residual(x, m, mpost_ref[lrow, :], gate, bm)
    if with_ffn:
        shift, scale, gate = _mod3(mod_ref, 1, d)
        f = _pre_mod(x, fpre_ref[lrow, :], scale, shift, bm).astype(_BF16)
        o = _conv_ffn(f, cv_ref, wup_ref, cw_ref, cb_ref.at[lrow, :], wdn_ref, h_ref, bm=bm)
        x = _gated_residual(x, o, fpost_ref[lrow, :], gate, bm)
    _store_rows(y_ref, x, out_batch_major)


def _token_layer(x, mod, gains, *, layer, bm, tt, pos_base=0, pool=None, ffn=None, cast_next=None,
                 in_batch_major=False, out_batch_major=False, state_batch_major=False):
    mix_pre_g, mix_post_g, ffn_pre_g, ffn_post_g = gains
    d = x.shape[-1]
    nrows = x.shape[0] * x.shape[1] if in_batch_major else x.shape[0]
    t = tt * bm
    assert not state_batch_major or nrows == t
    tm_spec = pl.BlockSpec((t, d), lambda i: (i, 0))
    bm_spec = pl.BlockSpec((bm, tt, d), lambda i: (0, i, 0))

    args = [x, mod]
    in_specs = [bm_spec if in_batch_major else tm_spec, _layer_slab(mod.shape, layer)]
    out_specs = [bm_spec if out_batch_major else tm_spec]
    out_shape = [jax.ShapeDtypeStruct((bm, nrows // bm, d) if out_batch_major else (nrows, d), _F32)]
    scratch = []
    if pool is not None:
        st0, st_slab, pool_w, pool_scale = pool
        st_shape = st0.shape if st_slab is None else st0.shape[1:]
        args += [st0, mix_pre_g, mix_post_g, pool_w, pool_scale]
        in_specs += [_whole(st0.shape) if st_slab is None else _layer_slab(st0.shape, st_slab)]
        in_specs += [_whole(a.shape) for a in args[-4:]]
        out_specs += [_whole(st_shape)]
        out_shape += [jax.ShapeDtypeStruct(st_shape, _F32)]
    if ffn is not None:
        cv0, w_up, conv_w, conv_b, w_down = ffn
        hid = w_down.shape[1]
        assert hid % V7X_MXU_DIM == 0 and cv0.shape[1] == 2 * hid
        args += [cv0, ffn_pre_g, ffn_post_g, w_up, conv_w, conv_b, w_down]
        in_specs += [_whole(cv0.shape), _whole(ffn_pre_g.shape), _whole(ffn_post_g.shape), _whole(w_up.shape),
                     _layer_slab(conv_w.shape, layer), _whole(conv_b.shape), _layer_slab(w_down.shape, layer)]
        out_specs += [_whole(cv0.shape)]
        out_shape += [jax.ShapeDtypeStruct(cv0.shape, _F32)]
        if cast_next is not None:
            w_stack, slab = cast_next
            _, r_w, c_w = w_stack.shape
            rows = r_w // (nrows // t)
            assert rows * (nrows // t) == r_w and rows % (2 * V7X_SUBLANES) == 0
            args += [w_stack]
            in_specs += [pl.BlockSpec((None, rows, c_w), lambda i: (slab, i, 0))]
            out_specs += [pl.BlockSpec((rows, c_w), lambda i: (i, 0))]
            out_shape += [jax.ShapeDtypeStruct((r_w, c_w), _BF16)]
        scratch += [pltpu.VMEM((t, hid), _BF16)]
    return pl.pallas_call(
        functools.partial(_layer_kernel, layer=layer, bm=bm, pos_base=pos_base,
                          with_pool=pool is not None, with_ffn=ffn is not None,
                          in_batch_major=in_batch_major, out_batch_major=out_batch_major,
                          state_batch_major=state_batch_major, cast_next=cast_next is not None),
        grid=(nrows // t,),
        in_specs=in_specs,
        out_specs=out_specs,
        out_shape=out_shape,
        scratch_shapes=scratch,
        compiler_params=pltpu.CompilerParams(
            dimension_semantics=("arbitrary",), vmem_limit_bytes=VMEM_LIMIT_BYTES),
        name="_".join(n for n, on in (("pool", pool), ("ffn", ffn)) if on is not None) + "_layer",
    )(*args)


def _s5_prep_kernel(are_ref, aim_ref, ldt_ref, btr_ref, bti_ref, ctr_ref, cti_ref,
                    lbr_ref, lbi_ref, wb_ref, wc_ref):
    a_re, a_im = are_ref[...], aim_ref[...]
    dt = jnp.exp(ldt_ref[...])
    mag = jnp.exp(a_re * dt)
    ang = a_im * dt
    lb_re = mag * jnp.cos(ang)
    lb_im = mag * jnp.sin(ang)
    n_re = lb_re - 1.0
    n_im = lb_im
    den = a_re * a_re + a_im * a_im
    f_re = ((n_re * a_re + n_im * a_im) / den)[:, None, :]
    f_im = ((n_im * a_re - n_re * a_im) / den)[:, None, :]
    lbr_ref[...] = lb_re
    lbi_ref[...] = lb_im
    b_re, b_im = btr_ref[...], bti_ref[...]
    bb_re = (f_re * b_re - f_im * b_im).astype(wb_ref.dtype)
    bb_im = (f_re * b_im + f_im * b_re).astype(wb_ref.dtype)
    c_re = ctr_ref[...].astype(wc_ref.dtype)
    nc_im = (-cti_ref[...]).astype(wc_ref.dtype)
    wb_ref[...] = jnp.zeros(wb_ref.shape, wb_ref.dtype)
    wc_ref[...] = jnp.zeros(wc_ref.shape, wc_ref.dtype)
    g, gc, p = b_re.shape
    gpt = wb_ref.shape[1] // gc
    half = gpt * p
    for gi in range(g):
        j, gl = divmod(gi, gpt)
        rows, cols = slice(gl * gc, (gl + 1) * gc), slice(gl * p, (gl + 1) * p)
        icols = slice(half + gl * p, half + (gl + 1) * p)
        wb_ref[j, rows, cols] = bb_re[gi]
        wb_ref[j, rows, icols] = bb_im[gi]
        wc_ref[j, cols, rows] = c_re[gi]
        wc_ref[j, icols, rows] = nc_im[gi]


def _s5_prep(a_re, a_im, log_dt, b_re, b_im, c_re, c_im):
    g, p = a_re.shape
    gc = b_re.shape[-1]
    gpt = V7X_MXU_DIM // gc
    tr = lambda a: jnp.swapaxes(a, 1, 2)
    return pl.pallas_call(
        _s5_prep_kernel,
        out_shape=[
            jax.ShapeDtypeStruct((g, p), _F32),
            jax.ShapeDtypeStruct((g, p), _F32),
            jax.ShapeDtypeStruct((g // gpt, gpt * gc, 2 * gpt * p), _BF16),
            jax.ShapeDtypeStruct((g // gpt, 2 * gpt * p, gpt * gc), _BF16),
        ],
        compiler_params=pltpu.CompilerParams(vmem_limit_bytes=VMEM_LIMIT_BYTES),
        name="s5_prep",
    )(a_re, a_im, log_dt.reshape(g, 1), tr(b_re), tr(b_im), tr(c_re), tr(c_im))


def _s5_kernel(x_ref, mod_ref, xr0_ref, xi0_ref, gpre_ref, gpost_ref, wb_ref, wc_ref,
               lbr_ref, lbi_ref, dsk_ref, ga_ref, gb_ref,
               y_ref, xr_ref, xi_ref, *, layer, bm):
    i = pl.program_id(0)
    t, d = x_ref.shape
    tt = t // bm
    ntile, kin, ncol2 = wb_ref.shape
    ncol = ncol2 // 2
    sub = V7X_SUBLANES
    lrow = slice(layer, layer + 1)

    @pl.when(i == 0)
    def _():
        xr_ref[...] = xr0_ref[...]
        xi_ref[...] = xi0_ref[...]

    x = x_ref[...]
    shift, scale, gate = _mod3(mod_ref, 0, d)
    h = _pre_mod(x, gpre_ref[lrow, :], scale, shift, bm)
    u = h.astype(_BF16)

    ys = []
    for j in range(ntile):
        scol = slice(j * ncol, (j + 1) * ncol)
        bu = _dot(u[:, j * kin:(j + 1) * kin], wb_ref[j])
        lr = jnp.broadcast_to(lbr_ref[:, scol], (sub, ncol))
        li = jnp.broadcast_to(lbi_ref[:, scol], (sub, ncol))
        blocks = [None] * (t // sub)
        for rb in range(bm // sub):
            rows = slice(rb * sub, (rb + 1) * sub)
            pr, pi = xr_ref[rows, scol], xi_ref[rows, scol]
            for ts in range(tt):
                r0 = ts * bm + rb * sub
                nr = lr * pr - li * pi + bu[r0:r0 + sub, :ncol]
                ni = lr * pi + li * pr + bu[r0:r0 + sub, ncol:]
                blocks[r0 // sub] = jnp.concatenate([nr, ni], axis=1)
                pr, pi = nr, ni
            xr_ref[rows, scol] = pr
            xi_ref[rows, scol] = pi
        ys.append(_dot(jnp.concatenate(blocks, axis=0).astype(_BF16), wc_ref[j]))
    y = jnp.concatenate(ys, axis=1) + dsk_ref[...] * h
    g = _gelu(y).astype(_BF16)
    out = _dot(g, ga_ref[...].astype(_BF16)) * jax.nn.sigmoid(_dot(g, gb_ref[...].astype(_BF16)))
    y_ref[...] = _gated_residual(x, out, gpost_ref[lrow, :], gate, bm)


def _s5_layer(x, mod, xr0, xi0, g_pre, g_post, wb, wc, lbr, lbi, dskip, glu_a, glu_b, *, layer, bm, tt):
    r, d = x.shape
    t = tt * bm
    nst = xr0.shape[1]
    return pl.pallas_call(
        functools.partial(_s5_kernel, layer=layer, bm=bm),
        grid=(r // t,),
        in_specs=[
            pl.BlockSpec((t, d), lambda i: (i, 0)),
            _layer_slab(mod.shape, layer),
            _whole((bm, nst)),
            _whole((bm, nst)),
            _whole(g_pre.shape),
            _whole(g_post.shape),
            _whole(wb.shape),
            _whole(wc.shape),
            _whole((1, nst)),
            _whole((1, nst)),
            _whole((1, d)),
            _whole(glu_a.shape),
            _whole(glu_b.shape),
        ],
        out_specs=[
            pl.BlockSpec((t, d), lambda i: (i, 0)),
            _whole((bm, nst)),
            _whole((bm, nst)),
        ],
        out_shape=[
            jax.ShapeDtypeStruct((r, d), _F32),
            jax.ShapeDtypeStruct((bm, nst), _F32),
            jax.ShapeDtypeStruct((bm, nst), _F32),
        ],
        compiler_params=pltpu.CompilerParams(
            dimension_semantics=("arbitrary",), vmem_limit_bytes=VMEM_LIMIT_BYTES),
        name="s5_layer",
    )(x, mod, xr0, xi0, g_pre, g_post, wb, wc, lbr, lbi, dskip, glu_a, glu_b)


def _time_major(a):
    b, l, c = a.shape
    return jnp.swapaxes(a, 0, 1).reshape(l * b, c)


def _batch_major(a, b):
    lb, c = a.shape
    return jnp.swapaxes(a.reshape(lb // b, b, c), 0, 1)


def _stacked_batch_major(states, b):
    st = jnp.stack(states)
    n, tb, c = st.shape
    return jnp.swapaxes(st.reshape(n, tb // b, b, c), 1, 2)


def kernel(x_prompt, x_sample, c_prompt, c_sample, state_pool, state_ssm_re, state_ssm_im, state_ffn_conv, ada_w, ada_b, mix_pre_g, mix_post_g, ffn_pre_g, ffn_post_g, pool_w, pool_scale, ssm_A_re, ssm_A_im, ssm_log_dt, ssm_B_re, ssm_B_im, ssm_C_re, ssm_C_im, ssm_D, ssm_glu_a, ssm_glu_b, ffn_w_up, ffn_conv_w, ffn_conv_b, ffn_w_down):
    depth = ada_w.shape[0]
    bp, lp, d = x_prompt.shape
    bs, ls, _ = x_sample.shape
    nup = ffn_w_up.shape[-1]
    groups = ssm_A_re.shape[1]
    nst = groups * SSM_P

    mod_p, mod_s = _ada_mod(c_prompt, c_sample, ada_w, ada_b)
    gains = (mix_pre_g, mix_post_g, ffn_pre_g, ffn_post_g)
    w_dn = ffn_w_down
    w_up_next = _slab_to_bf16(ffn_w_up, 0, CAST_TILE_ROWS)

    geo = {
        "p": dict(bm=bp, pool_ffn_tt=PROMPT_POOL_FFN_TT, ffn_tt=PROMPT_FFN_TT, s5_tt=PROMPT_S5_TT, pos_base=0,
                  fuse_pool=True),
        "s": dict(bm=bs, pool_ffn_tt=ls // 2, ffn_tt=ls // 2, s5_tt=ls // 2, pos_base=PAST_LEN, fuse_pool=False,
                  pool_tt=ls),
    }
    ys = {"p": x_prompt, "s": x_sample}
    mods = {"p": mod_p, "s": mod_s}
    pool_out = {"p": [], "s": []}
    ssm_out = {"p": [], "s": []}
    conv_out = {"p": [], "s": []}

    for l in range(depth):
        j = l // 2
        w_up = w_up_next
        pool = {"p": None, "s": None}
        if l % 2 == 0:
            pw = pool_w[j]
            ps = pool_scale[j].reshape(1, d)
            pool["p"] = (jnp.zeros((POOL_STATE * bp, d), _F32), None, pw, ps)
            pool["s"] = (state_pool, j, pw, ps)
        else:
            lb_re, lb_im, wb, wc = _s5_prep(ssm_A_re[j], ssm_A_im[j], ssm_log_dt[j], ssm_B_re[j], ssm_B_im[j],
                                            ssm_C_re[j], ssm_C_im[j])
            lbr = lb_re.reshape(1, nst)
            lbi = lb_im.reshape(1, nst)
            ga = ssm_glu_a[j]
            gb = ssm_glu_b[j]
            for k in ("p", "s"):
                gk = geo[k]
                if k == "p":
                    xr0 = jnp.zeros((bp, nst), _F32)
                    xi0 = xr0
                else:
                    xr0 = state_ssm_re[j].reshape(bs, nst)
                    xi0 = state_ssm_im[j].reshape(bs, nst)
                ys[k], xr, xi = _s5_layer(ys[k], mods[k], xr0, xi0, mix_pre_g, mix_post_g,
                                          wb, wc, lbr, lbi, ssm_D[j].reshape(1, d), ga, gb,
                                          layer=l, bm=gk["bm"], tt=gk["s5_tt"])
                ssm_out[k].append((xr.reshape(gk["bm"], groups, SSM_P), xi.reshape(gk["bm"], groups, SSM_P)))

        for k in ("p", "s"):
            gk = geo[k]
            if k == "p":
                cv0 = jnp.zeros(((CONV_W - 1) * bp, nup), _F32)
            else:
                cv0 = _time_major(state_ffn_conv[l])
            ffn = (cv0, w_up, ffn_conv_w, ffn_conv_b, w_dn)
            cast_next = (ffn_w_up, l + 1) if k == "p" and l + 1 < depth else None
            common = dict(layer=l, bm=gk["bm"], pos_base=gk["pos_base"])
            if pool[k] is not None and not gk["fuse_pool"]:
                ys[k], st = _token_layer(ys[k], mods[k], gains, tt=gk["pool_tt"], pool=pool[k],
                                         in_batch_major=True, state_batch_major=True, **common)
                pool_out[k].append(st)
                pool[k] = None
            tt = gk["ffn_tt"] if pool[k] is None else gk["pool_ffn_tt"]
            outs = list(_token_layer(ys[k], mods[k], gains, tt=tt, pool=pool[k], ffn=ffn, cast_next=cast_next,
                                in_batch_major=(l == 0 and pool[k] is not None),
                                out_batch_major=(k == "p" and l == depth - 1), **common))
            if cast_next is not None:
                w_up_next = outs.pop()
            ys[k], cv = outs[0], outs[-1]
            if pool[k] is not None:
                pool_out[k].append(_batch_major(outs[1], gk["bm"]))
            conv_out[k].append(cv)

    y_prompt = ys["p"]
    y_sample = _batch_major(ys["s"], bs)
    return (y_prompt, y_sample,
            jnp.stack(pool_out["p"]), jnp.stack(pool_out["s"]),
            jnp.stack([a for a, _ in ssm_out["p"]]), jnp.stack([b for _, b in ssm_out["p"]]),
            jnp.stack([a for a, _ in ssm_out["s"]]), jnp.stack([b for _, b in ssm_out["s"]]),
            _stacked_batch_major(conv_out["p"], bp), _stacked_batch_major(conv_out["s"], bs))
```

```python
import functools
import math

import jax
import jax.numpy as jnp
from jax import lax
from jax.experimental import pallas as pl
from jax.experimental.pallas import tpu as pltpu

POOL_WINDOWS = (2, 4, 8, 16)
POOL_STATE = max(POOL_WINDOWS) - 1
SSM_GC = 16
SSM_P = 64
CONV_W = 3
EPS = 1e-6
PAST_LEN = 16384

V7X_SUBLANES = 8
V7X_LANES = 128
V7X_MXU_DIM = 256
VMEM_LIMIT_BYTES = 56 * 1024 * 1024

PROMPT_POOL_FFN_TT = 64
PROMPT_FFN_TT = 128
PROMPT_S5_TT = 128
ADA_TILE_N = 1536

_F32 = jnp.float32
_BF16 = jnp.bfloat16


def _resident(block_shape, index_map):
    return pl.BlockSpec(block_shape, index_map, pipeline_mode=pl.Buffered(1))


def _whole(shape):
    nd = len(shape)
    return _resident(shape, lambda i: (0,) * nd)


def _layer_slab(shape, l):
    nd = len(shape)
    return _resident((None,) + tuple(shape[1:]), lambda i: (l,) + (0,) * (nd - 1))


def _unit_rms(x):
    return x * lax.rsqrt(jnp.mean(x * x, axis=-1, keepdims=True) + EPS)


def _pre_mod(x, g, scale, shift, bm):
    t, d = x.shape
    r = _unit_rms(x).reshape(t // bm, bm, d)
    return (r * (g * (1.0 + scale))[None] + shift[None]).reshape(t, d)


def _gated_residual(x, m, g, gate, bm):
    t, d = x.shape
    r = _unit_rms(m).reshape(t // bm, bm, d)
    return x + (r * (gate * g)[None]).reshape(t, d)


def _gelu(x):
    return 0.5 * x * (1.0 + lax.erf(x * math.sqrt(0.5)))


def _dot(a, b):
    return jnp.dot(a, b, preferred_element_type=_F32)


def _load_rows(x_ref, batch_major):
    if not batch_major:
        return x_ref[...]
    b, tt, d = x_ref.shape
    return jnp.swapaxes(x_ref[...], 0, 1).reshape(tt * b, d)


def _store_rows(y_ref, y, batch_major):
    if not batch_major:
        y_ref[...] = y
    else:
        b, tt, d = y_ref.shape
        y_ref[...] = jnp.swapaxes(y.reshape(tt, b, d), 0, 1)


def _mod3(mod_ref, k, d):
    return tuple(mod_ref[:, (3 * k + n) * d:(3 * k + n + 1) * d] for n in range(3))


def _ada_kernel(cp_ref, cs_ref, w_ref, b_ref, mp_ref, ms_ref):
    bp = cp_ref.shape[0]
    c = jnp.concatenate([cp_ref[...], cs_ref[...]], axis=0)
    s = (c * jax.nn.sigmoid(c)).astype(_BF16)
    o = _dot(s, w_ref[...].astype(_BF16)) + b_ref[...]
    mp_ref[...] = o[:bp]
    ms_ref[...] = o[bp:]


def _ada_mod(c_prompt, c_sample, ada_w, ada_b):
    depth, d, n = ada_w.shape
    tn = ADA_TILE_N
    bp, bs = c_prompt.shape[0], c_sample.shape[0]
    return pl.pallas_call(
        _ada_kernel,
        grid=(depth, n // tn),
        in_specs=[
            pl.BlockSpec((bp, d), lambda l, j: (0, 0)),
            pl.BlockSpec((bs, d), lambda l, j: (0, 0)),
            pl.BlockSpec((None, d, tn), lambda l, j: (l, 0, j)),
            pl.BlockSpec((None, 1, tn), lambda l, j: (l, 0, j)),
        ],
        out_specs=[
            pl.BlockSpec((None, bp, tn), lambda l, j: (l, 0, j)),
            pl.BlockSpec((None, bs, tn), lambda l, j: (l, 0, j)),
        ],
        out_shape=[
            jax.ShapeDtypeStruct((depth, bp, n), _F32),
            jax.ShapeDtypeStruct((depth, bs, n), _F32),
        ],
        compiler_params=pltpu.CompilerParams(
            dimension_semantics=("arbitrary", "arbitrary"),
            vmem_limit_bytes=VMEM_LIMIT_BYTES),
        name="ada_mod",
    )(c_prompt, c_sample, ada_w, ada_b.reshape(depth, 1, n))


def _pool_mix(h, st, pw_ref, ps_ref, i, *, bm, pos_base):
    t, d = h.shape
    tt = t // bm
    gc = d // len(POOL_WINDOWS)
    ext = jnp.concatenate([st, h], axis=0)
    pos = None
    if pos_base + 1 < max(POOL_WINDOWS):
        row = lax.broadcasted_iota(jnp.int32, (t, V7X_LANES), 0)
        pos = pos_base + i * tt + lax.shift_right_logical(row, bm.bit_length() - 1)
    parts = []
    for gi, w in enumerate(POOL_WINDOWS):
        cols = slice(gi * gc, (gi + 1) * gc)
        s = ext[:, cols]
        span = 1
        while span < w:
            n = s.shape[0]
            s = s[span * bm:] + s[:n - span * bm]
            span *= 2
        k0 = (POOL_STATE - (w - 1)) * bm
        wsum = s[k0:k0 + t]
        if pos is None:
            inv = 1.0 / w
        else:
            inv = 1.0 / jnp.minimum(pos + 1, w).astype(_F32)
            inv = jnp.concatenate([inv] * (gc // V7X_LANES), axis=1)
        pooled = wsum * inv - h[:, cols]
        parts.append(_dot(pooled.astype(_BF16), pw_ref[gi].astype(_BF16)))
    return jnp.concatenate(parts, axis=1) * ps_ref[...], ext[t:, :]


def _conv_ffn(f, cv_ref, wup_ref, cw_ref, cb_ref, wdn_ref, h_ref, *, bm):
    t = f.shape[0]
    hid = wdn_ref.shape[0]
    hc = V7X_MXU_DIM

    def conv_cols(c0):
        cols = slice(c0, c0 + hc)
        up = _dot(f, wup_ref[:, cols].astype(_BF16))
        ext = jnp.concatenate([cv_ref[:, cols], up], axis=0)
        cv_ref[:, cols] = ext[t:, :]
        conv = cb_ref[:, cols]
        for k in range(CONV_W):
            conv = conv + ext[k * bm:k * bm + t, :] * cw_ref[k:k + 1, cols]
        return conv

    for c in range(hid // hc):
        gate_c = conv_cols(c * hc)
        val_c = conv_cols(hid + c * hc)
        h_ref[:, c * hc:(c + 1) * hc] = (_gelu(gate_c) * val_c).astype(_BF16)
    return _dot(h_ref[...], wdn_ref[...].astype(_BF16))


def _layer_kernel(*refs, layer, bm, pos_base, with_pool, with_ffn, in_batch_major, out_batch_major,
                  state_batch_major, cast_next, cast_self):
    refs = list(refs)
    x_ref, mod_ref = refs[:2]
    del refs[:2]
    if with_pool:
        st0_ref, mpre_ref, mpost_ref, pw_ref, ps_ref = refs[:5]
        del refs[:5]
    if with_ffn:
        cv0_ref, fpre_ref, fpost_ref, wup_ref, cw_ref, cb_ref, wdn_ref = refs[:7]
        del refs[:7]
        if cast_next:
            wnext_ref = refs.pop(0)
    y_ref = refs.pop(0)
    if with_pool:
        st_ref = refs.pop(0)
    if with_ffn:
        cv_ref = refs.pop(0)
        if cast_next:
            wnext_out_ref = refs.pop(0)
            wnext_out_ref[...] = wnext_ref[...].astype(wnext_out_ref.dtype)
        if cast_self:
            wself_out_ref = refs.pop(0)
            rows = wself_out_ref.shape[0]
            r0 = pl.multiple_of(pl.program_id(0) * rows, rows)
            wself_out_ref[...] = wup_ref[pl.ds(r0, rows), :].astype(wself_out_ref.dtype)
        (h_ref,) = refs
    i = pl.program_id(0)
    d = mod_ref.shape[1] // 6
    lrow = slice(layer, layer + 1)

    @pl.when(i == 0)
    def _():
        if with_pool and not state_batch_major:
            st_ref[...] = st0_ref[...]
        if with_ffn:
            cv_ref[...] = cv0_ref[...]

    x = _load_rows(x_ref, in_batch_major)
    if with_pool:
        shift, scale, gate = _mod3(mod_ref, 0, d)
        h = _pre_mod(x, mpre_ref[lrow, :], scale, shift, bm)
        st = _load_rows(st0_ref, True) if state_batch_major else st_ref[...]
        m, st = _pool_mix(h, st, pw_ref, ps_ref, i, bm=bm, pos_base=pos_base)
        _store_rows(st_ref, st, state_batch_major)
        x = _gated_residual(x, m, mpost_ref[lrow, :], gate, bm)
    if with_ffn:
        shift, scale, gate = _mod3(mod_ref, 1, d)
        f = _pre_mod(x, fpre_ref[lrow, :], scale, shift, bm).astype(_BF16)
        o = _conv_ffn(f, cv_ref, wup_ref, cw_ref, cb_ref.at[lrow, :], wdn_ref, h_ref, bm=bm)
        x = _gated_residual(x, o, fpost_ref[lrow, :], gate, bm)
    _store_rows(y_ref, x, out_batch_major)


def _token_layer(x, mod, gains, *, layer, bm, tt, pos_base=0, pool=None, ffn=None, cast_next=None,
                 in_batch_major=False, out_batch_major=False, state_batch_major=False):
    mix_pre_g, mix_post_g, ffn_pre_g, ffn_post_g = gains
    d = x.shape[-1]
    nrows = x.shape[0] * x.shape[1] if in_batch_major else x.shape[0]
    t = tt * bm
    assert not state_batch_major or nrows == t
    tm_spec = pl.BlockSpec((t, d), lambda i: (i, 0))
    bm_spec = pl.BlockSpec((bm, tt, d), lambda i: (0, i, 0))

    args = [x, mod]
    in_specs = [bm_spec if in_batch_major else tm_spec, _layer_slab(mod.shape, layer)]
    out_specs = [bm_spec if out_batch_major else tm_spec]
    out_shape = [jax.ShapeDtypeStruct((bm, nrows // bm, d) if out_batch_major else (nrows, d), _F32)]
    scratch = []
    cast_self = False
    if pool is not None:
        st0, st_slab, pool_w, pool_scale = pool
        st_shape = st0.shape if st_slab is None else st0.shape[1:]
        args += [st0, mix_pre_g, mix_post_g, pool_w, pool_scale]
        in_specs += [_whole(st0.shape) if st_slab is None else _layer_slab(st0.shape, st_slab)]
        in_specs += [_whole(a.shape) for a in args[-4:]]
        out_specs += [_whole(st_shape)]
        out_shape += [jax.ShapeDtypeStruct(st_shape, _F32)]
    if ffn is not None:
        cv0, w_up, conv_w, conv_b, w_down = ffn
        hid = w_down.shape[1]
        assert hid % V7X_MXU_DIM == 0 and cv0.shape[1] == 2 * hid
        args += [cv0, ffn_pre_g, ffn_post_g, w_up, conv_w, conv_b, w_down]
        cast_self = w_up.ndim == 3
        in_specs += [_whole(cv0.shape), _whole(ffn_pre_g.shape), _whole(ffn_post_g.shape),
                     _layer_slab(w_up.shape, layer) if cast_self else _whole(w_up.shape),
                     _layer_slab(conv_w.shape, layer), _whole(conv_b.shape), _layer_slab(w_down.shape, layer)]
        out_specs += [_whole(cv0.shape)]
        out_shape += [jax.ShapeDtypeStruct(cv0.shape, _F32)]
        if cast_next is not None:
            w_stack, slab = cast_next
            _, r_w, c_w = w_stack.shape
            rows = r_w // (nrows // t)
            assert rows * (nrows // t) == r_w and rows % (2 * V7X_SUBLANES) == 0
            args += [w_stack]
            in_specs += [pl.BlockSpec((None, rows, c_w), lambda i: (slab, i, 0))]
            out_specs += [pl.BlockSpec((rows, c_w), lambda i: (i, 0))]
            out_shape += [jax.ShapeDtypeStruct((r_w, c_w), _BF16)]
        if cast_self:
            _, r_w, c_w = w_up.shape
            rows = r_w // (nrows // t)
            assert rows * (nrows // t) == r_w and rows % (2 * V7X_SUBLANES) == 0
            out_specs += [pl.BlockSpec((rows, c_w), lambda i: (i, 0))]
            out_shape += [jax.ShapeDtypeStruct((r_w, c_w), _BF16)]
        scratch += [pltpu.VMEM((t, hid), _BF16)]
    return pl.pallas_call(
        functools.partial(_layer_kernel, layer=layer, bm=bm, pos_base=pos_base,
                          with_pool=pool is not None, with_ffn=ffn is not None,
                          in_batch_major=in_batch_major, out_batch_major=out_batch_major,
                          state_batch_major=state_batch_major, cast_next=cast_next is not None,
                          cast_self=cast_self),
        grid=(nrows // t,),
        in_specs=in_specs,
        out_specs=out_specs,
        out_shape=out_shape,
        scratch_shapes=scratch,
        compiler_params=pltpu.CompilerParams(
            dimension_semantics=("arbitrary",), vmem_limit_bytes=VMEM_LIMIT_BYTES),
        name="_".join(n for n, on in (("pool", pool), ("ffn", ffn)) if on is not None) + "_layer",
    )(*args)


def _s5_prep_kernel(are_ref, aim_ref, ldt_ref, btr_ref, bti_ref, ctr_ref, cti_ref,
                    lbr_ref, lbi_ref, wb_ref, wc_ref):
    a_re, a_im = are_ref[...], aim_ref[...]
    dt = jnp.exp(ldt_ref[...])
    mag = jnp.exp(a_re * dt)
    ang = a_im * dt
    lb_re = mag * jnp.cos(ang)
    lb_im = mag * jnp.sin(ang)
    n_re = lb_re - 1.0
    n_im = lb_im
    den = a_re * a_re + a_im * a_im
    f_re = ((n_re * a_re + n_im * a_im) / den)[:, None, :]
    f_im = ((n_im * a_re - n_re * a_im) / den)[:, None, :]
    lbr_ref[...] = lb_re
    lbi_ref[...] = lb_im
    b_re, b_im = btr_ref[...], bti_ref[...]
    bb_re = (f_re * b_re - f_im * b_im).astype(wb_ref.dtype)
    bb_im = (f_re * b_im + f_im * b_re).astype(wb_ref.dtype)
    c_re = ctr_ref[...].astype(wc_ref.dtype)
    nc_im = (-cti_ref[...]).astype(wc_ref.dtype)
    wb_ref[...] = jnp.zeros(wb_ref.shape, wb_ref.dtype)
    wc_ref[...] = jnp.zeros(wc_ref.shape, wc_ref.dtype)
    g, gc, p = b_re.shape
    gpt = wb_ref.shape[1] // gc
    half = gpt * p
    for gi in range(g):
        j, gl = divmod(gi, gpt)
        rows, cols = slice(gl * gc, (gl + 1) * gc), slice(gl * p, (gl + 1) * p)
        icols = slice(half + gl * p, half + (gl + 1) * p)
        wb_ref[j, rows, cols] = bb_re[gi]
        wb_ref[j, rows, icols] = bb_im[gi]
        wc_ref[j, cols, rows] = c_re[gi]
        wc_ref[j, icols, rows] = nc_im[gi]


def _s5_prep(a_re, a_im, log_dt, b_re, b_im, c_re, c_im):
    g, p = a_re.shape
    gc = b_re.shape[-1]
    gpt = V7X_MXU_DIM // gc
    tr = lambda a: jnp.swapaxes(a, 1, 2)
    return pl.pallas_call(
        _s5_prep_kernel,
        out_shape=[
            jax.ShapeDtypeStruct((g, p), _F32),
            jax.ShapeDtypeStruct((g, p), _F32),
            jax.ShapeDtypeStruct((g // gpt, gpt * gc, 2 * gpt * p), _BF16),
            jax.ShapeDtypeStruct((g // gpt, 2 * gpt * p, gpt * gc), _BF16),
        ],
        compiler_params=pltpu.CompilerParams(vmem_limit_bytes=VMEM_LIMIT_BYTES),
        name="s5_prep",
    )(a_re, a_im, log_dt.reshape(g, 1), tr(b_re), tr(b_im), tr(c_re), tr(c_im))


def _s5_kernel(x_ref, mod_ref, xr0_ref, xi0_ref, gpre_ref, gpost_ref, wb_ref, wc_ref,
               lbr_ref, lbi_ref, dsk_ref, ga_ref, gb_ref,
               y_ref, xr_ref, xi_ref, *, layer, bm):
    i = pl.program_id(0)
    t, d = x_ref.shape
    tt = t // bm
    ntile, kin, ncol2 = wb_ref.shape
    ncol = ncol2 // 2
    sub = V7X_SUBLANES
    lrow = slice(layer, layer + 1)

    @pl.when(i == 0)
    def _():
        xr_ref[...] = xr0_ref[...]
        xi_ref[...] = xi0_ref[...]

    x = x_ref[...]
    shift, scale, gate = _mod3(mod_ref, 0, d)
    h = _pre_mod(x, gpre_ref[lrow, :], scale, shift, bm)
    u = h.astype(_BF16)

    ys = []
    for j in range(ntile):
        scol = slice(j * ncol, (j + 1) * ncol)
        bu = _dot(u[:, j * kin:(j + 1) * kin], wb_ref[j])
        lr = jnp.broadcast_to(lbr_ref[:, scol], (sub, ncol))
        li = jnp.broadcast_to(lbi_ref[:, scol], (sub, ncol))
        blocks = [None] * (t // sub)
        for rb in range(bm // sub):
            rows = slice(rb * sub, (rb + 1) * sub)
            pr, pi = xr_ref[rows, scol], xi_ref[rows, scol]
            for ts in range(tt):
                r0 = ts * bm + rb * sub
                nr = lr * pr - li * pi + bu[r0:r0 + sub, :ncol]
                ni = lr * pi + li * pr + bu[r0:r0 + sub, ncol:]
                blocks[r0 // sub] = jnp.concatenate([nr, ni], axis=1)
                pr, pi = nr, ni
            xr_ref[rows, scol] = pr
            xi_ref[rows, scol] = pi
        ys.append(_dot(jnp.concatenate(blocks, axis=0).astype(_BF16), wc_ref[j]))
    y = jnp.concatenate(ys, axis=1) + dsk_ref[...] * h
    g = _gelu(y).astype(_BF16)
    out = _dot(g, ga_ref[...].astype(_BF16)) * jax.nn.sigmoid(_dot(g, gb_ref[...].astype(_BF16)))
    y_ref[...] = _gated_residual(x, out, gpost_ref[lrow, :], gate, bm)


def _s5_layer(x, mod, xr0, xi0, g_pre, g_post, wb, wc, lbr, lbi, dskip, glu_a, glu_b, *, layer, bm, tt):
    r, d = x.shape
    t = tt * bm
    nst = xr0.shape[1]
    return pl.pallas_call(
        functools.partial(_s5_kernel, layer=layer, bm=bm),
        grid=(r // t,),
        in_specs=[
            pl.BlockSpec((t, d), lambda i: (i, 0)),
            _layer_slab(mod.shape, layer),
            _whole((bm, nst)),
            _whole((bm, nst)),
            _whole(g_pre.shape),
            _whole(g_post.shape),
            _whole(wb.shape),
            _whole(wc.shape),
            _whole((1, nst)),
            _whole((1, nst)),
            _whole((1, d)),
            _whole(glu_a.shape),
            _whole(glu_b.shape),
        ],
        out_specs=[
            pl.BlockSpec((t, d), lambda i: (i, 0)),
            _whole((bm, nst)),
            _whole((bm, nst)),
        ],
        out_shape=[
            jax.ShapeDtypeStruct((r, d), _F32),
            jax.ShapeDtypeStruct((bm, nst), _F32),
            jax.ShapeDtypeStruct((bm, nst), _F32),
        ],
        compiler_params=pltpu.CompilerParams(
            dimension_semantics=("arbitrary",), vmem_limit_bytes=VMEM_LIMIT_BYTES),
        name="s5_layer",
    )(x, mod, xr0, xi0, g_pre, g_post, wb, wc, lbr, lbi, dskip, glu_a, glu_b)


def _time_major(a):
    b, l, c = a.shape
    return jnp.swapaxes(a, 0, 1).reshape(l * b, c)


def _batch_major(a, b):
    lb, c = a.shape
    return jnp.swapaxes(a.reshape(lb // b, b, c), 0, 1)


def _stacked_batch_major(states, b):
    st = jnp.stack(states)
    n, tb, c = st.shape
    return jnp.swapaxes(st.reshape(n, tb // b, b, c), 1, 2)


def kernel(x_prompt, x_sample, c_prompt, c_sample, state_pool, state_ssm_re, state_ssm_im, state_ffn_conv, ada_w, ada_b, mix_pre_g, mix_post_g, ffn_pre_g, ffn_post_g, pool_w, pool_scale, ssm_A_re, ssm_A_im, ssm_log_dt, ssm_B_re, ssm_B_im, ssm_C_re, ssm_C_im, ssm_D, ssm_glu_a, ssm_glu_b, ffn_w_up, ffn_conv_w, ffn_conv_b, ffn_w_down):
    depth = ada_w.shape[0]
    bp, lp, d = x_prompt.shape
    bs, ls, _ = x_sample.shape
    nup = ffn_w_up.shape[-1]
    groups = ssm_A_re.shape[1]
    nst = groups * SSM_P

    mod_p, mod_s = _ada_mod(c_prompt, c_sample, ada_w, ada_b)
    gains = (mix_pre_g, mix_post_g, ffn_pre_g, ffn_post_g)
    w_dn = ffn_w_down
    w_up_next = ffn_w_up

    geo = {
        "p": dict(bm=bp, pool_ffn_tt=PROMPT_POOL_FFN_TT, ffn_tt=PROMPT_FFN_TT, s5_tt=PROMPT_S5_TT, pos_base=0,
                  fuse_pool=True),
        "s": dict(bm=bs, pool_ffn_tt=ls // 2, ffn_tt=ls // 2, s5_tt=ls // 2, pos_base=PAST_LEN, fuse_pool=False,
                  pool_tt=ls),
    }
    ys = {"p": x_prompt, "s": x_sample}
    mods = {"p": mod_p, "s": mod_s}
    pool_out = {"p": [], "s": []}
    ssm_out = {"p": [], "s": []}
    conv_out = {"p": [], "s": []}

    for l in range(depth):
        j = l // 2
        w_up = w_up_next
        pool = {"p": None, "s": None}
        if l % 2 == 0:
            pw = pool_w[j]
            ps = pool_scale[j].reshape(1, d)
            pool["p"] = (jnp.zeros((POOL_STATE * bp, d), _F32), None, pw, ps)
            pool["s"] = (state_pool, j, pw, ps)
        else:
            lb_re, lb_im, wb, wc = _s5_prep(ssm_A_re[j], ssm_A_im[j], ssm_log_dt[j], ssm_B_re[j], ssm_B_im[j],
                                            ssm_C_re[j], ssm_C_im[j])
            lbr = lb_re.reshape(1, nst)
            lbi = lb_im.reshape(1, nst)
            ga = ssm_glu_a[j]
            gb = ssm_glu_b[j]
            for k in ("p", "s"):
                gk = geo[k]
                if k == "p":
                    xr0 = jnp.zeros((bp, nst), _F32)
                    xi0 = xr0
                else:
                    xr0 = state_ssm_re[j].reshape(bs, nst)
                    xi0 = state_ssm_im[j].reshape(bs, nst)
                ys[k], xr, xi = _s5_layer(ys[k], mods[k], xr0, xi0, mix_pre_g, mix_post_g,
                                          wb, wc, lbr, lbi, ssm_D[j].reshape(1, d), ga, gb,
                                          layer=l, bm=gk["bm"], tt=gk["s5_tt"])
                ssm_out[k].append((xr.reshape(gk["bm"], groups, SSM_P), xi.reshape(gk["bm"], groups, SSM_P)))

        for k in ("p", "s"):
            gk = geo[k]
            if k == "p":
                cv0 = jnp.zeros(((CONV_W - 1) * bp, nup), _F32)
            else:
                cv0 = _time_major(state_ffn_conv[l])
            ffn = (cv0, w_up, ffn_conv_w, ffn_conv_b, w_dn)
            cast_next = (ffn_w_up, l + 1) if k == "p" and l + 1 < depth else None
            common = dict(layer=l, bm=gk["bm"], pos_base=gk["pos_base"])
            if pool[k] is not None and not gk["fuse_pool"]:
                ys[k], st = _token_layer(ys[k], mods[k], gains, tt=gk["pool_tt"], pool=pool[k],
                                         in_batch_major=True, state_batch_major=True, **common)
                pool_out[k].append(st)
                pool[k] = None
            tt = gk["ffn_tt"] if pool[k] is None else gk["pool_ffn_tt"]
            outs = list(_token_layer(ys[k], mods[k], gains, tt=tt, pool=pool[k], ffn=ffn, cast_next=cast_next,
                                in_batch_major=(l == 0 and pool[k] is not None),
                                out_batch_major=(k == "p" and l == depth - 1), **common))
            if w_up.ndim == 3:
                w_up = outs.pop()
            if cast_next is not None:
                w_up_next = outs.pop()
            ys[k], cv = outs[0], outs[-1]
            if pool[k] is not None:
                pool_out[k].append(_batch_major(outs[1], gk["bm"]))
            conv_out[k].append(cv)

    y_prompt = ys["p"]
    y_sample = _batch_major(ys["s"], bs)
    return (y_prompt, y_sample,
            jnp.stack(pool_out["p"]), jnp.stack(pool_out["s"]),
            jnp.stack([a for a, _ in ssm_out["p"]]), jnp.stack([b for _, b in ssm_out["p"]]),
            jnp.stack([a for a, _ in ssm_out["s"]]), jnp.stack([b for _, b in ssm_out["s"]]),
            _stacked_batch_major(conv_out["p"], bp), _stacked_batch_major(conv_out["s"], bs))
```

```python
import functools
import math

import jax
import jax.numpy as jnp
from jax import lax
from jax.experimental import pallas as pl
from jax.experimental.pallas import tpu as pltpu

POOL_WINDOWS = (2, 4, 8, 16)
POOL_STATE = max(POOL_WINDOWS) - 1
SSM_GC = 16
SSM_P = 64
CONV_W = 3
EPS = 1e-6
PAST_LEN = 16384

V7X_SUBLANES = 8
V7X_LANES = 128
V7X_MXU_DIM = 256
VMEM_LIMIT_BYTES = 56 * 1024 * 1024

PROMPT_POOL_FFN_TT = 64
PROMPT_FFN_TT = 128
PROMPT_S5_TT = 128
ADA_TILE_N = 1536
CAST_TILE_ROWS = 256

_F32 = jnp.float32
_BF16 = jnp.bfloat16


def _resident(block_shape, index_map):
    return pl.BlockSpec(block_shape, index_map, pipeline_mode=pl.Buffered(1))


def _whole(shape):
    nd = len(shape)
    return _resident(shape, lambda i: (0,) * nd)


def _layer_slab(shape, l):
    nd = len(shape)
    return _resident((None,) + tuple(shape[1:]), lambda i: (l,) + (0,) * (nd - 1))


def _unit_rms(x):
    return x * lax.rsqrt(jnp.mean(x * x, axis=-1, keepdims=True) + EPS)


def _pre_mod(x, g, scale, shift, bm):
    t, d = x.shape
    r = _unit_rms(x).reshape(t // bm, bm, d)
    return (r * (g * (1.0 + scale))[None] + shift[None]).reshape(t, d)


def _gated_residual(x, m, g, gate, bm):
    t, d = x.shape
    r = _unit_rms(m).reshape(t // bm, bm, d)
    return x + (r * (gate * g)[None]).reshape(t, d)


def _gelu(x):
    return 0.5 * x * (1.0 + lax.erf(x * math.sqrt(0.5)))


def _dot(a, b):
    return jnp.dot(a, b, preferred_element_type=_F32)


def _load_rows(x_ref, batch_major):
    if not batch_major:
        return x_ref[...]
    b, tt, d = x_ref.shape
    return jnp.swapaxes(x_ref[...], 0, 1).reshape(tt * b, d)


def _store_rows(y_ref, y, batch_major):
    if not batch_major:
        y_ref[...] = y
    else:
        b, tt, d = y_ref.shape
        y_ref[...] = jnp.swapaxes(y.reshape(tt, b, d), 0, 1)


def _mod3(mod_ref, k, d):
    return tuple(mod_ref[:, (3 * k + n) * d:(3 * k + n + 1) * d] for n in range(3))


def _cast_kernel(w_ref, o_ref):
    o_ref[...] = w_ref[...].astype(o_ref.dtype)


def _slab_to_bf16(w, slab, rows):
    _, r, c = w.shape
    return pl.pallas_call(
        _cast_kernel,
        grid=(r // rows,),
        in_specs=[pl.BlockSpec((None, rows, c), lambda i: (slab, i, 0))],
        out_specs=pl.BlockSpec((rows, c), lambda i: (i, 0)),
        out_shape=jax.ShapeDtypeStruct((r, c), _BF16),
        compiler_params=pltpu.CompilerParams(
            dimension_semantics=("arbitrary",), vmem_limit_bytes=VMEM_LIMIT_BYTES),
        name="to_bf16",
    )(w)


def _ada_kernel(cp_ref, cs_ref, w_ref, b_ref, mp_ref, ms_ref):
    bp = cp_ref.shape[0]
    c = jnp.concatenate([cp_ref[...], cs_ref[...]], axis=0)
    s = (c * jax.nn.sigmoid(c)).astype(_BF16)
    o = _dot(s, w_ref[...].astype(_BF16)) + b_ref[...]
    mp_ref[...] = o[:bp]
    ms_ref[...] = o[bp:]


def _ada_mod(c_prompt, c_sample, ada_w, ada_b):
    depth, d, n = ada_w.shape
    tn = ADA_TILE_N
    bp, bs = c_prompt.shape[0], c_sample.shape[0]
    return pl.pallas_call(
        _ada_kernel,
        grid=(depth, n // tn),
        in_specs=[
            pl.BlockSpec((bp, d), lambda l, j: (0, 0)),
            pl.BlockSpec((bs, d), lambda l, j: (0, 0)),
            pl.BlockSpec((None, d, tn), lambda l, j: (l, 0, j)),
            pl.BlockSpec((None, 1, tn), lambda l, j: (l, 0, j)),
        ],
        out_specs=[
            pl.BlockSpec((None, bp, tn), lambda l, j: (l, 0, j)),
            pl.BlockSpec((None, bs, tn), lambda l, j: (l, 0, j)),
        ],
        out_shape=[
            jax.ShapeDtypeStruct((depth, bp, n), _F32),
            jax.ShapeDtypeStruct((depth, bs, n), _F32),
        ],
        compiler_params=pltpu.CompilerParams(
            dimension_semantics=("arbitrary", "arbitrary"),
            vmem_limit_bytes=VMEM_LIMIT_BYTES),
        name="ada_mod",
    )(c_prompt, c_sample, ada_w, ada_b.reshape(depth, 1, n))


def _pool_mix(h, st, pw_ref, ps_ref, i, *, bm, pos_base):
    t, d = h.shape
    tt = t // bm
    gc = d // len(POOL_WINDOWS)
    ext = jnp.concatenate([st, h], axis=0)
    pos = None
    head = min(t, max(POOL_WINDOWS) * bm)
    if pos_base + 1 < max(POOL_WINDOWS):
        row = lax.broadcasted_iota(jnp.int32, (head, V7X_LANES), 0)
        pos = pos_base + i * tt + lax.shift_right_logical(row, bm.bit_length() - 1)
    parts = []
    for gi, w in enumerate(POOL_WINDOWS):
        cols = slice(gi * gc, (gi + 1) * gc)
        s = ext[:, cols]
        span = 1
        while span < w:
            n = s.shape[0]
            s = s[span * bm:] + s[:n - span * bm]
            span *= 2
        k0 = (POOL_STATE - (w - 1)) * bm
        wsum = s[k0:k0 + t]
        mean = wsum * (1.0 / w)
        if pos is not None:
            inv = 1.0 / jnp.minimum(pos + 1, w).astype(_F32)
            inv = jnp.concatenate([inv] * (gc // V7X_LANES), axis=1)
            mean = jnp.concatenate([wsum[:head] * inv, mean[head:]], axis=0)
        pooled = mean - h[:, cols]
        parts.append(_dot(pooled.astype(_BF16), pw_ref[gi].astype(_BF16)))
    return jnp.concatenate(parts, axis=1) * ps_ref[...], ext[t:, :]


def _conv_ffn(f, cv_ref, wup_ref, cw_ref, cb_ref, wdn_ref, h_ref, *, bm):
    t = f.shape[0]
    hid = wdn_ref.shape[0]
    hc = V7X_MXU_DIM

    def conv_cols(c0):
        cols = slice(c0, c0 + hc)
        up = _dot(f, wup_ref[:, cols])
        ext = jnp.concatenate([cv_ref[:, cols], up], axis=0)
        cv_ref[:, cols] = ext[t:, :]
        conv = cb_ref[:, cols]
        for k in range(CONV_W):
            conv = conv + ext[k * bm:k * bm + t, :] * cw_ref[k:k + 1, cols]
        return conv

    for c in range(hid // hc):
        gate_c = conv_cols(c * hc)
        val_c = conv_cols(hid + c * hc)
        h_ref[:, c * hc:(c + 1) * hc] = (_gelu(gate_c) * val_c).astype(_BF16)
    return _dot(h_ref[...], wdn_ref[...].astype(_BF16))


def _layer_kernel(*refs, layer, bm, pos_base, with_pool, with_ffn, in_batch_major, out_batch_major,
                  state_batch_major, cast_next):
    refs = list(refs)
    x_ref, mod_ref = refs[:2]
    del refs[:2]
    if with_pool:
        st0_ref, mpre_ref, mpost_ref, pw_ref, ps_ref = refs[:5]
        del refs[:5]
    if with_ffn:
        cv0_ref, fpre_ref, fpost_ref, wup_ref, cw_ref, cb_ref, wdn_ref = refs[:7]
        del refs[:7]
        if cast_next:
            wnext_ref = refs.pop(0)
    y_ref = refs.pop(0)
    if with_pool:
        st_ref = refs.pop(0)
    if with_ffn:
        cv_ref = refs.pop(0)
        if cast_next:
            wnext_out_ref = refs.pop(0)
            wnext_out_ref[...] = wnext_ref[...].astype(wnext_out_ref.dtype)
        (h_ref,) = refs
    i = pl.program_id(0)
    d = mod_ref.shape[1] // 6
    lrow = slice(layer, layer + 1)

    @pl.when(i == 0)
    def _():
        if with_pool and not state_batch_major:
            st_ref[...] = st0_ref[...]
        if with_ffn:
            cv_ref[...] = cv0_ref[...]

    x = _load_rows(x_ref, in_batch_major)
    if with_pool:
        shift, scale, gate = _mod3(mod_ref, 0, d)
        h = _pre_mod(x, mpre_ref[lrow, :], scale, shift, bm)
        st = _load_rows(st0_ref, True) if state_batch_major else st_ref[...]
        m, st = _pool_mix(h, st, pw_ref, ps_ref, i, bm=bm, pos_base=pos_base)
        _store_rows(st_ref, st, state_batch_major)
        x = _gated_residual(x, m, mpost_ref[lrow, :], gate, bm)
    if with_ffn:
        shift, scale, gate = _mod3(mod_ref, 1, d)
        f = _pre_mod(x, fpre_ref[lrow, :], scale, shift, bm).astype(_BF16)
        o = _conv_ffn(f, cv_ref, wup_ref, cw_ref, cb_ref.at[lrow, :], wdn_ref, h_ref, bm=bm)
        x = _gated_residual(x, o, fpost_ref[lrow, :], gate, bm)
    _store_rows(y_ref, x, out_batch_major)


def _token_layer(x, mod, gains, *, layer, bm, tt, pos_base=0, pool=None, ffn=None, cast_next=None,
                 in_batch_major=False, out_batch_major=False, state_batch_major=False):
    mix_pre_g, mix_post_g, ffn_pre_g, ffn_post_g = gains
    d = x.shape[-1]
    nrows = x.shape[0] * x.shape[1] if in_batch_major else x.shape[0]
    t = tt * bm
    assert not state_batch_major or nrows == t
    tm_spec = pl.BlockSpec((t, d), lambda i: (i, 0))
    bm_spec = pl.BlockSpec((bm, tt, d), lambda i: (0, i, 0))

    args = [x, mod]
    in_specs = [bm_spec if in_batch_major else tm_spec, _layer_slab(mod.shape, layer)]
    out_specs = [bm_spec if out_batch_major else tm_spec]
    out_shape = [jax.ShapeDtypeStruct((bm, nrows // bm, d) if out_batch_major else (nrows, d), _F32)]
    scratch = []
    if pool is not None:
        st0, st_slab, pool_w, pool_scale = pool
        st_shape = st0.shape if st_slab is None else st0.shape[1:]
        args += [st0, mix_pre_g, mix_post_g, pool_w, pool_scale]
        in_specs += [_whole(st0.shape) if st_slab is None else _layer_slab(st0.shape, st_slab)]
        in_specs += [_whole(a.shape) for a in args[-4:]]
        out_specs += [_whole(st_shape)]
        out_shape += [jax.ShapeDtypeStruct(st_shape, _F32)]
    if ffn is not None:
        cv0, w_up, conv_w, conv_b, w_down = ffn
        hid = w_down.shape[1]
        assert hid % V7X_MXU_DIM == 0 and cv0.shape[1] == 2 * hid
        args += [cv0, ffn_pre_g, ffn_post_g, w_up, conv_w, conv_b, w_down]
        in_specs += [_whole(cv0.shape), _whole(ffn_pre_g.shape), _whole(ffn_post_g.shape), _whole(w_up.shape),
                     _layer_slab(conv_w.shape, layer), _whole(conv_b.shape), _layer_slab(w_down.shape, layer)]
        out_specs += [_whole(cv0.shape)]
        out_shape += [jax.ShapeDtypeStruct(cv0.shape, _F32)]
        if cast_next is not None:
            w_stack, slab = cast_next
            _, r_w, c_w = w_stack.shape
            rows = r_w // (nrows // t)
            assert rows * (nrows // t) == r_w and rows % (2 * V7X_SUBLANES) == 0
            args += [w_stack]
            in_specs += [pl.BlockSpec((None, rows, c_w), lambda i: (slab, i, 0))]
            out_specs += [pl.BlockSpec((rows, c_w), lambda i: (i, 0))]
            out_shape += [jax.ShapeDtypeStruct((r_w, c_w), _BF16)]
        scratch += [pltpu.VMEM((t, hid), _BF16)]
    return pl.pallas_call(
        functools.partial(_layer_kernel, layer=layer, bm=bm, pos_base=pos_base,
                          with_pool=pool is not None, with_ffn=ffn is not None,
                          in_batch_major=in_batch_major, out_batch_major=out_batch_major,
                          state_batch_major=state_batch_major, cast_next=cast_next is not None),
        grid=(nrows // t,),
        in_specs=in_specs,
        out_specs=out_specs,
        out_shape=out_shape,
        scratch_shapes=scratch,
        compiler_params=pltpu.CompilerParams(
            dimension_semantics=("arbitrary",), vmem_limit_bytes=VMEM_LIMIT_BYTES),
        name="_".join(n for n, on in (("pool", pool), ("ffn", ffn)) if on is not None) + "_layer",
    )(*args)


def _s5_prep_kernel(are_ref, aim_ref, ldt_ref, btr_ref, bti_ref, ctr_ref, cti_ref,
                    lbr_ref, lbi_ref, wb_ref, wc_ref):
    a_re, a_im = are_ref[...], aim_ref[...]
    dt = jnp.exp(ldt_ref[...])
    mag = jnp.exp(a_re * dt)
    ang = a_im * dt
    lb_re = mag * jnp.cos(ang)
    lb_im = mag * jnp.sin(ang)
    n_re = lb_re - 1.0
    n_im = lb_im
    den = a_re * a_re + a_im * a_im
    f_re = ((n_re * a_re + n_im * a_im) / den)[:, None, :]
    f_im = ((n_im * a_re - n_re * a_im) / den)[:, None, :]
    lbr_ref[...] = lb_re
    lbi_ref[...] = lb_im
    b_re, b_im = btr_ref[...], bti_ref[...]
    bb_re = (f_re * b_re - f_im * b_im).astype(wb_ref.dtype)
    bb_im = (f_re * b_im + f_im * b_re).astype(wb_ref.dtype)
    c_re = ctr_ref[...].astype(wc_ref.dtype)
    nc_im = (-cti_ref[...]).astype(wc_ref.dtype)
    wb_ref[...] = jnp.zeros(wb_ref.shape, wb_ref.dtype)
    wc_ref[...] = jnp.zeros(wc_ref.shape, wc_ref.dtype)
    g, gc, p = b_re.shape
    gpt = wb_ref.shape[1] // gc
    half = gpt * p
    for gi in range(g):
        j, gl = divmod(gi, gpt)
        rows, cols = slice(gl * gc, (gl + 1) * gc), slice(gl * p, (gl + 1) * p)
        icols = slice(half + gl * p, half + (gl + 1) * p)
        wb_ref[j, rows, cols] = bb_re[gi]
        wb_ref[j, rows, icols] = bb_im[gi]
        wc_ref[j, cols, rows] = c_re[gi]
        wc_ref[j, icols, rows] = nc_im[gi]


def _s5_prep(a_re, a_im, log_dt, b_re, b_im, c_re, c_im):
    g, p = a_re.shape
    gc = b_re.shape[-1]
    gpt = V7X_MXU_DIM // gc
    tr = lambda a: jnp.swapaxes(a, 1, 2)
    return pl.pallas_call(
        _s5_prep_kernel,
        out_shape=[
            jax.ShapeDtypeStruct((g, p), _F32),
            jax.ShapeDtypeStruct((g, p), _F32),
            jax.ShapeDtypeStruct((g // gpt, gpt * gc, 2 * gpt * p), _BF16),
            jax.ShapeDtypeStruct((g // gpt, 2 * gpt * p, gpt * gc), _BF16),
        ],
        compiler_params=pltpu.CompilerParams(vmem_limit_bytes=VMEM_LIMIT_BYTES),
        name="s5_prep",
    )(a_re, a_im, log_dt.reshape(g, 1), tr(b_re), tr(b_im), tr(c_re), tr(c_im))


def _s5_kernel(x_ref, mod_ref, xr0_ref, xi0_ref, gpre_ref, gpost_ref, wb_ref, wc_ref,
               lbr_ref, lbi_ref, dsk_ref, ga_ref, gb_ref,
               y_ref, xr_ref, xi_ref, *, layer, bm):
    i = pl.program_id(0)
    t, d = x_ref.shape
    tt = t // bm
    ntile, kin, ncol2 = wb_ref.shape
    ncol = ncol2 // 2
    sub = V7X_SUBLANES
    lrow = slice(layer, layer + 1)

    @pl.when(i == 0)
    def _():
        xr_ref[...] = xr0_ref[...]
        xi_ref[...] = xi0_ref[...]

    x = x_ref[...]
    shift, scale, gate = _mod3(mod_ref, 0, d)
    h = _pre_mod(x, gpre_ref[lrow, :], scale, shift, bm)
    u = h.astype(_BF16)

    ys = []
    for j in range(ntile):
        scol = slice(j * ncol, (j + 1) * ncol)
        bu = _dot(u[:, j * kin:(j + 1) * kin], wb_ref[j])
        lr = jnp.broadcast_to(lbr_ref[:, scol], (sub, ncol))
        li = jnp.broadcast_to(lbi_ref[:, scol], (sub, ncol))
        blocks = [None] * (t // sub)
        for rb in range(bm // sub):
            rows = slice(rb * sub, (rb + 1) * sub)
            pr, pi = xr_ref[rows, scol], xi_ref[rows, scol]
            for ts in range(tt):
                r0 = ts * bm + rb * sub
                nr = lr * pr - li * pi + bu[r0:r0 + sub, :ncol]
                ni = lr * pi + li * pr + bu[r0:r0 + sub, ncol:]
                blocks[r0 // sub] = jnp.concatenate([nr, ni], axis=1)
                pr, pi = nr, ni
            xr_ref[rows, scol] = pr
            xi_ref[rows, scol] = pi
        ys.append(_dot(jnp.concatenate(blocks, axis=0).astype(_BF16), wc_ref[j]))
    y = jnp.concatenate(ys, axis=1) + dsk_ref[...] * h
    g = _gelu(y).astype(_BF16)
    out = _dot(g, ga_ref[...].astype(_BF16)) * jax.nn.sigmoid(_dot(g, gb_ref[...].astype(_BF16)))
    y_ref[...] = _gated_residual(x, out, gpost_ref[lrow, :], gate, bm)


def _s5_layer(x, mod, xr0, xi0, g_pre, g_post, wb, wc, lbr, lbi, dskip, glu_a, glu_b, *, layer, bm, tt):
    r, d = x.shape
    t = tt * bm
    nst = xr0.shape[1]
    return pl.pallas_call(
        functools.partial(_s5_kernel, layer=layer, bm=bm),
        grid=(r // t,),
        in_specs=[
            pl.BlockSpec((t, d), lambda i: (i, 0)),
            _layer_slab(mod.shape, layer),
            _whole((bm, nst)),
            _whole((bm, nst)),
            _whole(g_pre.shape),
            _whole(g_post.shape),
            _whole(wb.shape),
            _whole(wc.shape),
            _whole((1, nst)),
            _whole((1, nst)),
            _whole((1, d)),
            _whole(glu_a.shape),
            _whole(glu_b.shape),
        ],
        out_specs=[
            pl.BlockSpec((t, d), lambda i: (i, 0)),
            _whole((bm, nst)),
            _whole((bm, nst)),
        ],
        out_shape=[
            jax.ShapeDtypeStruct((r, d), _F32),
            jax.ShapeDtypeStruct((bm, nst), _F32),
            jax.ShapeDtypeStruct((bm, nst), _F32),
        ],
        compiler_params=pltpu.CompilerParams(
            dimension_semantics=("arbitrary",), vmem_limit_bytes=VMEM_LIMIT_BYTES),
        name="s5_layer",
    )(x, mod, xr0, xi0, g_pre, g_post, wb, wc, lbr, lbi, dskip, glu_a, glu_b)


def _time_major(a):
    b, l, c = a.shape
    return jnp.swapaxes(a, 0, 1).reshape(l * b, c)


def _batch_major(a, b):
    lb, c = a.shape
    return jnp.swapaxes(a.reshape(lb // b, b, c), 0, 1)


def _stacked_batch_major(states, b):
    st = jnp.stack(states)
    n, tb, c = st.shape
    return jnp.swapaxes(st.reshape(n, tb // b, b, c), 1, 2)


def kernel(x_prompt, x_sample, c_prompt, c_sample, state_pool, state_ssm_re, state_ssm_im, state_ffn_conv, ada_w, ada_b, mix_pre_g, mix_post_g, ffn_pre_g, ffn_post_g, pool_w, pool_scale, ssm_A_re, ssm_A_im, ssm_log_dt, ssm_B_re, ssm_B_im, ssm_C_re, ssm_C_im, ssm_D, ssm_glu_a, ssm_glu_b, ffn_w_up, ffn_conv_w, ffn_conv_b, ffn_w_down):
    depth = ada_w.shape[0]
    bp, lp, d = x_prompt.shape
    bs, ls, _ = x_sample.shape
    nup = ffn_w_up.shape[-1]
    groups = ssm_A_re.shape[1]
    nst = groups * SSM_P

    mod_p, mod_s = _ada_mod(c_prompt, c_sample, ada_w, ada_b)
    gains = (mix_pre_g, mix_post_g, ffn_pre_g, ffn_post_g)
    w_dn = ffn_w_down
    w_up_next = _slab_to_bf16(ffn_w_up, 0, CAST_TILE_ROWS)

    geo = {
        "p": dict(bm=bp, pool_ffn_tt=PROMPT_POOL_FFN_TT, ffn_tt=PROMPT_FFN_TT, s5_tt=PROMPT_S5_TT, pos_base=0,
                  fuse_pool=True),
        "s": dict(bm=bs, pool_ffn_tt=ls // 2, ffn_tt=ls // 2, s5_tt=ls // 2, pos_base=PAST_LEN, fuse_pool=False,
                  pool_tt=ls),
    }
    ys = {"p": x_prompt, "s": x_sample}
    mods = {"p": mod_p, "s": mod_s}
    pool_out = {"p": [], "s": []}
    ssm_out = {"p": [], "s": []}
    conv_out = {"p": [], "s": []}

    for l in range(depth):
        j = l // 2
        w_up = w_up_next
        pool = {"p": None, "s": None}
        if l % 2 == 0:
            pw = pool_w[j]
            ps = pool_scale[j].reshape(1, d)
            pool["p"] = (jnp.zeros((POOL_STATE * bp, d), _F32), None, pw, ps)
            pool["s"] = (state_pool, j, pw, ps)
        else:
            lb_re, lb_im, wb, wc = _s5_prep(ssm_A_re[j], ssm_A_im[j], ssm_log_dt[j], ssm_B_re[j], ssm_B_im[j],
                                            ssm_C_re[j], ssm_C_im[j])
            lbr = lb_re.reshape(1, nst)
            lbi = lb_im.reshape(1, nst)
            ga = ssm_glu_a[j]
            gb = ssm_glu_b[j]
            for k in ("p", "s"):
                gk = geo[k]
                if k == "p":
                    xr0 = jnp.zeros((bp, nst), _F32)
                    xi0 = xr0
                else:
                    xr0 = state_ssm_re[j].reshape(bs, nst)
                    xi0 = state_ssm_im[j].reshape(bs, nst)
                ys[k], xr, xi = _s5_layer(ys[k], mods[k], xr0, xi0, mix_pre_g, mix_post_g,
                                          wb, wc, lbr, lbi, ssm_D[j].reshape(1, d), ga, gb,
                                          layer=l, bm=gk["bm"], tt=gk["s5_tt"])
                ssm_out[k].append((xr.reshape(gk["bm"], groups, SSM_P), xi.reshape(gk["bm"], groups, SSM_P)))

        for k in ("p", "s"):
            gk = geo[k]
            if k == "p":
                cv0 = jnp.zeros(((CONV_W - 1) * bp, nup), _F32)
            else:
                cv0 = _time_major(state_ffn_conv[l])
            ffn = (cv0, w_up, ffn_conv_w, ffn_conv_b, w_dn)
            cast_next = (ffn_w_up, l + 1) if k == "p" and l + 1 < depth else None
            common = dict(layer=l, bm=gk["bm"], pos_base=gk["pos_base"])
            if pool[k] is not None and not gk["fuse_pool"]:
                ys[k], st = _token_layer(ys[k], mods[k], gains, tt=gk["pool_tt"], pool=pool[k],
                                         in_batch_major=True, state_batch_major=True, **common)
                pool_out[k].append(st)
                pool[k] = None
            tt = gk["ffn_tt"] if pool[k] is None else gk["pool_ffn_tt"]
            outs = list(_token_layer(ys[k], mods[k], gains, tt=tt, pool=pool[k], ffn=ffn, cast_next=cast_next,
                                in_batch_major=(l == 0 and pool[k] is not None),
                                out_batch_major=(k == "p" and l == depth - 1), **common))
            if cast_next is not None:
                w_up_next = outs.pop()
            ys[k], cv = outs[0], outs[-1]
            if pool[k] is not None:
                pool_out[k].append(_batch_major(outs[1], gk["bm"]))
            conv_out[k].append(cv)

    y_prompt = ys["p"]
    y_sample = _batch_major(ys["s"], bs)
    return (y_prompt, y_sample,
            jnp.stack(pool_out["p"]), jnp.stack(pool_out["s"]),
            jnp.stack([a for a, _ in ssm_out["p"]]), jnp.stack([b for _, b in ssm_out["p"]]),
            jnp.stack([a for a, _ in ssm_out["s"]]), jnp.stack([b for _, b in ssm_out["s"]]),
            _stacked_batch_major(conv_out["p"], bp), _stacked_batch_major(conv_out["s"], bs))
```

```python
import functools
import math

import jax
import jax.numpy as jnp
from jax import lax
from jax.experimental import pallas as pl
from jax.experimental.pallas import tpu as pltpu

POOL_WINDOWS = (2, 4, 8, 16)
POOL_STATE = max(POOL_WINDOWS) - 1
SSM_GC = 16
SSM_P = 64
CONV_W = 3
EPS = 1e-6
PAST_LEN = 16384

V7X_SUBLANES = 8
V7X_LANES = 128
V7X_MXU_DIM = 256
VMEM_LIMIT_BYTES = 56 * 1024 * 1024

PROMPT_POOL_FFN_TT = 64
PROMPT_FFN_TT = 128
PROMPT_S5_TT = 128
ADA_TILE_N = 1536
CAST_TILE_ROWS = 256

_F32 = jnp.float32
_BF16 = jnp.bfloat16


def _resident(block_shape, index_map):
    return pl.BlockSpec(block_shape, index_map, pipeline_mode=pl.Buffered(1))


def _whole(shape):
    nd = len(shape)
    return _resident(shape, lambda i: (0,) * nd)


def _layer_slab(shape, l):
    nd = len(shape)
    return _resident((None,) + tuple(shape[1:]), lambda i: (l,) + (0,) * (nd - 1))


def _unit_rms(x):
    return x * lax.rsqrt(jnp.mean(x * x, axis=-1, keepdims=True) + EPS)


def _pre_mod(x, g, scale, shift, bm):
    t, d = x.shape
    r = _unit_rms(x).reshape(t // bm, bm, d)
    return (r * (g * (1.0 + scale))[None] + shift[None]).reshape(t, d)


def _gated_residual(x, m, g, gate, bm):
    t, d = x.shape
    r = _unit_rms(m).reshape(t // bm, bm, d)
    return x + (r * (gate * g)[None]).reshape(t, d)


def _gelu(x):
    return 0.5 * x * (1.0 + lax.erf(x * math.sqrt(0.5)))


def _dot(a, b):
    return jnp.dot(a, b, preferred_element_type=_F32)


def _load_rows(x_ref, batch_major):
    if not batch_major:
        return x_ref[...]
    b, tt, d = x_ref.shape
    return jnp.swapaxes(x_ref[...], 0, 1).reshape(tt * b, d)


def _store_rows(y_ref, y, batch_major):
    if not batch_major:
        y_ref[...] = y
    else:
        b, tt, d = y_ref.shape
        y_ref[...] = jnp.swapaxes(y.reshape(tt, b, d), 0, 1)


def _mod3(mod_ref, k, d):
    return tuple(mod_ref[:, (3 * k + n) * d:(3 * k + n + 1) * d] for n in range(3))


def _cast_kernel(w_ref, o_ref):
    o_ref[...] = w_ref[...].astype(o_ref.dtype)


def _slab_to_bf16(w, slab, rows):
    _, r, c = w.shape
    return pl.pallas_call(
        _cast_kernel,
        grid=(r // rows,),
        in_specs=[pl.BlockSpec((None, rows, c), lambda i: (slab, i, 0))],
        out_specs=pl.BlockSpec((rows, c), lambda i: (i, 0)),
        out_shape=jax.ShapeDtypeStruct((r, c), _BF16),
        compiler_params=pltpu.CompilerParams(
            dimension_semantics=("arbitrary",), vmem_limit_bytes=VMEM_LIMIT_BYTES),
        name="to_bf16",
    )(w)


def _ada_kernel(cp_ref, cs_ref, w_ref, b_ref, mp_ref, ms_ref):
    bp = cp_ref.shape[0]
    c = jnp.concatenate([cp_ref[...], cs_ref[...]], axis=0)
    s = (c * jax.nn.sigmoid(c)).astype(_BF16)
    o = _dot(s, w_ref[...].astype(_BF16)) + b_ref[...]
    mp_ref[...] = o[:bp]
    ms_ref[...] = o[bp:]


def _ada_mod(c_prompt, c_sample, ada_w, ada_b):
    depth, d, n = ada_w.shape
    tn = ADA_TILE_N
    bp, bs = c_prompt.shape[0], c_sample.shape[0]
    return pl.pallas_call(
        _ada_kernel,
        grid=(depth, n // tn),
        in_specs=[
            pl.BlockSpec((bp, d), lambda l, j: (0, 0)),
            pl.BlockSpec((bs, d), lambda l, j: (0, 0)),
            pl.BlockSpec((None, d, tn), lambda l, j: (l, 0, j)),
            pl.BlockSpec((None, 1, tn), lambda l, j: (l, 0, j)),
        ],
        out_specs=[
            pl.BlockSpec((None, bp, tn), lambda l, j: (l, 0, j)),
            pl.BlockSpec((None, bs, tn), lambda l, j: (l, 0, j)),
        ],
        out_shape=[
            jax.ShapeDtypeStruct((depth, bp, n), _F32),
            jax.ShapeDtypeStruct((depth, bs, n), _F32),
        ],
        compiler_params=pltpu.CompilerParams(
            dimension_semantics=("arbitrary", "arbitrary"),
            vmem_limit_bytes=VMEM_LIMIT_BYTES),
        name="ada_mod",
    )(c_prompt, c_sample, ada_w, ada_b.reshape(depth, 1, n))


def _pool_mix(h, st, pw_ref, ps_ref, i, *, bm, pos_base):
    t, d = h.shape
    tt = t // bm
    gc = d // len(POOL_WINDOWS)
    ext = jnp.concatenate([st, h], axis=0)
    pos = None
    head = min(t, max(POOL_WINDOWS) * bm)
    if pos_base + 1 < max(POOL_WINDOWS):
        row = lax.broadcasted_iota(jnp.int32, (head, V7X_LANES), 0)
        pos = pos_base + i * tt + lax.shift_right_logical(row, bm.bit_length() - 1)
    parts = []
    for gi, w in enumerate(POOL_WINDOWS):
        cols = slice(gi * gc, (gi + 1) * gc)
        s = ext[:, cols]
        span = 1
        while span < w:
            n = s.shape[0]
            s = s[span * bm:] + s[:n - span * bm]
            span *= 2
        k0 = (POOL_STATE - (w - 1)) * bm
        wsum = s[k0:k0 + t]
        mean = wsum * (1.0 / w)
        if pos is not None:
            inv = 1.0 / jnp.minimum(pos + 1, w).astype(_F32)
            inv = jnp.concatenate([inv] * (gc // V7X_LANES), axis=1)
            mean = jnp.concatenate([wsum[:head] * inv, mean[head:]], axis=0)
        pooled = mean - h[:, cols]
        parts.append(_dot(pooled.astype(_BF16), pw_ref[gi].astype(_BF16)))
    return jnp.concatenate(parts, axis=1) * ps_ref[...], ext[t:, :]


def _conv_ffn(f, cv_ref, wup_ref, cw_ref, cb_ref, wdn_ref, h_ref, *, bm):
    t = f.shape[0]
    hid = wdn_ref.shape[0]
    hc = V7X_MXU_DIM

    def conv_cols(c0):
        cols = slice(c0, c0 + hc)
        up = _dot(f, wup_ref[:, cols])
        ext = jnp.concatenate([cv_ref[:, cols], up], axis=0)
        cv_ref[:, cols] = ext[t:, :]
        conv = cb_ref[:, cols]
        for k in range(CONV_W):
            conv = conv + ext[k * bm:k * bm + t, :] * cw_ref[k:k + 1, cols]
        return conv

    for c in range(hid // hc):
        gate_c = conv_cols(c * hc)
        val_c = conv_cols(hid + c * hc)
        h_ref[:, c * hc:(c + 1) * hc] = (_gelu(gate_c) * val_c).astype(_BF16)
    return _dot(h_ref[...], wdn_ref[...].astype(_BF16))


def _layer_kernel(*refs, layer, bm, pos_base, with_pool, with_ffn, in_batch_major, out_batch_major,
                  state_batch_major, cast_next, pool_zero, ffn_zero):
    refs = list(refs)
    x_ref, mod_ref = refs[:2]
    del refs[:2]
    if with_pool:
        st0_ref = None if pool_zero else refs.pop(0)
        mpre_ref, mpost_ref, pw_ref, ps_ref = refs[:4]
        del refs[:4]
    if with_ffn:
        cv0_ref = None if ffn_zero else refs.pop(0)
        fpre_ref, fpost_ref, wup_ref, cw_ref, cb_ref, wdn_ref = refs[:6]
        del refs[:6]
        if cast_next:
            wnext_ref = refs.pop(0)
    y_ref = refs.pop(0)
    if with_pool:
        st_ref = refs.pop(0)
    if with_ffn:
        cv_ref = refs.pop(0)
        if cast_next:
            wnext_out_ref = refs.pop(0)
            wnext_out_ref[...] = wnext_ref[...].astype(wnext_out_ref.dtype)
        (h_ref,) = refs
    i = pl.program_id(0)
    d = mod_ref.shape[1] // 6
    lrow = slice(layer, layer + 1)

    @pl.when(i == 0)
    def _():
        if with_pool and not state_batch_major:
            st_ref[...] = jnp.zeros_like(st_ref) if pool_zero else st0_ref[...]
        if with_ffn:
            cv_ref[...] = jnp.zeros_like(cv_ref) if ffn_zero else cv0_ref[...]

    x = _load_rows(x_ref, in_batch_major)
    if with_pool:
        shift, scale, gate = _mod3(mod_ref, 0, d)
        h = _pre_mod(x, mpre_ref[lrow, :], scale, shift, bm)
        st = _load_rows(st0_ref, True) if state_batch_major else st_ref[...]
        m, st = _pool_mix(h, st, pw_ref, ps_ref, i, bm=bm, pos_base=pos_base)
        _store_rows(st_ref, st, state_batch_major)
        x = _gated_residual(x, m, mpost_ref[lrow, :], gate, bm)
    if with_ffn:
        shift, scale, gate = _mod3(mod_ref, 1, d)
        f = _pre_mod(x, fpre_ref[lrow, :], scale, shift, bm).astype(_BF16)
        o = _conv_ffn(f, cv_ref, wup_ref, cw_ref, cb_ref.at[lrow, :], wdn_ref, h_ref, bm=bm)
        x = _gated_residual(x, o, fpost_ref[lrow, :], gate, bm)
    _store_rows(y_ref, x, out_batch_major)


def _token_layer(x, mod, gains, *, layer, bm, tt, pos_base=0, pool=None, ffn=None, cast_next=None,
                 in_batch_major=False, out_batch_major=False, state_batch_major=False):
    mix_pre_g, mix_post_g, ffn_pre_g, ffn_post_g = gains
    d = x.shape[-1]
    nrows = x.shape[0] * x.shape[1] if in_batch_major else x.shape[0]
    t = tt * bm
    assert not state_batch_major or nrows == t
    tm_spec = pl.BlockSpec((t, d), lambda i: (i, 0))
    bm_spec = pl.BlockSpec((bm, tt, d), lambda i: (0, i, 0))

    args = [x, mod]
    in_specs = [bm_spec if in_batch_major else tm_spec, _layer_slab(mod.shape, layer)]
    out_specs = [bm_spec if out_batch_major else tm_spec]
    out_shape = [jax.ShapeDtypeStruct((bm, nrows // bm, d) if out_batch_major else (nrows, d), _F32)]
    scratch = []
    if pool is not None:
        st0, st_slab, pool_w, pool_scale = pool
        if st0 is None:
            st_shape = (POOL_STATE * bm, d)
        else:
            st_shape = st0.shape if st_slab is None else st0.shape[1:]
            args += [st0]
            in_specs += [_whole(st0.shape) if st_slab is None else _layer_slab(st0.shape, st_slab)]
        args += [mix_pre_g, mix_post_g, pool_w, pool_scale]
        in_specs += [_whole(a.shape) for a in args[-4:]]
        out_specs += [_whole(st_shape)]
        out_shape += [jax.ShapeDtypeStruct(st_shape, _F32)]
    if ffn is not None:
        cv0, w_up, conv_w, conv_b, w_down = ffn
        hid = w_down.shape[1]
        cv_shape = ((CONV_W - 1) * bm, 2 * hid)
        assert hid % V7X_MXU_DIM == 0 and (cv0 is None or cv0.shape == cv_shape)
        if cv0 is not None:
            args += [cv0]
            in_specs += [_whole(cv_shape)]
        args += [ffn_pre_g, ffn_post_g, w_up, conv_w, conv_b, w_down]
        in_specs += [_whole(ffn_pre_g.shape), _whole(ffn_post_g.shape), _whole(w_up.shape),
                     _layer_slab(conv_w.shape, layer), _whole(conv_b.shape), _layer_slab(w_down.shape, layer)]
        out_specs += [_whole(cv_shape)]
        out_shape += [jax.ShapeDtypeStruct(cv_shape, _F32)]
        if cast_next is not None:
            w_stack, slab = cast_next
            _, r_w, c_w = w_stack.shape
            rows = r_w // (nrows // t)
            assert rows * (nrows // t) == r_w and rows % (2 * V7X_SUBLANES) == 0
            args += [w_stack]
            in_specs += [pl.BlockSpec((None, rows, c_w), lambda i: (slab, i, 0))]
            out_specs += [pl.BlockSpec((rows, c_w), lambda i: (i, 0))]
            out_shape += [jax.ShapeDtypeStruct((r_w, c_w), _BF16)]
        scratch += [pltpu.VMEM((t, hid), _BF16)]
    return pl.pallas_call(
        functools.partial(_layer_kernel, layer=layer, bm=bm, pos_base=pos_base,
                          with_pool=pool is not None, with_ffn=ffn is not None,
                          in_batch_major=in_batch_major, out_batch_major=out_batch_major,
                          state_batch_major=state_batch_major, cast_next=cast_next is not None,
                          pool_zero=pool is not None and pool[0] is None,
                          ffn_zero=ffn is not None and ffn[0] is None),
        grid=(nrows // t,),
        in_specs=in_specs,
        out_specs=out_specs,
        out_shape=out_shape,
        scratch_shapes=scratch,
        compiler_params=pltpu.CompilerParams(
            dimension_semantics=("arbitrary",), vmem_limit_bytes=VMEM_LIMIT_BYTES),
        name="_".join(n for n, on in (("pool", pool), ("ffn", ffn)) if on is not None) + "_layer",
    )(*args)


def _s5_prep_kernel(are_ref, aim_ref, ldt_ref, btr_ref, bti_ref, ctr_ref, cti_ref,
                    lbr_ref, lbi_ref, wb_ref, wc_ref):
    a_re, a_im = are_ref[...], aim_ref[...]
    dt = jnp.exp(ldt_ref[...])
    mag = jnp.exp(a_re * dt)
    ang = a_im * dt
    lb_re = mag * jnp.cos(ang)
    lb_im = mag * jnp.sin(ang)
    n_re = lb_re - 1.0
    n_im = lb_im
    den = a_re * a_re + a_im * a_im
    f_re = ((n_re * a_re + n_im * a_im) / den)[:, None, :]
    f_im = ((n_im * a_re - n_re * a_im) / den)[:, None, :]
    lbr_ref[...] = lb_re
    lbi_ref[...] = lb_im
    b_re, b_im = btr_ref[...], bti_ref[...]
    bb_re = (f_re * b_re - f_im * b_im).astype(wb_ref.dtype)
    bb_im = (f_re * b_im + f_im * b_re).astype(wb_ref.dtype)
    c_re = ctr_ref[...].astype(wc_ref.dtype)
    nc_im = (-cti_ref[...]).astype(wc_ref.dtype)
    wb_ref[...] = jnp.zeros(wb_ref.shape, wb_ref.dtype)
    wc_ref[...] = jnp.zeros(wc_ref.shape, wc_ref.dtype)
    g, gc, p = b_re.shape
    gpt = wb_ref.shape[1] // gc
    half = gpt * p
    for gi in range(g):
        j, gl = divmod(gi, gpt)
        rows, cols = slice(gl * gc, (gl + 1) * gc), slice(gl * p, (gl + 1) * p)
        icols = slice(half + gl * p, half + (gl + 1) * p)
        wb_ref[j, rows, cols] = bb_re[gi]
        wb_ref[j, rows, icols] = bb_im[gi]
        wc_ref[j, cols, rows] = c_re[gi]
        wc_ref[j, icols, rows] = nc_im[gi]


def _s5_prep(a_re, a_im, log_dt, b_re, b_im, c_re, c_im):
    g, p = a_re.shape
    gc = b_re.shape[-1]
    gpt = V7X_MXU_DIM // gc
    tr = lambda a: jnp.swapaxes(a, 1, 2)
    return pl.pallas_call(
        _s5_prep_kernel,
        out_shape=[
            jax.ShapeDtypeStruct((g, p), _F32),
            jax.ShapeDtypeStruct((g, p), _F32),
            jax.ShapeDtypeStruct((g // gpt, gpt * gc, 2 * gpt * p), _BF16),
            jax.ShapeDtypeStruct((g // gpt, 2 * gpt * p, gpt * gc), _BF16),
        ],
        compiler_params=pltpu.CompilerParams(vmem_limit_bytes=VMEM_LIMIT_BYTES),
        name="s5_prep",
    )(a_re, a_im, log_dt.reshape(g, 1), tr(b_re), tr(b_im), tr(c_re), tr(c_im))


def _s5_kernel(*refs, layer, bm, zero_state):
    refs = list(refs)
    x_ref, mod_ref = refs[:2]
    del refs[:2]
    if not zero_state:
        xr0_ref, xi0_ref = refs[:2]
        del refs[:2]
    gpre_ref, gpost_ref, wb_ref, wc_ref, lbr_ref, lbi_ref, dsk_ref, ga_ref, gb_ref, y_ref, xr_ref, xi_ref = refs
    i = pl.program_id(0)
    t, d = x_ref.shape
    tt = t // bm
    ntile, kin, ncol2 = wb_ref.shape
    ncol = ncol2 // 2
    sub = V7X_SUBLANES
    lrow = slice(layer, layer + 1)

    @pl.when(i == 0)
    def _():
        xr_ref[...] = jnp.zeros_like(xr_ref) if zero_state else xr0_ref[...]
        xi_ref[...] = jnp.zeros_like(xi_ref) if zero_state else xi0_ref[...]

    x = x_ref[...]
    shift, scale, gate = _mod3(mod_ref, 0, d)
    h = _pre_mod(x, gpre_ref[lrow, :], scale, shift, bm)
    u = h.astype(_BF16)

    ys = []
    for j in range(ntile):
        scol = slice(j * ncol, (j + 1) * ncol)
        bu = _dot(u[:, j * kin:(j + 1) * kin], wb_ref[j])
        lr = jnp.broadcast_to(lbr_ref[:, scol], (sub, ncol))
        li = jnp.broadcast_to(lbi_ref[:, scol], (sub, ncol))
        blocks = [None] * (t // sub)
        for rb in range(bm // sub):
            rows = slice(rb * sub, (rb + 1) * sub)
            pr, pi = xr_ref[rows, scol], xi_ref[rows, scol]
            for ts in range(tt):
                r0 = ts * bm + rb * sub
                nr = lr * pr - li * pi + bu[r0:r0 + sub, :ncol]
                ni = lr * pi + li * pr + bu[r0:r0 + sub, ncol:]
                blocks[r0 // sub] = jnp.concatenate([nr, ni], axis=1)
                pr, pi = nr, ni
            xr_ref[rows, scol] = pr
            xi_ref[rows, scol] = pi
        ys.append(_dot(jnp.concatenate(blocks, axis=0).astype(_BF16), wc_ref[j]))
    y = jnp.concatenate(ys, axis=1) + dsk_ref[...] * h
    g = _gelu(y).astype(_BF16)
    out = _dot(g, ga_ref[...].astype(_BF16)) * jax.nn.sigmoid(_dot(g, gb_ref[...].astype(_BF16)))
    y_ref[...] = _gated_residual(x, out, gpost_ref[lrow, :], gate, bm)


def _s5_layer(x, mod, state0, g_pre, g_post, wb, wc, lbr, lbi, dskip, glu_a, glu_b, *, layer, bm, tt):
    r, d = x.shape
    t = tt * bm
    nst = lbr.shape[1]
    state_args = [] if state0 is None else list(state0)
    return pl.pallas_call(
        functools.partial(_s5_kernel, layer=layer, bm=bm, zero_state=state0 is None),
        grid=(r // t,),
        in_specs=[
            pl.BlockSpec((t, d), lambda i: (i, 0)),
            _layer_slab(mod.shape, layer),
            *[_whole((bm, nst)) for _ in state_args],
            _whole(g_pre.shape),
            _whole(g_post.shape),
            _whole(wb.shape),
            _whole(wc.shape),
            _whole((1, nst)),
            _whole((1, nst)),
            _whole((1, d)),
            _whole(glu_a.shape),
            _whole(glu_b.shape),
        ],
        out_specs=[
            pl.BlockSpec((t, d), lambda i: (i, 0)),
            _whole((bm, nst)),
            _whole((bm, nst)),
        ],
        out_shape=[
            jax.ShapeDtypeStruct((r, d), _F32),
            jax.ShapeDtypeStruct((bm, nst), _F32),
            jax.ShapeDtypeStruct((bm, nst), _F32),
        ],
        compiler_params=pltpu.CompilerParams(
            dimension_semantics=("arbitrary",), vmem_limit_bytes=VMEM_LIMIT_BYTES),
        name="s5_layer",
    )(x, mod, *state_args, g_pre, g_post, wb, wc, lbr, lbi, dskip, glu_a, glu_b)


def _time_major(a):
    b, l, c = a.shape
    return jnp.swapaxes(a, 0, 1).reshape(l * b, c)


def _batch_major(a, b):
    lb, c = a.shape
    return jnp.swapaxes(a.reshape(lb // b, b, c), 0, 1)


def _stacked_batch_major(states, b):
    st = jnp.stack(states)
    n, tb, c = st.shape
    return jnp.swapaxes(st.reshape(n, tb // b, b, c), 1, 2)


def kernel(x_prompt, x_sample, c_prompt, c_sample, state_pool, state_ssm_re, state_ssm_im, state_ffn_conv, ada_w, ada_b, mix_pre_g, mix_post_g, ffn_pre_g, ffn_post_g, pool_w, pool_scale, ssm_A_re, ssm_A_im, ssm_log_dt, ssm_B_re, ssm_B_im, ssm_C_re, ssm_C_im, ssm_D, ssm_glu_a, ssm_glu_b, ffn_w_up, ffn_conv_w, ffn_conv_b, ffn_w_down):
    depth = ada_w.shape[0]
    bp, lp, d = x_prompt.shape
    bs, ls, _ = x_sample.shape
    groups = ssm_A_re.shape[1]
    nst = groups * SSM_P

    mod_p, mod_s = _ada_mod(c_prompt, c_sample, ada_w, ada_b)
    gains = (mix_pre_g, mix_post_g, ffn_pre_g, ffn_post_g)
    w_dn = ffn_w_down
    w_up_next = _slab_to_bf16(ffn_w_up, 0, CAST_TILE_ROWS)

    geo = {
        "p": dict(bm=bp, pool_ffn_tt=PROMPT_POOL_FFN_TT, ffn_tt=PROMPT_FFN_TT, s5_tt=PROMPT_S5_TT, pos_base=0,
                  fuse_pool=True),
        "s": dict(bm=bs, pool_ffn_tt=ls // 2, ffn_tt=ls // 2, s5_tt=ls // 2, pos_base=PAST_LEN, fuse_pool=False,
                  pool_tt=ls),
    }
    ys = {"p": x_prompt, "s": x_sample}
    mods = {"p": mod_p, "s": mod_s}
    pool_out = {"p": [], "s": []}
    ssm_out = {"p": [], "s": []}
    conv_out = {"p": [], "s": []}

    for l in range(depth):
        j = l // 2
        w_up = w_up_next
        pool = {"p": None, "s": None}
        if l % 2 == 0:
            pw = pool_w[j]
            ps = pool_scale[j].reshape(1, d)
            pool["p"] = (None, None, pw, ps)
            pool["s"] = (state_pool, j, pw, ps)
        else:
            lb_re, lb_im, wb, wc = _s5_prep(ssm_A_re[j], ssm_A_im[j], ssm_log_dt[j], ssm_B_re[j], ssm_B_im[j],
                                            ssm_C_re[j], ssm_C_im[j])
            lbr = lb_re.reshape(1, nst)
            lbi = lb_im.reshape(1, nst)
            ga = ssm_glu_a[j]
            gb = ssm_glu_b[j]
            for k in ("p", "s"):
                gk = geo[k]
                state0 = None if k == "p" else (state_ssm_re[j].reshape(bs, nst), state_ssm_im[j].reshape(bs, nst))
                ys[k], xr, xi = _s5_layer(ys[k], mods[k], state0, mix_pre_g, mix_post_g,
                                          wb, wc, lbr, lbi, ssm_D[j].reshape(1, d), ga, gb,
                                          layer=l, bm=gk["bm"], tt=gk["s5_tt"])
                ssm_out[k].append((xr.reshape(gk["bm"], groups, SSM_P), xi.reshape(gk["bm"], groups, SSM_P)))

        for k in ("p", "s"):
            gk = geo[k]
            cv0 = None if k == "p" else _time_major(state_ffn_conv[l])
            ffn = (cv0, w_up, ffn_conv_w, ffn_conv_b, w_dn)
            cast_next = (ffn_w_up, l + 1) if k == "p" and l + 1 < depth else None
            common = dict(layer=l, bm=gk["bm"], pos_base=gk["pos_base"])
            if pool[k] is not None and not gk["fuse_pool"]:
                ys[k], st = _token_layer(ys[k], mods[k], gains, tt=gk["pool_tt"], pool=pool[k],
                                         in_batch_major=True, state_batch_major=True, **common)
                pool_out[k].append(st)
                pool[k] = None
            tt = gk["ffn_tt"] if pool[k] is None else gk["pool_ffn_tt"]
            outs = list(_token_layer(ys[k], mods[k], gains, tt=tt, pool=pool[k], ffn=ffn, cast_next=cast_next,
                                in_batch_major=(l == 0 and pool[k] is not None),
                                out_batch_major=(k == "p" and l == depth - 1), **common))
            if cast_next is not None:
                w_up_next = outs.pop()
            ys[k], cv = outs[0], outs[-1]
            if pool[k] is not None:
                pool_out[k].append(_batch_major(outs[1], gk["bm"]))
            conv_out[k].append(cv)

    y_prompt = ys["p"]
    y_sample = _batch_major(ys["s"], bs)
    return (y_prompt, y_sample,
            jnp.stack(pool_out["p"]), jnp.stack(pool_out["s"]),
            jnp.stack([a for a, _ in ssm_out["p"]]), jnp.stack([b for _, b in ssm_out["p"]]),
            jnp.stack([a for a, _ in ssm_out["s"]]), jnp.stack([b for _, b in ssm_out["s"]]),
            _stacked_batch_major(conv_out["p"], bp), _stacked_batch_major(conv_out["s"], bs))
```

```python
import functools
import math

import jax
import jax.numpy as jnp
from jax import lax
from jax.experimental import pallas as pl
from jax.experimental.pallas import tpu as pltpu

POOL_WINDOWS = (2, 4, 8, 16)
POOL_STATE = max(POOL_WINDOWS) - 1
SSM_GC = 16
SSM_P = 64
CONV_W = 3
EPS = 1e-6
PAST_LEN = 16384

V7X_SUBLANES = 8
V7X_LANES = 128
V7X_MXU_DIM = 256
VMEM_LIMIT_BYTES = 56 * 1024 * 1024

PROMPT_POOL_FFN_TT = 64
PROMPT_FFN_TT = 128
PROMPT_S5_TT = 128
ADA_TILE_N = 1536
CAST_TILE_ROWS = 256

_F32 = jnp.float32
_BF16 = jnp.bfloat16


def _resident(block_shape, index_map):
    return pl.BlockSpec(block_shape, index_map, pipeline_mode=pl.Buffered(1))


def _whole(shape):
    nd = len(shape)
    return _resident(shape, lambda i: (0,) * nd)


def _layer_slab(shape, l):
    nd = len(shape)
    return _resident((None,) + tuple(shape[1:]), lambda i: (l,) + (0,) * (nd - 1))


def _unit_rms(x):
    return x * lax.rsqrt(jnp.mean(x * x, axis=-1, keepdims=True) + EPS)


def _pre_mod(x, g, scale, shift, bm):
    t, d = x.shape
    r = _unit_rms(x).reshape(t // bm, bm, d)
    return (r * (g * (1.0 + scale))[None] + shift[None]).reshape(t, d)


def _gated_residual(x, m, g, gate, bm):
    t, d = x.shape
    r = _unit_rms(m).reshape(t // bm, bm, d)
    return x + (r * (gate * g)[None]).reshape(t, d)


def _gelu(x):
    return 0.5 * x * (1.0 + lax.erf(x * math.sqrt(0.5)))


def _dot(a, b):
    return jnp.dot(a, b, preferred_element_type=_F32)


def _load_rows(x_ref, batch_major):
    if not batch_major:
        return x_ref[...]
    b, tt, d = x_ref.shape
    return jnp.swapaxes(x_ref[...], 0, 1).reshape(tt * b, d)


def _store_rows(y_ref, y, batch_major):
    if not batch_major:
        y_ref[...] = y
    else:
        b, tt, d = y_ref.shape
        y_ref[...] = jnp.swapaxes(y.reshape(tt, b, d), 0, 1)


def _mod3(mod_ref, k, d):
    return tuple(mod_ref[:, (3 * k + n) * d:(3 * k + n + 1) * d] for n in range(3))


def _cast_kernel(wa_ref, wb_ref, o_ref):
    o_ref[...] = jnp.concatenate([wa_ref[...], wb_ref[...]], axis=1).astype(o_ref.dtype)


def _slab_to_bf16(w, slab, rows):
    _, r, c = w.shape
    return pl.pallas_call(
        _cast_kernel,
        grid=(r // rows,),
        in_specs=[pl.BlockSpec((None, rows, c // 2), lambda i: (slab, i, 0)),
                  pl.BlockSpec((None, rows, c // 2), lambda i: (slab, i, 1))],
        out_specs=pl.BlockSpec((rows, c), lambda i: (i, 0)),
        out_shape=jax.ShapeDtypeStruct((r, c), _BF16),
        compiler_params=pltpu.CompilerParams(
            dimension_semantics=("arbitrary",), vmem_limit_bytes=VMEM_LIMIT_BYTES),
        name="to_bf16",
    )(w, w)


def _ada_kernel(cp_ref, cs_ref, wa_ref, wb_ref, b_ref, mp_ref, ms_ref):
    bp = cp_ref.shape[0]
    kh = wa_ref.shape[0]
    c = jnp.concatenate([cp_ref[...], cs_ref[...]], axis=0)
    s = (c * jax.nn.sigmoid(c)).astype(_BF16)
    o = _dot(s[:, :kh], wa_ref[...].astype(_BF16)) + _dot(s[:, kh:], wb_ref[...].astype(_BF16)) + b_ref[...]
    mp_ref[...] = o[:bp]
    ms_ref[...] = o[bp:]


def _ada_mod(c_prompt, c_sample, ada_w, ada_b):
    depth, d, n = ada_w.shape
    tn = ADA_TILE_N
    bp, bs = c_prompt.shape[0], c_sample.shape[0]
    return pl.pallas_call(
        _ada_kernel,
        grid=(depth, n // tn),
        in_specs=[
            pl.BlockSpec((bp, d), lambda l, j: (0, 0)),
            pl.BlockSpec((bs, d), lambda l, j: (0, 0)),
            pl.BlockSpec((None, d // 2, tn), lambda l, j: (l, 0, j)),
            pl.BlockSpec((None, d // 2, tn), lambda l, j: (l, 1, j)),
            pl.BlockSpec((None, 1, tn), lambda l, j: (l, 0, j)),
        ],
        out_specs=[
            pl.BlockSpec((None, bp, tn), lambda l, j: (l, 0, j)),
            pl.BlockSpec((None, bs, tn), lambda l, j: (l, 0, j)),
        ],
        out_shape=[
            jax.ShapeDtypeStruct((depth, bp, n), _F32),
            jax.ShapeDtypeStruct((depth, bs, n), _F32),
        ],
        compiler_params=pltpu.CompilerParams(
            dimension_semantics=("arbitrary", "arbitrary"),
            vmem_limit_bytes=VMEM_LIMIT_BYTES),
        name="ada_mod",
    )(c_prompt, c_sample, ada_w, ada_w, ada_b.reshape(depth, 1, n))


def _pool_mix(h, st, pw_ref, ps_ref, i, *, bm, pos_base):
    t, d = h.shape
    tt = t // bm
    gc = d // len(POOL_WINDOWS)
    ext = jnp.concatenate([st, h], axis=0)
    pos = None
    head = min(t, max(POOL_WINDOWS) * bm)
    if pos_base + 1 < max(POOL_WINDOWS):
        row = lax.broadcasted_iota(jnp.int32, (head, V7X_LANES), 0)
        pos = pos_base + i * tt + lax.shift_right_logical(row, bm.bit_length() - 1)
    parts = []
    for gi, w in enumerate(POOL_WINDOWS):
        cols = slice(gi * gc, (gi + 1) * gc)
        s = ext[:, cols]
        span = 1
        while span < w:
            n = s.shape[0]
            s = s[span * bm:] + s[:n - span * bm]
            span *= 2
        k0 = (POOL_STATE - (w - 1)) * bm
        wsum = s[k0:k0 + t]
        mean = wsum * (1.0 / w)
        if pos is not None:
            inv = 1.0 / jnp.minimum(pos + 1, w).astype(_F32)
            inv = jnp.concatenate([inv] * (gc // V7X_LANES), axis=1)
            mean = jnp.concatenate([wsum[:head] * inv, mean[head:]], axis=0)
        pooled = mean - h[:, cols]
        parts.append(_dot(pooled.astype(_BF16), pw_ref[gi].astype(_BF16)))
    return jnp.concatenate(parts, axis=1) * ps_ref[...], ext[t:, :]


def _conv_ffn(f, cv_ref, wup_ref, cw_ref, cb_ref, wdn_ref, h_ref, *, bm):
    t = f.shape[0]
    hid = wdn_ref.shape[0]
    hc = V7X_MXU_DIM

    def conv_cols(c0):
        cols = slice(c0, c0 + hc)
        up = _dot(f, wup_ref[:, cols])
        ext = jnp.concatenate([cv_ref[:, cols], up], axis=0)
        cv_ref[:, cols] = ext[t:, :]
        conv = cb_ref[:, cols]
        for k in range(CONV_W):
            conv = conv + ext[k * bm:k * bm + t, :] * cw_ref[k:k + 1, cols]
        return conv

    for c in range(hid // hc):
        gate_c = conv_cols(c * hc)
        val_c = conv_cols(hid + c * hc)
        h_ref[:, c * hc:(c + 1) * hc] = (_gelu(gate_c) * val_c).astype(_BF16)
    return _dot(h_ref[...], wdn_ref[...].astype(_BF16))


def _layer_kernel(*refs, layer, bm, pos_base, with_pool, with_ffn, in_batch_major, out_batch_major,
                  state_batch_major, cast_next, pool_zero, ffn_zero):
    refs = list(refs)
    x_ref, mod_ref = refs[:2]
    del refs[:2]
    if with_pool:
        st0_ref = None if pool_zero else refs.pop(0)
        mpre_ref, mpost_ref, pw_ref, ps_ref = refs[:4]
        del refs[:4]
    if with_ffn:
        cv0_ref = None if ffn_zero else refs.pop(0)
        fpre_ref, fpost_ref, wup_ref, cw_ref, cb_ref, wdn_ref = refs[:6]
        del refs[:6]
        if cast_next:
            wnext_ref = refs.pop(0)
    y_ref = refs.pop(0)
    if with_pool:
        st_ref = refs.pop(0)
    if with_ffn:
        cv_ref = refs.pop(0)
        if cast_next:
            wnext_out_ref = refs.pop(0)
            wnext_out_ref[...] = wnext_ref[...].astype(wnext_out_ref.dtype)
        (h_ref,) = refs
    i = pl.program_id(0)
    d = mod_ref.shape[1] // 6
    lrow = slice(layer, layer + 1)

    @pl.when(i == 0)
    def _():
        if with_pool and not state_batch_major:
            st_ref[...] = jnp.zeros_like(st_ref) if pool_zero else st0_ref[...]
        if with_ffn:
            cv_ref[...] = jnp.zeros_like(cv_ref) if ffn_zero else cv0_ref[...]

    x = _load_rows(x_ref, in_batch_major)
    if with_pool:
        shift, scale, gate = _mod3(mod_ref, 0, d)
        h = _pre_mod(x, mpre_ref[lrow, :], scale, shift, bm)
        st = _load_rows(st0_ref, True) if state_batch_major else st_ref[...]
        m, st = _pool_mix(h, st, pw_ref, ps_ref, i, bm=bm, pos_base=pos_base)
        _store_rows(st_ref, st, state_batch_major)
        x = _gated_residual(x, m, mpost_ref[lrow, :], gate, bm)
    if with_ffn:
        shift, scale, gate = _mod3(mod_ref, 1, d)
        f = _pre_mod(x, fpre_ref[lrow, :], scale, shift, bm).astype(_BF16)
        o = _conv_ffn(f, cv_ref, wup_ref, cw_ref, cb_ref.at[lrow, :], wdn_ref, h_ref, bm=bm)
        x = _gated_residual(x, o, fpost_ref[lrow, :], gate, bm)
    _store_rows(y_ref, x, out_batch_major)


def _token_layer(x, mod, gains, *, layer, bm, tt, pos_base=0, pool=None, ffn=None, cast_next=None,
                 in_batch_major=False, out_batch_major=False, state_batch_major=False):
    mix_pre_g, mix_post_g, ffn_pre_g, ffn_post_g = gains
    d = x.shape[-1]
    nrows = x.shape[0] * x.shape[1] if in_batch_major else x.shape[0]
    t = tt * bm
    assert not state_batch_major or nrows == t
    tm_spec = pl.BlockSpec((t, d), lambda i: (i, 0))
    bm_spec = pl.BlockSpec((bm, tt, d), lambda i: (0, i, 0))

    args = [x, mod]
    in_specs = [bm_spec if in_batch_major else tm_spec, _layer_slab(mod.shape, layer)]
    out_specs = [bm_spec if out_batch_major else tm_spec]
    out_shape = [jax.ShapeDtypeStruct((bm, nrows // bm, d) if out_batch_major else (nrows, d), _F32)]
    scratch = []
    if pool is not None:
        st0, st_slab, pool_w, pool_scale = pool
        if st0 is None:
            st_shape = (POOL_STATE * bm, d)
        else:
            st_shape = st0.shape if st_slab is None else st0.shape[1:]
            args += [st0]
            in_specs += [_whole(st0.shape) if st_slab is None else _layer_slab(st0.shape, st_slab)]
        args += [mix_pre_g, mix_post_g, pool_w, pool_scale]
        in_specs += [_whole(a.shape) for a in args[-4:]]
        out_specs += [_whole(st_shape)]
        out_shape += [jax.ShapeDtypeStruct(st_shape, _F32)]
    if ffn is not None:
        cv0, w_up, conv_w, conv_b, w_down = ffn
        hid = w_down.shape[1]
        cv_shape = ((CONV_W - 1) * bm, 2 * hid)
        assert hid % V7X_MXU_DIM == 0 and (cv0 is None or cv0.shape == cv_shape)
        if cv0 is not None:
            args += [cv0]
            in_specs += [_whole(cv_shape)]
        args += [ffn_pre_g, ffn_post_g, w_up, conv_w, conv_b, w_down]
        in_specs += [_whole(ffn_pre_g.shape), _whole(ffn_post_g.shape), _whole(w_up.shape),
                     _layer_slab(conv_w.shape, layer), _whole(conv_b.shape), _layer_slab(w_down.shape, layer)]
        out_specs += [_whole(cv_shape)]
        out_shape += [jax.ShapeDtypeStruct(cv_shape, _F32)]
        if cast_next is not None:
            w_stack, slab = cast_next
            _, r_w, c_w = w_stack.shape
            rows = r_w // (nrows // t)
            assert rows * (nrows // t) == r_w and rows % (2 * V7X_SUBLANES) == 0
            args += [w_stack]
            in_specs += [pl.BlockSpec((None, rows, c_w), lambda i: (slab, i, 0))]
            out_specs += [pl.BlockSpec((rows, c_w), lambda i: (i, 0))]
            out_shape += [jax.ShapeDtypeStruct((r_w, c_w), _BF16)]
        scratch += [pltpu.VMEM((t, hid), _BF16)]
    return pl.pallas_call(
        functools.partial(_layer_kernel, layer=layer, bm=bm, pos_base=pos_base,
                          with_pool=pool is not None, with_ffn=ffn is not None,
                          in_batch_major=in_batch_major, out_batch_major=out_batch_major,
                          state_batch_major=state_batch_major, cast_next=cast_next is not None,
                          pool_zero=pool is not None and pool[0] is None,
                          ffn_zero=ffn is not None and ffn[0] is None),
        grid=(nrows // t,),
        in_specs=in_specs,
        out_specs=out_specs,
        out_shape=out_shape,
        scratch_shapes=scratch,
        compiler_params=pltpu.CompilerParams(
            dimension_semantics=("arbitrary",), vmem_limit_bytes=VMEM_LIMIT_BYTES),
        name="_".join(n for n, on in (("pool", pool), ("ffn", ffn)) if on is not None) + "_layer",
    )(*args)


def _s5_prep_kernel(are_ref, aim_ref, ldt_ref, btr_ref, bti_ref, ctr_ref, cti_ref,
                    lbr_ref, lbi_ref, wb_ref, wc_ref):
    a_re, a_im = are_ref[...], aim_ref[...]
    dt = jnp.exp(ldt_ref[...])
    mag = jnp.exp(a_re * dt)
    ang = a_im * dt
    lb_re = mag * jnp.cos(ang)
    lb_im = mag * jnp.sin(ang)
    n_re = lb_re - 1.0
    n_im = lb_im
    den = a_re * a_re + a_im * a_im
    f_re = ((n_re * a_re + n_im * a_im) / den)[:, None, :]
    f_im = ((n_im * a_re - n_re * a_im) / den)[:, None, :]
    lbr_ref[...] = lb_re
    lbi_ref[...] = lb_im
    b_re, b_im = btr_ref[...], bti_ref[...]
    bb_re = (f_re * b_re - f_im * b_im).astype(wb_ref.dtype)
    bb_im = (f_re * b_im + f_im * b_re).astype(wb_ref.dtype)
    c_re = ctr_ref[...].astype(wc_ref.dtype)
    nc_im = (-cti_ref[...]).astype(wc_ref.dtype)
    wb_ref[...] = jnp.zeros(wb_ref.shape, wb_ref.dtype)
    wc_ref[...] = jnp.zeros(wc_ref.shape, wc_ref.dtype)
    g, gc, p = b_re.shape
    gpt = wb_ref.shape[1] // gc
    half = gpt * p
    for gi in range(g):
        j, gl = divmod(gi, gpt)
        rows, cols = slice(gl * gc, (gl + 1) * gc), slice(gl * p, (gl + 1) * p)
        icols = slice(half + gl * p, half + (gl + 1) * p)
        wb_ref[j, rows, cols] = bb_re[gi]
        wb_ref[j, rows, icols] = bb_im[gi]
        wc_ref[j, cols, rows] = c_re[gi]
        wc_ref[j, icols, rows] = nc_im[gi]


def _s5_prep(a_re, a_im, log_dt, b_re, b_im, c_re, c_im):
    g, p = a_re.shape
    gc = b_re.shape[-1]
    gpt = V7X_MXU_DIM // gc
    tr = lambda a: jnp.swapaxes(a, 1, 2)
    return pl.pallas_call(
        _s5_prep_kernel,
        out_shape=[
            jax.ShapeDtypeStruct((g, p), _F32),
            jax.ShapeDtypeStruct((g, p), _F32),
            jax.ShapeDtypeStruct((g // gpt, gpt * gc, 2 * gpt * p), _BF16),
            jax.ShapeDtypeStruct((g // gpt, 2 * gpt * p, gpt * gc), _BF16),
        ],
        compiler_params=pltpu.CompilerParams(vmem_limit_bytes=VMEM_LIMIT_BYTES),
        name="s5_prep",
    )(a_re, a_im, log_dt.reshape(g, 1), tr(b_re), tr(b_im), tr(c_re), tr(c_im))


def _s5_kernel(*refs, layer, bm, zero_state):
    refs = list(refs)
    x_ref, mod_ref = refs[:2]
    del refs[:2]
    if not zero_state:
        xr0_ref, xi0_ref = refs[:2]
        del refs[:2]
    gpre_ref, gpost_ref, wb_ref, wc_ref, lbr_ref, lbi_ref, dsk_ref, ga_ref, gb_ref, y_ref, xr_ref, xi_ref = refs
    i = pl.program_id(0)
    t, d = x_ref.shape
    tt = t // bm
    ntile, kin, ncol2 = wb_ref.shape
    ncol = ncol2 // 2
    sub = V7X_SUBLANES
    lrow = slice(layer, layer + 1)

    @pl.when(i == 0)
    def _():
        xr_ref[...] = jnp.zeros_like(xr_ref) if zero_state else xr0_ref[...]
        xi_ref[...] = jnp.zeros_like(xi_ref) if zero_state else xi0_ref[...]

    x = x_ref[...]
    shift, scale, gate = _mod3(mod_ref, 0, d)
    h = _pre_mod(x, gpre_ref[lrow, :], scale, shift, bm)
    u = h.astype(_BF16)

    ys = []
    for j in range(ntile):
        scol = slice(j * ncol, (j + 1) * ncol)
        bu = _dot(u[:, j * kin:(j + 1) * kin], wb_ref[j])
        lr = jnp.broadcast_to(lbr_ref[:, scol], (sub, ncol))
        li = jnp.broadcast_to(lbi_ref[:, scol], (sub, ncol))
        blocks = [None] * (t // sub)
        for rb in range(bm // sub):
            rows = slice(rb * sub, (rb + 1) * sub)
            pr, pi = xr_ref[rows, scol], xi_ref[rows, scol]
            for ts in range(tt):
                r0 = ts * bm + rb * sub
                nr = lr * pr - li * pi + bu[r0:r0 + sub, :ncol]
                ni = lr * pi + li * pr + bu[r0:r0 + sub, ncol:]
                blocks[r0 // sub] = jnp.concatenate([nr, ni], axis=1)
                pr, pi = nr, ni
            xr_ref[rows, scol] = pr
            xi_ref[rows, scol] = pi
        ys.append(_dot(jnp.concatenate(blocks, axis=0).astype(_BF16), wc_ref[j]))
    y = jnp.concatenate(ys, axis=1) + dsk_ref[...] * h
    g = _gelu(y).astype(_BF16)
    out = _dot(g, ga_ref[...].astype(_BF16)) * jax.nn.sigmoid(_dot(g, gb_ref[...].astype(_BF16)))
    y_ref[...] = _gated_residual(x, out, gpost_ref[lrow, :], gate, bm)


def _s5_layer(x, mod, state0, g_pre, g_post, wb, wc, lbr, lbi, dskip, glu_a, glu_b, *, layer, bm, tt):
    r, d = x.shape
    t = tt * bm
    nst = lbr.shape[1]
    state_args = [] if state0 is None else list(state0)
    return pl.pallas_call(
        functools.partial(_s5_kernel, layer=layer, bm=bm, zero_state=state0 is None),
        grid=(r // t,),
        in_specs=[
            pl.BlockSpec((t, d), lambda i: (i, 0)),
            _layer_slab(mod.shape, layer),
            *[_whole((bm, nst)) for _ in state_args],
            _whole(g_pre.shape),
            _whole(g_post.shape),
            _whole(wb.shape),
            _whole(wc.shape),
            _whole((1, nst)),
            _whole((1, nst)),
            _whole((1, d)),
            _whole(glu_a.shape),
            _whole(glu_b.shape),
        ],
        out_specs=[
            pl.BlockSpec((t, d), lambda i: (i, 0)),
            _whole((bm, nst)),
            _whole((bm, nst)),
        ],
        out_shape=[
            jax.ShapeDtypeStruct((r, d), _F32),
            jax.ShapeDtypeStruct((bm, nst), _F32),
            jax.ShapeDtypeStruct((bm, nst), _F32),
        ],
        compiler_params=pltpu.CompilerParams(
            dimension_semantics=("arbitrary",), vmem_limit_bytes=VMEM_LIMIT_BYTES),
        name="s5_layer",
    )(x, mod, *state_args, g_pre, g_post, wb, wc, lbr, lbi, dskip, glu_a, glu_b)


def _time_major(a):
    b, l, c = a.shape
    return jnp.swapaxes(a, 0, 1).reshape(l * b, c)


def _batch_major(a, b):
    lb, c = a.shape
    return jnp.swapaxes(a.reshape(lb // b, b, c), 0, 1)


def _stacked_batch_major(states, b):
    st = jnp.stack(states)
    n, tb, c = st.shape
    return jnp.swapaxes(st.reshape(n, tb // b, b, c), 1, 2)


def kernel(x_prompt, x_sample, c_prompt, c_sample, state_pool, state_ssm_re, state_ssm_im, state_ffn_conv, ada_w, ada_b, mix_pre_g, mix_post_g, ffn_pre_g, ffn_post_g, pool_w, pool_scale, ssm_A_re, ssm_A_im, ssm_log_dt, ssm_B_re, ssm_B_im, ssm_C_re, ssm_C_im, ssm_D, ssm_glu_a, ssm_glu_b, ffn_w_up, ffn_conv_w, ffn_conv_b, ffn_w_down):
    depth = ada_w.shape[0]
    bp, lp, d = x_prompt.shape
    bs, ls, _ = x_sample.shape
    groups = ssm_A_re.shape[1]
    nst = groups * SSM_P

    mod_p, mod_s = _ada_mod(c_prompt, c_sample, ada_w, ada_b)
    gains = (mix_pre_g, mix_post_g, ffn_pre_g, ffn_post_g)
    w_dn = ffn_w_down
    w_up_next = _slab_to_bf16(ffn_w_up, 0, CAST_TILE_ROWS)

    geo = {
        "p": dict(bm=bp, pool_ffn_tt=PROMPT_POOL_FFN_TT, ffn_tt=PROMPT_FFN_TT, s5_tt=PROMPT_S5_TT, pos_base=0,
                  fuse_pool=True),
        "s": dict(bm=bs, pool_ffn_tt=ls // 2, ffn_tt=ls // 2, s5_tt=ls // 2, pos_base=PAST_LEN, fuse_pool=False,
                  pool_tt=ls),
    }
    ys = {"p": x_prompt, "s": x_sample}
    mods = {"p": mod_p, "s": mod_s}
    pool_out = {"p": [], "s": []}
    ssm_out = {"p": [], "s": []}
    conv_out = {"p": [], "s": []}

    for l in range(depth):
        j = l // 2
        w_up = w_up_next
        pool = {"p": None, "s": None}
        if l % 2 == 0:
            pw = pool_w[j]
            ps = pool_scale[j].reshape(1, d)
            pool["p"] = (None, None, pw, ps)
            pool["s"] = (state_pool, j, pw, ps)
        else:
            lb_re, lb_im, wb, wc = _s5_prep(ssm_A_re[j], ssm_A_im[j], ssm_log_dt[j], ssm_B_re[j], ssm_B_im[j],
                                            ssm_C_re[j], ssm_C_im[j])
            lbr = lb_re.reshape(1, nst)
            lbi = lb_im.reshape(1, nst)
            ga = ssm_glu_a[j]
            gb = ssm_glu_b[j]
            for k in ("p", "s"):
                gk = geo[k]
                state0 = None if k == "p" else (state_ssm_re[j].reshape(bs, nst), state_ssm_im[j].reshape(bs, nst))
                ys[k], xr, xi = _s5_layer(ys[k], mods[k], state0, mix_pre_g, mix_post_g,
                                          wb, wc, lbr, lbi, ssm_D[j].reshape(1, d), ga, gb,
                                          layer=l, bm=gk["bm"], tt=gk["s5_tt"])
                ssm_out[k].append((xr.reshape(gk["bm"], groups, SSM_P), xi.reshape(gk["bm"], groups, SSM_P)))

        for k in ("p", "s"):
            gk = geo[k]
            cv0 = None if k == "p" else _time_major(state_ffn_conv[l])
            ffn = (cv0, w_up, ffn_conv_w, ffn_conv_b, w_dn)
            cast_next = (ffn_w_up, l + 1) if k == "p" and l + 1 < depth else None
            common = dict(layer=l, bm=gk["bm"], pos_base=gk["pos_base"])
            if pool[k] is not None and not gk["fuse_pool"]:
                ys[k], st = _token_layer(ys[k], mods[k], gains, tt=gk["pool_tt"], pool=pool[k],
                                         in_batch_major=True, state_batch_major=True, **common)
                pool_out[k].append(st)
                pool[k] = None
            tt = gk["ffn_tt"] if pool[k] is None else gk["pool_ffn_tt"]
            outs = list(_token_layer(ys[k], mods[k], gains, tt=tt, pool=pool[k], ffn=ffn, cast_next=cast_next,
                                in_batch_major=(l == 0 and pool[k] is not None),
                                out_batch_major=(k == "p" and l == depth - 1), **common))
            if cast_next is not None:
                w_up_next = outs.pop()
            ys[k], cv = outs[0], outs[-1]
            if pool[k] is not None:
                pool_out[k].append(_batch_major(outs[1], gk["bm"]))
            conv_out[k].append(cv)

    y_prompt = ys["p"]
    y_sample = _batch_major(ys["s"], bs)
    return (y_prompt, y_sample,
            jnp.stack(pool_out["p"]), jnp.stack(pool_out["s"]),
            jnp.stack([a for a, _ in ssm_out["p"]]), jnp.stack([b for _, b in ssm_out["p"]]),
            jnp.stack([a for a, _ in ssm_out["s"]]), jnp.stack([b for _, b in ssm_out["s"]]),
            _stacked_batch_major(conv_out["p"], bp), _stacked_batch_major(conv_out["s"], bs))
```

```python
import functools
import math

import jax
import jax.numpy as jnp
from jax import lax
from jax.experimental import pallas as pl
from jax.experimental.pallas import tpu as pltpu

POOL_WINDOWS = (2, 4, 8, 16)
POOL_STATE = max(POOL_WINDOWS) - 1
SSM_GC = 16
SSM_P = 64
CONV_W = 3
EPS = 1e-6
PAST_LEN = 16384

V7X_SUBLANES = 8
V7X_LANES = 128
V7X_MXU_DIM = 256
VMEM_LIMIT_BYTES = 56 * 1024 * 1024

PROMPT_POOL_FFN_TT = 64
PROMPT_FFN_TT = 128
PROMPT_S5_TT = 128
ADA_TILE_N = 1536
CAST_TILE_ROWS = 256

_F32 = jnp.float32
_BF16 = jnp.bfloat16


def _resident(block_shape, index_map):
    return pl.BlockSpec(block_shape, index_map, pipeline_mode=pl.Buffered(1))


def _whole(shape):
    nd = len(shape)
    return _resident(shape, lambda i: (0,) * nd)


def _layer_slab(shape, l):
    nd = len(shape)
    return _resident((None,) + tuple(shape[1:]), lambda i: (l,) + (0,) * (nd - 1))


def _unit_rms(x):
    return x * lax.rsqrt(jnp.mean(x * x, axis=-1, keepdims=True) + EPS)


def _pre_mod(x, g, scale, shift, bm):
    t, d = x.shape
    r = _unit_rms(x).reshape(t // bm, bm, d)
    return (r * (g * (1.0 + scale))[None] + shift[None]).reshape(t, d)


def _gated_residual(x, m, g, gate, bm):
    t, d = x.shape
    r = _unit_rms(m).reshape(t // bm, bm, d)
    return x + (r * (gate * g)[None]).reshape(t, d)


def _gelu(x):
    return 0.5 * x * (1.0 + lax.erf(x * math.sqrt(0.5)))


def _dot(a, b):
    return jnp.dot(a, b, preferred_element_type=_F32)


def _load_rows(x_ref, batch_major):
    if not batch_major:
        return x_ref[...]
    b, tt, d = x_ref.shape
    return jnp.swapaxes(x_ref[...], 0, 1).reshape(tt * b, d)


def _store_rows(y_ref, y, batch_major):
    if not batch_major:
        y_ref[...] = y
    else:
        b, tt, d = y_ref.shape
        y_ref[...] = jnp.swapaxes(y.reshape(tt, b, d), 0, 1)


def _mod3(mod_ref, k, d):
    return tuple(mod_ref[:, (3 * k + n) * d:(3 * k + n + 1) * d] for n in range(3))


def _paired_cols(gate, val):
    hc = V7X_MXU_DIM
    parts = []
    for c in range(gate.shape[1] // hc):
        parts += [gate[:, c * hc:(c + 1) * hc], val[:, c * hc:(c + 1) * hc]]
    return jnp.concatenate(parts, axis=1)


def _cast_kernel(wa_ref, wb_ref, o_ref):
    o_ref[...] = _paired_cols(wa_ref[...], wb_ref[...]).astype(o_ref.dtype)


def _slab_to_bf16(w, slab, rows):
    _, r, c = w.shape
    return pl.pallas_call(
        _cast_kernel,
        grid=(r // rows,),
        in_specs=[pl.BlockSpec((None, rows, c // 2), lambda i: (slab, i, 0)),
                  pl.BlockSpec((None, rows, c // 2), lambda i: (slab, i, 1))],
        out_specs=pl.BlockSpec((rows, c), lambda i: (i, 0)),
        out_shape=jax.ShapeDtypeStruct((r, c), _BF16),
        compiler_params=pltpu.CompilerParams(
            dimension_semantics=("arbitrary",), vmem_limit_bytes=VMEM_LIMIT_BYTES),
        name="to_bf16",
    )(w, w)


def _ada_kernel(cp_ref, cs_ref, wa_ref, wb_ref, b_ref, mp_ref, ms_ref):
    bp = cp_ref.shape[0]
    kh = wa_ref.shape[0]
    c = jnp.concatenate([cp_ref[...], cs_ref[...]], axis=0)
    s = (c * jax.nn.sigmoid(c)).astype(_BF16)
    o = _dot(s[:, :kh], wa_ref[...].astype(_BF16)) + _dot(s[:, kh:], wb_ref[...].astype(_BF16)) + b_ref[...]
    mp_ref[...] = o[:bp]
    ms_ref[...] = o[bp:]


def _ada_mod(c_prompt, c_sample, ada_w, ada_b):
    depth, d, n = ada_w.shape
    tn = ADA_TILE_N
    bp, bs = c_prompt.shape[0], c_sample.shape[0]
    return pl.pallas_call(
        _ada_kernel,
        grid=(depth, n // tn),
        in_specs=[
            pl.BlockSpec((bp, d), lambda l, j: (0, 0)),
            pl.BlockSpec((bs, d), lambda l, j: (0, 0)),
            pl.BlockSpec((None, d // 2, tn), lambda l, j: (l, 0, j)),
            pl.BlockSpec((None, d // 2, tn), lambda l, j: (l, 1, j)),
            pl.BlockSpec((None, 1, tn), lambda l, j: (l, 0, j)),
        ],
        out_specs=[
            pl.BlockSpec((None, bp, tn), lambda l, j: (l, 0, j)),
            pl.BlockSpec((None, bs, tn), lambda l, j: (l, 0, j)),
        ],
        out_shape=[
            jax.ShapeDtypeStruct((depth, bp, n), _F32),
            jax.ShapeDtypeStruct((depth, bs, n), _F32),
        ],
        compiler_params=pltpu.CompilerParams(
            dimension_semantics=("arbitrary", "arbitrary"),
            vmem_limit_bytes=VMEM_LIMIT_BYTES),
        name="ada_mod",
    )(c_prompt, c_sample, ada_w, ada_w, ada_b.reshape(depth, 1, n))


def _pool_mix(h, st, pw_ref, ps_ref, i, *, bm, pos_base):
    t, d = h.shape
    tt = t // bm
    gc = d // len(POOL_WINDOWS)
    ext = jnp.concatenate([st, h], axis=0)
    pos = None
    head = min(t, max(POOL_WINDOWS) * bm)
    if pos_base + 1 < max(POOL_WINDOWS):
        row = lax.broadcasted_iota(jnp.int32, (head, V7X_LANES), 0)
        pos = pos_base + i * tt + lax.shift_right_logical(row, bm.bit_length() - 1)
    parts = []
    for gi, w in enumerate(POOL_WINDOWS):
        cols = slice(gi * gc, (gi + 1) * gc)
        s = ext[:, cols]
        span = 1
        while span < w:
            n = s.shape[0]
            s = s[span * bm:] + s[:n - span * bm]
            span *= 2
        k0 = (POOL_STATE - (w - 1)) * bm
        wsum = s[k0:k0 + t]
        mean = wsum * (1.0 / w)
        if pos is not None:
            inv = 1.0 / jnp.minimum(pos + 1, w).astype(_F32)
            inv = jnp.concatenate([inv] * (gc // V7X_LANES), axis=1)
            mean = jnp.concatenate([wsum[:head] * inv, mean[head:]], axis=0)
        pooled = mean - h[:, cols]
        parts.append(_dot(pooled.astype(_BF16), pw_ref[gi].astype(_BF16)))
    return jnp.concatenate(parts, axis=1) * ps_ref[...], ext[t:, :]


def _conv_ffn(f, cv_ref, wup_ref, cw_ref, cb_ref, wdn_ref, h_ref, *, bm):
    t = f.shape[0]
    hid = wdn_ref.shape[0]
    hc = V7X_MXU_DIM

    def conv_cols(up, c0):
        cols = slice(c0, c0 + hc)
        ext = jnp.concatenate([cv_ref[:, cols], up], axis=0)
        cv_ref[:, cols] = ext[t:, :]
        conv = cb_ref[:, cols]
        for k in range(CONV_W):
            conv = conv + ext[k * bm:k * bm + t, :] * cw_ref[k:k + 1, cols]
        return conv

    for c in range(hid // hc):
        up2 = _dot(f, wup_ref[:, 2 * c * hc:2 * (c + 1) * hc])
        gate_c = conv_cols(up2[:, :hc], c * hc)
        val_c = conv_cols(up2[:, hc:], hid + c * hc)
        h_ref[:, c * hc:(c + 1) * hc] = (_gelu(gate_c) * val_c).astype(_BF16)
    return _dot(h_ref[...], wdn_ref[...].astype(_BF16))


def _layer_kernel(*refs, layer, bm, pos_base, with_pool, with_ffn, in_batch_major, out_batch_major,
                  state_batch_major, cast_next, pool_zero, ffn_zero):
    refs = list(refs)
    x_ref, mod_ref = refs[:2]
    del refs[:2]
    if with_pool:
        st0_ref = None if pool_zero else refs.pop(0)
        mpre_ref, mpost_ref, pw_ref, ps_ref = refs[:4]
        del refs[:4]
    if with_ffn:
        cv0_ref = None if ffn_zero else refs.pop(0)
        fpre_ref, fpost_ref, wup_ref, cw_ref, cb_ref, wdn_ref = refs[:6]
        del refs[:6]
        if cast_next:
            wnext_ref = refs.pop(0)
    y_ref = refs.pop(0)
    if with_pool:
        st_ref = refs.pop(0)
    if with_ffn:
        cv_ref = refs.pop(0)
        if cast_next:
            wnext_out_ref = refs.pop(0)
            nh = wnext_ref.shape[1] // 2
            wnext_out_ref[...] = _paired_cols(wnext_ref[:, :nh], wnext_ref[:, nh:]).astype(wnext_out_ref.dtype)
        (h_ref,) = refs
    i = pl.program_id(0)
    d = mod_ref.shape[1] // 6
    lrow = slice(layer, layer + 1)

    @pl.when(i == 0)
    def _():
        if with_pool and not state_batch_major:
            st_ref[...] = jnp.zeros_like(st_ref) if pool_zero else st0_ref[...]
        if with_ffn:
            cv_ref[...] = jnp.zeros_like(cv_ref) if ffn_zero else cv0_ref[...]

    x = _load_rows(x_ref, in_batch_major)
    if with_pool:
        shift, scale, gate = _mod3(mod_ref, 0, d)
        h = _pre_mod(x, mpre_ref[lrow, :], scale, shift, bm)
        st = _load_rows(st0_ref, True) if state_batch_major else st_ref[...]
        m, st = _pool_mix(h, st, pw_ref, ps_ref, i, bm=bm, pos_base=pos_base)
        _store_rows(st_ref, st, state_batch_major)
        x = _gated_residual(x, m, mpost_ref[lrow, :], gate, bm)
    if with_ffn:
        shift, scale, gate = _mod3(mod_ref, 1, d)
        f = _pre_mod(x, fpre_ref[lrow, :], scale, shift, bm).astype(_BF16)
        o = _conv_ffn(f, cv_ref, wup_ref, cw_ref, cb_ref.at[lrow, :], wdn_ref, h_ref, bm=bm)
        x = _gated_residual(x, o, fpost_ref[lrow, :], gate, bm)
    _store_rows(y_ref, x, out_batch_major)


def _token_layer(x, mod, gains, *, layer, bm, tt, pos_base=0, pool=None, ffn=None, cast_next=None,
                 in_batch_major=False, out_batch_major=False, state_batch_major=False):
    mix_pre_g, mix_post_g, ffn_pre_g, ffn_post_g = gains
    d = x.shape[-1]
    nrows = x.shape[0] * x.shape[1] if in_batch_major else x.shape[0]
    t = tt * bm
    assert not state_batch_major or nrows == t
    tm_spec = pl.BlockSpec((t, d), lambda i: (i, 0))
    bm_spec = pl.BlockSpec((bm, tt, d), lambda i: (0, i, 0))

    args = [x, mod]
    in_specs = [bm_spec if in_batch_major else tm_spec, _layer_slab(mod.shape, layer)]
    out_specs = [bm_spec if out_batch_major else tm_spec]
    out_shape = [jax.ShapeDtypeStruct((bm, nrows // bm, d) if out_batch_major else (nrows, d), _F32)]
    scratch = []
    if pool is not None:
        st0, st_slab, pool_w, pool_scale = pool
        if st0 is None:
            st_shape = (POOL_STATE * bm, d)
        else:
            st_shape = st0.shape if st_slab is None else st0.shape[1:]
            args += [st0]
            in_specs += [_whole(st0.shape) if st_slab is None else _layer_slab(st0.shape, st_slab)]
        args += [mix_pre_g, mix_post_g, pool_w, pool_scale]
        in_specs += [_whole(a.shape) for a in args[-4:]]
        out_specs += [_whole(st_shape)]
        out_shape += [jax.ShapeDtypeStruct(st_shape, _F32)]
    if ffn is not None:
        cv0, w_up, conv_w, conv_b, w_down = ffn
        hid = w_down.shape[1]
        cv_shape = ((CONV_W - 1) * bm, 2 * hid)
        assert hid % V7X_MXU_DIM == 0 and (cv0 is None or cv0.shape == cv_shape)
        if cv0 is not None:
            args += [cv0]
            in_specs += [_whole(cv_shape)]
        args += [ffn_pre_g, ffn_post_g, w_up, conv_w, conv_b, w_down]
        in_specs += [_whole(ffn_pre_g.shape), _whole(ffn_post_g.shape), _whole(w_up.shape),
                     _layer_slab(conv_w.shape, layer), _whole(conv_b.shape), _layer_slab(w_down.shape, layer)]
        out_specs += [_whole(cv_shape)]
        out_shape += [jax.ShapeDtypeStruct(cv_shape, _F32)]
        if cast_next is not None:
            w_stack, slab = cast_next
            _, r_w, c_w = w_stack.shape
            rows = r_w // (nrows // t)
            assert rows * (nrows // t) == r_w and rows % (2 * V7X_SUBLANES) == 0
            args += [w_stack]
            in_specs += [pl.BlockSpec((None, rows, c_w), lambda i: (slab, i, 0))]
            out_specs += [pl.BlockSpec((rows, c_w), lambda i: (i, 0))]
            out_shape += [jax.ShapeDtypeStruct((r_w, c_w), _BF16)]
        scratch += [pltpu.VMEM((t, hid), _BF16)]
    return pl.pallas_call(
        functools.partial(_layer_kernel, layer=layer, bm=bm, pos_base=pos_base,
                          with_pool=pool is not None, with_ffn=ffn is not None,
                          in_batch_major=in_batch_major, out_batch_major=out_batch_major,
                          state_batch_major=state_batch_major, cast_next=cast_next is not None,
                          pool_zero=pool is not None and pool[0] is None,
                          ffn_zero=ffn is not None and ffn[0] is None),
        grid=(nrows // t,),
        in_specs=in_specs,
        out_specs=out_specs,
        out_shape=out_shape,
        scratch_shapes=scratch,
        compiler_params=pltpu.CompilerParams(
            dimension_semantics=("arbitrary",), vmem_limit_bytes=VMEM_LIMIT_BYTES),
        name="_".join(n for n, on in (("pool", pool), ("ffn", ffn)) if on is not None) + "_layer",
    )(*args)


def _s5_prep_kernel(are_ref, aim_ref, ldt_ref, btr_ref, bti_ref, ctr_ref, cti_ref,
                    lbr_ref, lbi_ref, wb_ref, wc_ref):
    a_re, a_im = are_ref[...], aim_ref[...]
    dt = jnp.exp(ldt_ref[...])
    mag = jnp.exp(a_re * dt)
    ang = a_im * dt
    lb_re = mag * jnp.cos(ang)
    lb_im = mag * jnp.sin(ang)
    n_re = lb_re - 1.0
    n_im = lb_im
    den = a_re * a_re + a_im * a_im
    f_re = ((n_re * a_re + n_im * a_im) / den)[:, None, :]
    f_im = ((n_im * a_re - n_re * a_im) / den)[:, None, :]
    lbr_ref[...] = lb_re
    lbi_ref[...] = lb_im
    b_re, b_im = btr_ref[...], bti_ref[...]
    bb_re = (f_re * b_re - f_im * b_im).astype(wb_ref.dtype)
    bb_im = (f_re * b_im + f_im * b_re).astype(wb_ref.dtype)
    c_re = ctr_ref[...].astype(wc_ref.dtype)
    nc_im = (-cti_ref[...]).astype(wc_ref.dtype)
    wb_ref[...] = jnp.zeros(wb_ref.shape, wb_ref.dtype)
    wc_ref[...] = jnp.zeros(wc_ref.shape, wc_ref.dtype)
    g, gc, p = b_re.shape
    gpt = wb_ref.shape[1] // gc
    half = gpt * p
    for gi in range(g):
        j, gl = divmod(gi, gpt)
        rows, cols = slice(gl * gc, (gl + 1) * gc), slice(gl * p, (gl + 1) * p)
        icols = slice(half + gl * p, half + (gl + 1) * p)
        wb_ref[j, rows, cols] = bb_re[gi]
        wb_ref[j, rows, icols] = bb_im[gi]
        wc_ref[j, cols, rows] = c_re[gi]
        wc_ref[j, icols, rows] = nc_im[gi]


def _s5_prep(a_re, a_im, log_dt, b_re, b_im, c_re, c_im):
    g, p = a_re.shape
    gc = b_re.shape[-1]
    gpt = V7X_MXU_DIM // gc
    tr = lambda a: jnp.swapaxes(a, 1, 2)
    return pl.pallas_call(
        _s5_prep_kernel,
        out_shape=[
            jax.ShapeDtypeStruct((g, p), _F32),
            jax.ShapeDtypeStruct((g, p), _F32),
            jax.ShapeDtypeStruct((g // gpt, gpt * gc, 2 * gpt * p), _BF16),
            jax.ShapeDtypeStruct((g // gpt, 2 * gpt * p, gpt * gc), _BF16),
        ],
        compiler_params=pltpu.CompilerParams(vmem_limit_bytes=VMEM_LIMIT_BYTES),
        name="s5_prep",
    )(a_re, a_im, log_dt.reshape(g, 1), tr(b_re), tr(b_im), tr(c_re), tr(c_im))


def _s5_kernel(*refs, layer, bm, zero_state):
    refs = list(refs)
    x_ref, mod_ref = refs[:2]
    del refs[:2]
    if not zero_state:
        xr0_ref, xi0_ref = refs[:2]
        del refs[:2]
    gpre_ref, gpost_ref, wb_ref, wc_ref, lbr_ref, lbi_ref, dsk_ref, ga_ref, gb_ref, y_ref, xr_ref, xi_ref = refs
    i = pl.program_id(0)
    t, d = x_ref.shape
    tt = t // bm
    ntile, kin, ncol2 = wb_ref.shape
    ncol = ncol2 // 2
    sub = V7X_SUBLANES
    lrow = slice(layer, layer + 1)

    @pl.when(i == 0)
    def _():
        xr_ref[...] = jnp.zeros_like(xr_ref) if zero_state else xr0_ref[...]
        xi_ref[...] = jnp.zeros_like(xi_ref) if zero_state else xi0_ref[...]

    x = x_ref[...]
    shift, scale, gate = _mod3(mod_ref, 0, d)
    h = _pre_mod(x, gpre_ref[lrow, :], scale, shift, bm)
    u = h.astype(_BF16)

    ys = []
    for j in range(ntile):
        scol = slice(j * ncol, (j + 1) * ncol)
        bu = _dot(u[:, j * kin:(j + 1) * kin], wb_ref[j])
        lr = jnp.broadcast_to(lbr_ref[:, scol], (sub, ncol))
        li = jnp.broadcast_to(lbi_ref[:, scol], (sub, ncol))
        blocks = [None] * (t // sub)
        for rb in range(bm // sub):
            rows = slice(rb * sub, (rb + 1) * sub)
            pr, pi = xr_ref[rows, scol], xi_ref[rows, scol]
            for ts in range(tt):
                r0 = ts * bm + rb * sub
                nr = lr * pr - li * pi + bu[r0:r0 + sub, :ncol]
                ni = lr * pi + li * pr + bu[r0:r0 + sub, ncol:]
                blocks[r0 // sub] = jnp.concatenate([nr, ni], axis=1)
                pr, pi = nr, ni
            xr_ref[rows, scol] = pr
            xi_ref[rows, scol] = pi
        ys.append(_dot(jnp.concatenate(blocks, axis=0).astype(_BF16), wc_ref[j]))
    y = jnp.concatenate(ys, axis=1) + dsk_ref[...] * h
    g = _gelu(y).astype(_BF16)
    out = _dot(g, ga_ref[...].astype(_BF16)) * jax.nn.sigmoid(_dot(g, gb_ref[...].astype(_BF16)))
    y_ref[...] = _gated_residual(x, out, gpost_ref[lrow, :], gate, bm)


def _s5_layer(x, mod, state0, g_pre, g_post, wb, wc, lbr, lbi, dskip, glu_a, glu_b, *, layer, bm, tt):
    r, d = x.shape
    t = tt * bm
    nst = lbr.shape[1]
    state_args = [] if state0 is None else list(state0)
    return pl.pallas_call(
        functools.partial(_s5_kernel, layer=layer, bm=bm, zero_state=state0 is None),
        grid=(r // t,),
        in_specs=[
            pl.BlockSpec((t, d), lambda i: (i, 0)),
            _layer_slab(mod.shape, layer),
            *[_whole((bm, nst)) for _ in state_args],
            _whole(g_pre.shape),
            _whole(g_post.shape),
            _whole(wb.shape),
            _whole(wc.shape),
            _whole((1, nst)),
            _whole((1, nst)),
            _whole((1, d)),
            _whole(glu_a.shape),
            _whole(glu_b.shape),
        ],
        out_specs=[
            pl.BlockSpec((t, d), lambda i: (i, 0)),
            _whole((bm, nst)),
            _whole((bm, nst)),
        ],
        out_shape=[
            jax.ShapeDtypeStruct((r, d), _F32),
            jax.ShapeDtypeStruct((bm, nst), _F32),
            jax.ShapeDtypeStruct((bm, nst), _F32),
        ],
        compiler_params=pltpu.CompilerParams(
            dimension_semantics=("arbitrary",), vmem_limit_bytes=VMEM_LIMIT_BYTES),
        name="s5_layer",
    )(x, mod, *state_args, g_pre, g_post, wb, wc, lbr, lbi, dskip, glu_a, glu_b)


def _time_major(a):
    b, l, c = a.shape
    return jnp.swapaxes(a, 0, 1).reshape(l * b, c)


def _batch_major(a, b):
    lb, c = a.shape
    return jnp.swapaxes(a.reshape(lb // b, b, c), 0, 1)


def _stacked_batch_major(states, b):
    st = jnp.stack(states)
    n, tb, c = st.shape
    return jnp.swapaxes(st.reshape(n, tb // b, b, c), 1, 2)


def kernel(x_prompt, x_sample, c_prompt, c_sample, state_pool, state_ssm_re, state_ssm_im, state_ffn_conv, ada_w, ada_b, mix_pre_g, mix_post_g, ffn_pre_g, ffn_post_g, pool_w, pool_scale, ssm_A_re, ssm_A_im, ssm_log_dt, ssm_B_re, ssm_B_im, ssm_C_re, ssm_C_im, ssm_D, ssm_glu_a, ssm_glu_b, ffn_w_up, ffn_conv_w, ffn_conv_b, ffn_w_down):
    depth = ada_w.shape[0]
    bp, lp, d = x_prompt.shape
    bs, ls, _ = x_sample.shape
    groups = ssm_A_re.shape[1]
    nst = groups * SSM_P

    mod_p, mod_s = _ada_mod(c_prompt, c_sample, ada_w, ada_b)
    gains = (mix_pre_g, mix_post_g, ffn_pre_g, ffn_post_g)
    w_dn = ffn_w_down
    w_up_next = _slab_to_bf16(ffn_w_up, 0, CAST_TILE_ROWS)

    geo = {
        "p": dict(bm=bp, pool_ffn_tt=PROMPT_POOL_FFN_TT, ffn_tt=PROMPT_FFN_TT, s5_tt=PROMPT_S5_TT, pos_base=0,
                  fuse_pool=True),
        "s": dict(bm=bs, pool_ffn_tt=ls // 2, ffn_tt=ls // 2, s5_tt=ls // 2, pos_base=PAST_LEN, fuse_pool=False,
                  pool_tt=ls),
    }
    ys = {"p": x_prompt, "s": x_sample}
    mods = {"p": mod_p, "s": mod_s}
    pool_out = {"p": [], "s": []}
    ssm_out = {"p": [], "s": []}
    conv_out = {"p": [], "s": []}

    for l in range(depth):
        j = l // 2
        w_up = w_up_next
        pool = {"p": None, "s": None}
        if l % 2 == 0:
            pw = pool_w[j]
            ps = pool_scale[j].reshape(1, d)
            pool["p"] = (None, None, pw, ps)
            pool["s"] = (state_pool, j, pw, ps)
        else:
            lb_re, lb_im, wb, wc = _s5_prep(ssm_A_re[j], ssm_A_im[j], ssm_log_dt[j], ssm_B_re[j], ssm_B_im[j],
                                            ssm_C_re[j], ssm_C_im[j])
            lbr = lb_re.reshape(1, nst)
            lbi = lb_im.reshape(1, nst)
            ga = ssm_glu_a[j]
            gb = ssm_glu_b[j]
            for k in ("p", "s"):
                gk = geo[k]
                state0 = None if k == "p" else (state_ssm_re[j].reshape(bs, nst), state_ssm_im[j].reshape(bs, nst))
                ys[k], xr, xi = _s5_layer(ys[k], mods[k], state0, mix_pre_g, mix_post_g,
                                          wb, wc, lbr, lbi, ssm_D[j].reshape(1, d), ga, gb,
                                          layer=l, bm=gk["bm"], tt=gk["s5_tt"])
                ssm_out[k].append((xr.reshape(gk["bm"], groups, SSM_P), xi.reshape(gk["bm"], groups, SSM_P)))

        for k in ("p", "s"):
            gk = geo[k]
            cv0 = None if k == "p" else _time_major(state_ffn_conv[l])
            ffn = (cv0, w_up, ffn_conv_w, ffn_conv_b, w_dn)
            cast_next = (ffn_w_up, l + 1) if k == "p" and l + 1 < depth else None
            common = dict(layer=l, bm=gk["bm"], pos_base=gk["pos_base"])
            if pool[k] is not None and not gk["fuse_pool"]:
                ys[k], st = _token_layer(ys[k], mods[k], gains, tt=gk["pool_tt"], pool=pool[k],
                                         in_batch_major=True, state_batch_major=True, **common)
                pool_out[k].append(st)
                pool[k] = None
            tt = gk["ffn_tt"] if pool[k] is None else gk["pool_ffn_tt"]
            outs = list(_token_layer(ys[k], mods[k], gains, tt=tt, pool=pool[k], ffn=ffn, cast_next=cast_next,
                                in_batch_major=(l == 0 and pool[k] is not None),
                                out_batch_major=(k == "p" and l == depth - 1), **common))
            if cast_next is not None:
                w_up_next = outs.pop()
            ys[k], cv = outs[0], outs[-1]
            if pool[k] is not None:
                pool_out[k].append(_batch_major(outs[1], gk["bm"]))
            conv_out[k].append(cv)

    y_prompt = ys["p"]
    y_sample = _batch_major(ys["s"], bs)
    return (y_prompt, y_sample,
            jnp.stack(pool_out["p"]), jnp.stack(pool_out["s"]),
            jnp.stack([a for a, _ in ssm_out["p"]]), jnp.stack([b for _, b in ssm_out["p"]]),
            jnp.stack([a for a, _ in ssm_out["s"]]), jnp.stack([b for _, b in ssm_out["s"]]),
            _stacked_batch_major(conv_out["p"], bp), _stacked_batch_major(conv_out["s"], bs))
```

```python
import functools
import math

import jax
import jax.numpy as jnp
from jax import lax
from jax.experimental import pallas as pl
from jax.experimental.pallas import tpu as pltpu

POOL_WINDOWS = (2, 4, 8, 16)
POOL_STATE = max(POOL_WINDOWS) - 1
SSM_GC = 16
SSM_P = 64
CONV_W = 3
EPS = 1e-6
PAST_LEN = 16384

V7X_SUBLANES = 8
V7X_LANES = 128
V7X_MXU_DIM = 256
VMEM_LIMIT_BYTES = 56 * 1024 * 1024

PROMPT_POOL_FFN_TT = 64
PROMPT_FFN_TT = 128
PROMPT_S5_TT = 128
ADA_TILE_N = 1536
CAST_TILE_ROWS = 256

_F32 = jnp.float32
_BF16 = jnp.bfloat16


def _resident(block_shape, index_map):
    return pl.BlockSpec(block_shape, index_map, pipeline_mode=pl.Buffered(1))


def _whole(shape):
    nd = len(shape)
    return _resident(shape, lambda i: (0,) * nd)


def _layer_slab(shape, l):
    nd = len(shape)
    return _resident((None,) + tuple(shape[1:]), lambda i: (l,) + (0,) * (nd - 1))


def _unit_rms(x):
    return x * lax.rsqrt(jnp.mean(x * x, axis=-1, keepdims=True) + EPS)


def _pre_mod(x, g, scale, shift, bm):
    t, d = x.shape
    r = _unit_rms(x).reshape(t // bm, bm, d)
    return (r * (g * (1.0 + scale))[None] + shift[None]).reshape(t, d)


def _gated_residual(x, m, g, gate, bm):
    t, d = x.shape
    r = _unit_rms(m).reshape(t // bm, bm, d)
    return x + (r * (gate * g)[None]).reshape(t, d)


def _gelu(x):
    return 0.5 * x * (1.0 + lax.erf(x * math.sqrt(0.5)))


def _dot(a, b):
    return jnp.dot(a, b, preferred_element_type=_F32)


def _load_rows(x_ref, batch_major):
    if not batch_major:
        return x_ref[...]
    b, tt, d = x_ref.shape
    return jnp.swapaxes(x_ref[...], 0, 1).reshape(tt * b, d)


def _store_rows(y_ref, y, batch_major):
    if not batch_major:
        y_ref[...] = y
    else:
        b, tt, d = y_ref.shape
        y_ref[...] = jnp.swapaxes(y.reshape(tt, b, d), 0, 1)


def _mod3(mod_ref, k, d):
    return tuple(mod_ref[:, (3 * k + n) * d:(3 * k + n + 1) * d] for n in range(3))


def _paired_cols(gate, val):
    hc = V7X_MXU_DIM
    parts = []
    for c in range(gate.shape[1] // hc):
        parts += [gate[:, c * hc:(c + 1) * hc], val[:, c * hc:(c + 1) * hc]]
    return jnp.concatenate(parts, axis=1)


def _cast_kernel(wa_ref, wb_ref, o_ref):
    o_ref[...] = _paired_cols(wa_ref[...], wb_ref[...]).astype(o_ref.dtype)


def _slab_to_bf16(w, slab, rows):
    _, r, c = w.shape
    return pl.pallas_call(
        _cast_kernel,
        grid=(r // rows,),
        in_specs=[pl.BlockSpec((None, rows, c // 2), lambda i: (slab, i, 0)),
                  pl.BlockSpec((None, rows, c // 2), lambda i: (slab, i, 1))],
        out_specs=pl.BlockSpec((rows, c), lambda i: (i, 0)),
        out_shape=jax.ShapeDtypeStruct((r, c), _BF16),
        compiler_params=pltpu.CompilerParams(
            dimension_semantics=("arbitrary",), vmem_limit_bytes=VMEM_LIMIT_BYTES),
        name="to_bf16",
    )(w, w)


def _ada_kernel(cp_ref, cs_ref, wa_ref, wb_ref, b_ref, mp_ref, ms_ref):
    bp = cp_ref.shape[0]
    kh = wa_ref.shape[0]
    c = jnp.concatenate([cp_ref[...], cs_ref[...]], axis=0)
    s = (c * jax.nn.sigmoid(c)).astype(_BF16)
    o = _dot(s[:, :kh], wa_ref[...].astype(_BF16)) + _dot(s[:, kh:], wb_ref[...].astype(_BF16)) + b_ref[...]
    mp_ref[...] = o[:bp]
    ms_ref[...] = o[bp:]


def _ada_mod(c_prompt, c_sample, ada_w, ada_b):
    depth, d, n = ada_w.shape
    tn = ADA_TILE_N
    bp, bs = c_prompt.shape[0], c_sample.shape[0]
    return pl.pallas_call(
        _ada_kernel,
        grid=(depth, n // tn),
        in_specs=[
            pl.BlockSpec((bp, d), lambda l, j: (0, 0)),
            pl.BlockSpec((bs, d), lambda l, j: (0, 0)),
            pl.BlockSpec((None, d // 2, tn), lambda l, j: (l, 0, j)),
            pl.BlockSpec((None, d // 2, tn), lambda l, j: (l, 1, j)),
            pl.BlockSpec((None, 1, tn), lambda l, j: (l, 0, j)),
        ],
        out_specs=[
            pl.BlockSpec((None, bp, tn), lambda l, j: (l, 0, j)),
            pl.BlockSpec((None, bs, tn), lambda l, j: (l, 0, j)),
        ],
        out_shape=[
            jax.ShapeDtypeStruct((depth, bp, n), _F32),
            jax.ShapeDtypeStruct((depth, bs, n), _F32),
        ],
        compiler_params=pltpu.CompilerParams(
            dimension_semantics=("arbitrary", "arbitrary"),
            vmem_limit_bytes=VMEM_LIMIT_BYTES),
        name="ada_mod",
    )(c_prompt, c_sample, ada_w, ada_w, ada_b.reshape(depth, 1, n))


def _pool_mix(h, st, pw_ref, ps_ref, i, *, bm, pos_base):
    t, d = h.shape
    tt = t // bm
    gc = d // len(POOL_WINDOWS)
    ext = jnp.concatenate([st, h], axis=0)
    pos = None
    head = min(t, max(POOL_WINDOWS) * bm)
    if pos_base + 1 < max(POOL_WINDOWS):
        row = lax.broadcasted_iota(jnp.int32, (head, V7X_LANES), 0)
        pos = pos_base + i * tt + lax.shift_right_logical(row, bm.bit_length() - 1)
    parts = []
    for gi, w in enumerate(POOL_WINDOWS):
        cols = slice(gi * gc, (gi + 1) * gc)
        s = ext[:, cols]
        span = 1
        while span < w:
            n = s.shape[0]
            s = s[span * bm:] + s[:n - span * bm]
            span *= 2
        k0 = (POOL_STATE - (w - 1)) * bm
        wsum = s[k0:k0 + t]
        mean = wsum * (1.0 / w)
        if pos is not None:
            inv = 1.0 / jnp.minimum(pos + 1, w).astype(_F32)
            inv = jnp.concatenate([inv] * (gc // V7X_LANES), axis=1)
            mean = jnp.concatenate([wsum[:head] * inv, mean[head:]], axis=0)
        pooled = mean - h[:, cols]
        parts.append(_dot(pooled.astype(_BF16), pw_ref[gi].astype(_BF16)))
    return jnp.concatenate(parts, axis=1) * ps_ref[...], ext[t:, :]


def _conv_ffn(f, cv_ref, wup_ref, cw_ref, cb_ref, wdn_ref, h_ref, *, bm):
    t = f.shape[0]
    hid = wdn_ref.shape[0]
    hc = V7X_MXU_DIM

    def conv_cols(up, c0):
        cols = slice(c0, c0 + hc)
        ext = jnp.concatenate([cv_ref[:, cols], up], axis=0)
        cv_ref[:, cols] = ext[t:, :]
        conv = cb_ref[:, cols]
        for k in range(CONV_W):
            conv = conv + ext[k * bm:k * bm + t, :] * cw_ref[k:k + 1, cols]
        return conv

    for c in range(hid // hc):
        up2 = _dot(f, wup_ref[:, 2 * c * hc:2 * (c + 1) * hc])
        gate_c = conv_cols(up2[:, :hc], c * hc)
        val_c = conv_cols(up2[:, hc:], hid + c * hc)
        h_ref[:, c * hc:(c + 1) * hc] = (_gelu(gate_c) * val_c).astype(_BF16)
    return _dot(h_ref[...], wdn_ref[...].astype(_BF16))


def _layer_kernel(*refs, layer, bm, pos_base, with_pool, with_ffn, in_batch_major, out_batch_major,
                  state_batch_major, cast_next, pool_zero, ffn_zero):
    refs = list(refs)
    x_ref, mod_ref = refs[:2]
    del refs[:2]
    if with_pool:
        st0_ref = None if pool_zero else refs.pop(0)
        mpre_ref, mpost_ref, pw_ref, ps_ref = refs[:4]
        del refs[:4]
    if with_ffn:
        cv0_ref = None if ffn_zero else refs.pop(0)
        fpre_ref, fpost_ref, wup_ref, cw_ref, cb_ref, wdn_ref = refs[:6]
        del refs[:6]
        if cast_next:
            wnext_ref = refs.pop(0)
    y_ref = refs.pop(0)
    if with_pool:
        st_ref = refs.pop(0)
    if with_ffn:
        cv_ref = refs.pop(0)
        if cast_next:
            wnext_out_ref = refs.pop(0)
            nh = wnext_ref.shape[1] // 2
            wnext_out_ref[...] = _paired_cols(wnext_ref[:, :nh], wnext_ref[:, nh:]).astype(wnext_out_ref.dtype)
        (h_ref,) = refs
    i = pl.program_id(0)
    d = mod_ref.shape[1] // 6
    lrow = slice(layer, layer + 1)

    @pl.when(i == 0)
    def _():
        if with_pool and not state_batch_major:
            st_ref[...] = jnp.zeros_like(st_ref) if pool_zero else st0_ref[...]
        if with_ffn:
            cv_ref[...] = jnp.zeros_like(cv_ref) if ffn_zero else cv0_ref[...]

    x = _load_rows(x_ref, in_batch_major)
    if with_pool:
        shift, scale, gate = _mod3(mod_ref, 0, d)
        h = _pre_mod(x, mpre_ref[lrow, :], scale, shift, bm)
        st = _load_rows(st0_ref, True) if state_batch_major else st_ref[...]
        m, st = _pool_mix(h, st, pw_ref, ps_ref, i, bm=bm, pos_base=pos_base)
        _store_rows(st_ref, st, state_batch_major)
        x = _gated_residual(x, m, mpost_ref[lrow, :], gate, bm)
    if with_ffn:
        shift, scale, gate = _mod3(mod_ref, 1, d)
        f = _pre_mod(x, fpre_ref[lrow, :], scale, shift, bm).astype(_BF16)
        o = _conv_ffn(f, cv_ref, wup_ref, cw_ref, cb_ref.at[lrow, :], wdn_ref, h_ref, bm=bm)
        x = _gated_residual(x, o, fpost_ref[lrow, :], gate, bm)
    _store_rows(y_ref, x, out_batch_major)


def _token_layer(x, mod, gains, *, layer, bm, tt, pos_base=0, pool=None, ffn=None, cast_next=None,
                 in_batch_major=False, out_batch_major=False, state_batch_major=False):
    mix_pre_g, mix_post_g, ffn_pre_g, ffn_post_g = gains
    d = x.shape[-1]
    nrows = x.shape[0] * x.shape[1] if in_batch_major else x.shape[0]
    t = tt * bm
    assert not state_batch_major or nrows == t
    tm_spec = pl.BlockSpec((t, d), lambda i: (i, 0))
    bm_spec = pl.BlockSpec((bm, tt, d), lambda i: (0, i, 0))

    args = [x, mod]
    in_specs = [bm_spec if in_batch_major else tm_spec, _layer_slab(mod.shape, layer)]
    out_specs = [bm_spec if out_batch_major else tm_spec]
    out_shape = [jax.ShapeDtypeStruct((bm, nrows // bm, d) if out_batch_major else (nrows, d), _F32)]
    scratch = []
    if pool is not None:
        st0, st_slab, pool_w, pool_scale = pool
        if st0 is None:
            st_shape = (POOL_STATE * bm, d)
        else:
            st_shape = st0.shape if st_slab is None else st0.shape[1:]
            args += [st0]
            in_specs += [_whole(st0.shape) if st_slab is None else _layer_slab(st0.shape, st_slab)]
        args += [mix_pre_g, mix_post_g, pool_w, pool_scale]
        in_specs += [_whole(a.shape) for a in args[-4:]]
        out_specs += [_whole(st_shape)]
        out_shape += [jax.ShapeDtypeStruct(st_shape, _F32)]
    if ffn is not None:
        cv0, w_up, conv_w, conv_b, w_down = ffn
        hid = w_down.shape[1]
        cv_shape = ((CONV_W - 1) * bm, 2 * hid)
        assert hid % V7X_MXU_DIM == 0 and (cv0 is None or cv0.shape == cv_shape)
        if cv0 is not None:
            args += [cv0]
            in_specs += [_whole(cv_shape)]
        args += [ffn_pre_g, ffn_post_g, w_up, conv_w, conv_b, w_down]
        in_specs += [_whole(ffn_pre_g.shape), _whole(ffn_post_g.shape), _whole(w_up.shape),
                     _layer_slab(conv_w.shape, layer), _whole(conv_b.shape), _layer_slab(w_down.shape, layer)]
        out_specs += [_whole(cv_shape)]
        out_shape += [jax.ShapeDtypeStruct(cv_shape, _F32)]
        if cast_next is not None:
            w_stack, slab = cast_next
            _, r_w, c_w = w_stack.shape
            rows = r_w // (nrows // t)
            assert rows * (nrows // t) == r_w and rows % (2 * V7X_SUBLANES) == 0
            args += [w_stack]
            in_specs += [pl.BlockSpec((None, rows, c_w), lambda i: (slab, i, 0))]
            out_specs += [pl.BlockSpec((rows, c_w), lambda i: (i, 0))]
            out_shape += [jax.ShapeDtypeStruct((r_w, c_w), _BF16)]
        scratch += [pltpu.VMEM((t, hid), _BF16)]
    return pl.pallas_call(
        functools.partial(_layer_kernel, layer=layer, bm=bm, pos_base=pos_base,
                          with_pool=pool is not None, with_ffn=ffn is not None,
                          in_batch_major=in_batch_major, out_batch_major=out_batch_major,
                          state_batch_major=state_batch_major, cast_next=cast_next is not None,
                          pool_zero=pool is not None and pool[0] is None,
                          ffn_zero=ffn is not None and ffn[0] is None),
        grid=(nrows // t,),
        in_specs=in_specs,
        out_specs=out_specs,
        out_shape=out_shape,
        scratch_shapes=scratch,
        compiler_params=pltpu.CompilerParams(
            dimension_semantics=("arbitrary",), vmem_limit_bytes=VMEM_LIMIT_BYTES),
        name="_".join(n for n, on in (("pool", pool), ("ffn", ffn)) if on is not None) + "_layer",
    )(*args)


def _ffn_stream_kernel(x_ref, mod_ref, cvg0_ref, cvv0_ref, fpre_ref, fpost_ref, wup_ref, cwg_ref, cwv_ref,
                       cbg_ref, cbv_ref, wdn_ref, y_ref, cvg_ref, cvv_ref, f_ref, acc_ref, *, layer, bm):
    c = pl.program_id(0)
    d = mod_ref.shape[1] // 6
    lrow = slice(layer, layer + 1)
    t = x_ref.shape[0]
    hc = wdn_ref.shape[0]
    shift, scale, gate = _mod3(mod_ref, 1, d)

    @pl.when(c == 0)
    def _():
        f_ref[...] = _pre_mod(x_ref[...], fpre_ref[lrow, :], scale, shift, bm).astype(_BF16)
        acc_ref[...] = jnp.zeros_like(acc_ref)

    def conv_cols(up, cv0_ref, cv_ref, cw_ref, cb_ref):
        ext = jnp.concatenate([cv0_ref[...], up], axis=0)
        cv_ref[...] = ext[t:, :]
        conv = cb_ref[lrow, :]
        for k in range(CONV_W):
            conv = conv + ext[k * bm:k * bm + t, :] * cw_ref[k:k + 1, :]
        return conv

    up2 = _dot(f_ref[...], wup_ref[...])
    gate_c = conv_cols(up2[:, :hc], cvg0_ref, cvg_ref, cwg_ref, cbg_ref)
    val_c = conv_cols(up2[:, hc:], cvv0_ref, cvv_ref, cwv_ref, cbv_ref)
    h = (_gelu(gate_c) * val_c).astype(_BF16)
    acc_ref[...] += _dot(h, wdn_ref[...].astype(_BF16))

    @pl.when(c == pl.num_programs(0) - 1)
    def _():
        y_ref[...] = _gated_residual(x_ref[...], acc_ref[...], fpost_ref[lrow, :], gate, bm)


def _ffn_stream_layer(x, mod, gains, cv0, w_up, conv_w, conv_b, w_down, *, layer, bm):
    _, _, ffn_pre_g, ffn_post_g = gains
    t, d = x.shape
    hid = w_down.shape[1]
    hc = V7X_MXU_DIM
    nch = hid // hc
    depth = conv_b.shape[0]
    cvr = (CONV_W - 1) * bm
    assert hid % hc == 0 and cv0.shape == (cvr, 2 * hid)
    gate_cols = lambda rows: pl.BlockSpec((rows, hc), lambda c: (0, c))
    val_cols = lambda rows: pl.BlockSpec((rows, hc), lambda c: (0, nch + c))
    return pl.pallas_call(
        functools.partial(_ffn_stream_kernel, layer=layer, bm=bm),
        grid=(nch,),
        in_specs=[_whole(x.shape), _layer_slab(mod.shape, layer), gate_cols(cvr), val_cols(cvr),
                  _whole(ffn_pre_g.shape), _whole(ffn_post_g.shape),
                  pl.BlockSpec((d, 2 * hc), lambda c: (0, c)),
                  pl.BlockSpec((None, CONV_W, hc), lambda c: (layer, 0, c)),
                  pl.BlockSpec((None, CONV_W, hc), lambda c: (layer, 0, nch + c)),
                  gate_cols(depth), val_cols(depth),
                  pl.BlockSpec((None, hc, d), lambda c: (layer, c, 0))],
        out_specs=[_whole(x.shape), gate_cols(cvr), gate_cols(cvr)],
        out_shape=[jax.ShapeDtypeStruct((t, d), _F32), jax.ShapeDtypeStruct((cvr, hid), _F32),
                   jax.ShapeDtypeStruct((cvr, hid), _F32)],
        scratch_shapes=[pltpu.VMEM((t, d), _BF16), pltpu.VMEM((t, d), _F32)],
        compiler_params=pltpu.CompilerParams(
            dimension_semantics=("arbitrary",), vmem_limit_bytes=VMEM_LIMIT_BYTES),
        name="ffn_stream_layer",
    )(x, mod, cv0, cv0, ffn_pre_g, ffn_post_g, w_up, conv_w, conv_w, conv_b, conv_b, w_down)


def _s5_prep_kernel(are_ref, aim_ref, ldt_ref, btr_ref, bti_ref, ctr_ref, cti_ref,
                    lbr_ref, lbi_ref, wb_ref, wc_ref):
    a_re, a_im = are_ref[...], aim_ref[...]
    dt = jnp.exp(ldt_ref[...])
    mag = jnp.exp(a_re * dt)
    ang = a_im * dt
    lb_re = mag * jnp.cos(ang)
    lb_im = mag * jnp.sin(ang)
    n_re = lb_re - 1.0
    n_im = lb_im
    den = a_re * a_re + a_im * a_im
    f_re = ((n_re * a_re + n_im * a_im) / den)[:, None, :]
    f_im = ((n_im * a_re - n_re * a_im) / den)[:, None, :]
    lbr_ref[...] = lb_re
    lbi_ref[...] = lb_im
    b_re, b_im = btr_ref[...], bti_ref[...]
    bb_re = (f_re * b_re - f_im * b_im).astype(wb_ref.dtype)
    bb_im = (f_re * b_im + f_im * b_re).astype(wb_ref.dtype)
    c_re = ctr_ref[...].astype(wc_ref.dtype)
    nc_im = (-cti_ref[...]).astype(wc_ref.dtype)
    wb_ref[...] = jnp.zeros(wb_ref.shape, wb_ref.dtype)
    wc_ref[...] = jnp.zeros(wc_ref.shape, wc_ref.dtype)
    g, gc, p = b_re.shape
    gpt = wb_ref.shape[1] // gc
    half = gpt * p
    for gi in range(g):
        j, gl = divmod(gi, gpt)
        rows, cols = slice(gl * gc, (gl + 1) * gc), slice(gl * p, (gl + 1) * p)
        icols = slice(half + gl * p, half + (gl + 1) * p)
        wb_ref[j, rows, cols] = bb_re[gi]
        wb_ref[j, rows, icols] = bb_im[gi]
        wc_ref[j, cols, rows] = c_re[gi]
        wc_ref[j, icols, rows] = nc_im[gi]


def _s5_prep(a_re, a_im, log_dt, b_re, b_im, c_re, c_im):
    g, p = a_re.shape
    gc = b_re.shape[-1]
    gpt = V7X_MXU_DIM // gc
    tr = lambda a: jnp.swapaxes(a, 1, 2)
    return pl.pallas_call(
        _s5_prep_kernel,
        out_shape=[
            jax.ShapeDtypeStruct((g, p), _F32),
            jax.ShapeDtypeStruct((g, p), _F32),
            jax.ShapeDtypeStruct((g // gpt, gpt * gc, 2 * gpt * p), _BF16),
            jax.ShapeDtypeStruct((g // gpt, 2 * gpt * p, gpt * gc), _BF16),
        ],
        compiler_params=pltpu.CompilerParams(vmem_limit_bytes=VMEM_LIMIT_BYTES),
        name="s5_prep",
    )(a_re, a_im, log_dt.reshape(g, 1), tr(b_re), tr(b_im), tr(c_re), tr(c_im))


def _s5_kernel(*refs, layer, bm, zero_state):
    refs = list(refs)
    x_ref, mod_ref = refs[:2]
    del refs[:2]
    if not zero_state:
        xr0_ref, xi0_ref = refs[:2]
        del refs[:2]
    gpre_ref, gpost_ref, wb_ref, wc_ref, lbr_ref, lbi_ref, dsk_ref, ga_ref, gb_ref, y_ref, xr_ref, xi_ref = refs
    i = pl.program_id(0)
    t, d = x_ref.shape
    tt = t // bm
    ntile, kin, ncol2 = wb_ref.shape
    ncol = ncol2 // 2
    sub = V7X_SUBLANES
    lrow = slice(layer, layer + 1)

    @pl.when(i == 0)
    def _():
        xr_ref[...] = jnp.zeros_like(xr_ref) if zero_state else xr0_ref[...]
        xi_ref[...] = jnp.zeros_like(xi_ref) if zero_state else xi0_ref[...]

    x = x_ref[...]
    shift, scale, gate = _mod3(mod_ref, 0, d)
    h = _pre_mod(x, gpre_ref[lrow, :], scale, shift, bm)
    u = h.astype(_BF16)

    ys = []
    for j in range(ntile):
        scol = slice(j * ncol, (j + 1) * ncol)
        bu = _dot(u[:, j * kin:(j + 1) * kin], wb_ref[j])
        lr = jnp.broadcast_to(lbr_ref[:, scol], (sub, ncol))
        li = jnp.broadcast_to(lbi_ref[:, scol], (sub, ncol))
        blocks = [None] * (t // sub)
        for rb in range(bm // sub):
            rows = slice(rb * sub, (rb + 1) * sub)
            pr, pi = xr_ref[rows, scol], xi_ref[rows, scol]
            for ts in range(tt):
                r0 = ts * bm + rb * sub
                nr = lr * pr - li * pi + bu[r0:r0 + sub, :ncol]
                ni = lr * pi + li * pr + bu[r0:r0 + sub, ncol:]
                blocks[r0 // sub] = jnp.concatenate([nr, ni], axis=1)
                pr, pi = nr, ni
            xr_ref[rows, scol] = pr
            xi_ref[rows, scol] = pi
        ys.append(_dot(jnp.concatenate(blocks, axis=0).astype(_BF16), wc_ref[j]))
    y = jnp.concatenate(ys, axis=1) + dsk_ref[...] * h
    g = _gelu(y).astype(_BF16)
    out = _dot(g, ga_ref[...].astype(_BF16)) * jax.nn.sigmoid(_dot(g, gb_ref[...].astype(_BF16)))
    y_ref[...] = _gated_residual(x, out, gpost_ref[lrow, :], gate, bm)


def _s5_layer(x, mod, state0, g_pre, g_post, wb, wc, lbr, lbi, dskip, glu_a, glu_b, *, layer, bm, tt):
    r, d = x.shape
    t = tt * bm
    nst = lbr.shape[1]
    state_args = [] if state0 is None else list(state0)
    return pl.pallas_call(
        functools.partial(_s5_kernel, layer=layer, bm=bm, zero_state=state0 is None),
        grid=(r // t,),
        in_specs=[
            pl.BlockSpec((t, d), lambda i: (i, 0)),
            _layer_slab(mod.shape, layer),
            *[_whole((bm, nst)) for _ in state_args],
            _whole(g_pre.shape),
            _whole(g_post.shape),
            _whole(wb.shape),
            _whole(wc.shape),
            _whole((1, nst)),
            _whole((1, nst)),
            _whole((1, d)),
            _whole(glu_a.shape),
            _whole(glu_b.shape),
        ],
        out_specs=[
            pl.BlockSpec((t, d), lambda i: (i, 0)),
            _whole((bm, nst)),
            _whole((bm, nst)),
        ],
        out_shape=[
            jax.ShapeDtypeStruct((r, d), _F32),
            jax.ShapeDtypeStruct((bm, nst), _F32),
            jax.ShapeDtypeStruct((bm, nst), _F32),
        ],
        compiler_params=pltpu.CompilerParams(
            dimension_semantics=("arbitrary",), vmem_limit_bytes=VMEM_LIMIT_BYTES),
        name="s5_layer",
    )(x, mod, *state_args, g_pre, g_post, wb, wc, lbr, lbi, dskip, glu_a, glu_b)


def _time_major(a):
    b, l, c = a.shape
    return jnp.swapaxes(a, 0, 1).reshape(l * b, c)


def _batch_major(a, b):
    lb, c = a.shape
    return jnp.swapaxes(a.reshape(lb // b, b, c), 0, 1)


def _stacked_batch_major(states, b):
    st = jnp.stack(states)
    n, tb, c = st.shape
    return jnp.swapaxes(st.reshape(n, tb // b, b, c), 1, 2)


def kernel(x_prompt, x_sample, c_prompt, c_sample, state_pool, state_ssm_re, state_ssm_im, state_ffn_conv, ada_w, ada_b, mix_pre_g, mix_post_g, ffn_pre_g, ffn_post_g, pool_w, pool_scale, ssm_A_re, ssm_A_im, ssm_log_dt, ssm_B_re, ssm_B_im, ssm_C_re, ssm_C_im, ssm_D, ssm_glu_a, ssm_glu_b, ffn_w_up, ffn_conv_w, ffn_conv_b, ffn_w_down):
    depth = ada_w.shape[0]
    bp, lp, d = x_prompt.shape
    bs, ls, _ = x_sample.shape
    groups = ssm_A_re.shape[1]
    nst = groups * SSM_P

    mod_p, mod_s = _ada_mod(c_prompt, c_sample, ada_w, ada_b)
    gains = (mix_pre_g, mix_post_g, ffn_pre_g, ffn_post_g)
    w_dn = ffn_w_down
    w_up_next = _slab_to_bf16(ffn_w_up, 0, CAST_TILE_ROWS)

    geo = {
        "p": dict(bm=bp, pool_ffn_tt=PROMPT_POOL_FFN_TT, ffn_tt=PROMPT_FFN_TT, s5_tt=PROMPT_S5_TT, pos_base=0,
                  fuse_pool=True, stream_ffn=False),
        "s": dict(bm=bs, pool_ffn_tt=ls // 2, ffn_tt=ls // 2, s5_tt=ls // 2, pos_base=PAST_LEN, fuse_pool=False,
                  pool_tt=ls, stream_ffn=True),
    }
    ys = {"p": x_prompt, "s": x_sample}
    mods = {"p": mod_p, "s": mod_s}
    pool_out = {"p": [], "s": []}
    ssm_out = {"p": [], "s": []}
    conv_out = {"p": [], "s": []}

    for l in range(depth):
        j = l // 2
        w_up = w_up_next
        pool = {"p": None, "s": None}
        if l % 2 == 0:
            pw = pool_w[j]
            ps = pool_scale[j].reshape(1, d)
            pool["p"] = (None, None, pw, ps)
            pool["s"] = (state_pool, j, pw, ps)
        else:
            lb_re, lb_im, wb, wc = _s5_prep(ssm_A_re[j], ssm_A_im[j], ssm_log_dt[j], ssm_B_re[j], ssm_B_im[j],
                                            ssm_C_re[j], ssm_C_im[j])
            lbr = lb_re.reshape(1, nst)
            lbi = lb_im.reshape(1, nst)
            ga = ssm_glu_a[j]
            gb = ssm_glu_b[j]
            for k in ("p", "s"):
                gk = geo[k]
                state0 = None if k == "p" else (state_ssm_re[j].reshape(bs, nst), state_ssm_im[j].reshape(bs, nst))
                ys[k], xr, xi = _s5_layer(ys[k], mods[k], state0, mix_pre_g, mix_post_g,
                                          wb, wc, lbr, lbi, ssm_D[j].reshape(1, d), ga, gb,
                                          layer=l, bm=gk["bm"], tt=gk["s5_tt"])
                ssm_out[k].append((xr.reshape(gk["bm"], groups, SSM_P), xi.reshape(gk["bm"], groups, SSM_P)))

        for k in ("p", "s"):
            gk = geo[k]
            cv0 = None if k == "p" else _time_major(state_ffn_conv[l])
            ffn = (cv0, w_up, ffn_conv_w, ffn_conv_b, w_dn)
            cast_next = (ffn_w_up, l + 1) if k == "p" and l + 1 < depth else None
            common = dict(layer=l, bm=gk["bm"], pos_base=gk["pos_base"])
            if pool[k] is not None and not gk["fuse_pool"]:
                ys[k], st = _token_layer(ys[k], mods[k], gains, tt=gk["pool_tt"], pool=pool[k],
                                         in_batch_major=True, state_batch_major=True, **common)
                pool_out[k].append(st)
                pool[k] = None
            if gk["stream_ffn"]:
                ys[k], cvg, cvv = _ffn_stream_layer(ys[k], mods[k], gains, cv0, w_up, ffn_conv_w, ffn_conv_b, w_dn,
                                                    layer=l, bm=gk["bm"])
                conv_out[k].append(jnp.concatenate([cvg, cvv], axis=1))
                continue
            tt = gk["ffn_tt"] if pool[k] is None else gk["pool_ffn_tt"]
            outs = list(_token_layer(ys[k], mods[k], gains, tt=tt, pool=pool[k], ffn=ffn, cast_next=cast_next,
                                in_batch_major=(l == 0 and pool[k] is not None),
                                out_batch_major=(k == "p" and l == depth - 1), **common))
            if cast_next is not None:
                w_up_next = outs.pop()
            ys[k], cv = outs[0], outs[-1]
            if pool[k] is not None:
                pool_out[k].append(_batch_major(outs[1], gk["bm"]))
            conv_out[k].append(cv)

    y_prompt = ys["p"]
    y_sample = _batch_major(ys["s"], bs)
    return (y_prompt, y_sample,
            jnp.stack(pool_out["p"]), jnp.stack(pool_out["s"]),
            jnp.stack([a for a, _ in ssm_out["p"]]), jnp.stack([b for _, b in ssm_out["p"]]),
            jnp.stack([a for a, _ in ssm_out["s"]]), jnp.stack([b for _, b in ssm_out["s"]]),
            _stacked_batch_major(conv_out["p"], bp), _stacked_batch_major(conv_out["s"], bs))
```

```python
import functools
import math

import jax
import jax.numpy as jnp
from jax import lax
from jax.experimental import pallas as pl
from jax.experimental.pallas import tpu as pltpu

POOL_WINDOWS = (2, 4, 8, 16)
POOL_STATE = max(POOL_WINDOWS) - 1
SSM_GC = 16
SSM_P = 64
CONV_W = 3
EPS = 1e-6
PAST_LEN = 16384

V7X_SUBLANES = 8
V7X_LANES = 128
V7X_MXU_DIM = 256
VMEM_LIMIT_BYTES = 56 * 1024 * 1024

PROMPT_POOL_FFN_TT = 64
PROMPT_FFN_TT = 128
PROMPT_S5_TT = 128
ADA_TILE_N = 1536
CAST_TILE_ROWS = 256

_F32 = jnp.float32
_BF16 = jnp.bfloat16


def _resident(block_shape, index_map):
    return pl.BlockSpec(block_shape, index_map, pipeline_mode=pl.Buffered(1))


def _whole(shape):
    nd = len(shape)
    return _resident(shape, lambda i: (0,) * nd)


def _layer_slab(shape, l):
    nd = len(shape)
    return _resident((None,) + tuple(shape[1:]), lambda i: (l,) + (0,) * (nd - 1))


def _unit_rms(x):
    return x * lax.rsqrt(jnp.mean(x * x, axis=-1, keepdims=True) + EPS)


def _pre_mod(x, g, scale, shift, bm):
    t, d = x.shape
    r = _unit_rms(x).reshape(t // bm, bm, d)
    return (r * (g * (1.0 + scale))[None] + shift[None]).reshape(t, d)


def _gated_residual(x, m, g, gate, bm):
    t, d = x.shape
    r = _unit_rms(m).reshape(t // bm, bm, d)
    return x + (r * (gate * g)[None]).reshape(t, d)


def _gelu(x):
    return 0.5 * x * (1.0 + lax.erf(x * math.sqrt(0.5)))


def _dot(a, b):
    return jnp.dot(a, b, preferred_element_type=_F32)


def _load_rows(x_ref, batch_major):
    if not batch_major:
        return x_ref[...]
    b, tt, d = x_ref.shape
    return jnp.swapaxes(x_ref[...], 0, 1).reshape(tt * b, d)


def _store_rows(y_ref, y, batch_major):
    if not batch_major:
        y_ref[...] = y
    else:
        b, tt, d = y_ref.shape
        y_ref[...] = jnp.swapaxes(y.reshape(tt, b, d), 0, 1)


def _mod3(mod_ref, k, d):
    return tuple(mod_ref[:, (3 * k + n) * d:(3 * k + n + 1) * d] for n in range(3))


def _paired_cols(gate, val):
    hc = V7X_MXU_DIM
    parts = []
    for c in range(gate.shape[1] // hc):
        parts += [gate[:, c * hc:(c + 1) * hc], val[:, c * hc:(c + 1) * hc]]
    return jnp.concatenate(parts, axis=1)


def _cast_kernel(wa_ref, wb_ref, o_ref):
    o_ref[...] = _paired_cols(wa_ref[...], wb_ref[...]).astype(o_ref.dtype)


def _slab_to_bf16(w, slab, rows):
    _, r, c = w.shape
    return pl.pallas_call(
        _cast_kernel,
        grid=(r // rows,),
        in_specs=[pl.BlockSpec((None, rows, c // 2), lambda i: (slab, i, 0)),
                  pl.BlockSpec((None, rows, c // 2), lambda i: (slab, i, 1))],
        out_specs=pl.BlockSpec((rows, c), lambda i: (i, 0)),
        out_shape=jax.ShapeDtypeStruct((r, c), _BF16),
        compiler_params=pltpu.CompilerParams(
            dimension_semantics=("arbitrary",), vmem_limit_bytes=VMEM_LIMIT_BYTES),
        name="to_bf16",
    )(w, w)


def _ada_kernel(cp_ref, cs_ref, wa_ref, wb_ref, b_ref, mp_ref, ms_ref):
    bp = cp_ref.shape[0]
    kh = wa_ref.shape[0]
    c = jnp.concatenate([cp_ref[...], cs_ref[...]], axis=0)
    s = (c * jax.nn.sigmoid(c)).astype(_BF16)
    o = _dot(s[:, :kh], wa_ref[...].astype(_BF16)) + _dot(s[:, kh:], wb_ref[...].astype(_BF16)) + b_ref[...]
    mp_ref[...] = o[:bp]
    ms_ref[...] = o[bp:]


def _ada_mod(c_prompt, c_sample, ada_w, ada_b):
    depth, d, n = ada_w.shape
    tn = ADA_TILE_N
    bp, bs = c_prompt.shape[0], c_sample.shape[0]
    return pl.pallas_call(
        _ada_kernel,
        grid=(depth, n // tn),
        in_specs=[
            pl.BlockSpec((bp, d), lambda l, j: (0, 0)),
            pl.BlockSpec((bs, d), lambda l, j: (0, 0)),
            pl.BlockSpec((None, d // 2, tn), lambda l, j: (l, 0, j)),
            pl.BlockSpec((None, d // 2, tn), lambda l, j: (l, 1, j)),
            pl.BlockSpec((None, 1, tn), lambda l, j: (l, 0, j)),
        ],
        out_specs=[
            pl.BlockSpec((None, bp, tn), lambda l, j: (l, 0, j)),
            pl.BlockSpec((None, bs, tn), lambda l, j: (l, 0, j)),
        ],
        out_shape=[
            jax.ShapeDtypeStruct((depth, bp, n), _F32),
            jax.ShapeDtypeStruct((depth, bs, n), _F32),
        ],
        compiler_params=pltpu.CompilerParams(
            dimension_semantics=("arbitrary", "arbitrary"),
            vmem_limit_bytes=VMEM_LIMIT_BYTES),
        name="ada_mod",
    )(c_prompt, c_sample, ada_w, ada_w, ada_b.reshape(depth, 1, n))


def _pool_mix(h, st, pw_ref, ps_ref, i, *, bm, pos_base):
    t, d = h.shape
    tt = t // bm
    gc = d // len(POOL_WINDOWS)
    ext = jnp.concatenate([st, h], axis=0)
    pos = None
    head = min(t, max(POOL_WINDOWS) * bm)
    if pos_base + 1 < max(POOL_WINDOWS):
        row = lax.broadcasted_iota(jnp.int32, (head, V7X_LANES), 0)
        pos = pos_base + i * tt + lax.shift_right_logical(row, bm.bit_length() - 1)
    parts = []
    for gi, w in enumerate(POOL_WINDOWS):
        cols = slice(gi * gc, (gi + 1) * gc)
        s = ext[:, cols]
        span = 1
        while span < w:
            n = s.shape[0]
            s = s[span * bm:] + s[:n - span * bm]
            span *= 2
        k0 = (POOL_STATE - (w - 1)) * bm
        wsum = s[k0:k0 + t]
        mean = wsum * (1.0 / w)
        if pos is not None:
            inv = 1.0 / jnp.minimum(pos + 1, w).astype(_F32)
            inv = jnp.concatenate([inv] * (gc // V7X_LANES), axis=1)
            mean = jnp.concatenate([wsum[:head] * inv, mean[head:]], axis=0)
        pooled = mean - h[:, cols]
        parts.append(_dot(pooled.astype(_BF16), pw_ref[gi].astype(_BF16)))
    return jnp.concatenate(parts, axis=1) * ps_ref[...], ext[t:, :]


def _conv_ffn(f, cv_ref, wup_ref, cw_ref, cb_ref, wdn_ref, h_ref, *, bm):
    t = f.shape[0]
    hid = wdn_ref.shape[0]
    hc = V7X_MXU_DIM

    def conv_cols(up, c0):
        cols = slice(c0, c0 + hc)
        ext = jnp.concatenate([cv_ref[:, cols], up], axis=0)
        cv_ref[:, cols] = ext[t:, :]
        conv = cb_ref[:, cols]
        for k in range(CONV_W):
            conv = conv + ext[k * bm:k * bm + t, :] * cw_ref[k:k + 1, cols]
        return conv

    for c in range(hid // hc):
        up2 = _dot(f, wup_ref[:, 2 * c * hc:2 * (c + 1) * hc])
        gate_c = conv_cols(up2[:, :hc], c * hc)
        val_c = conv_cols(up2[:, hc:], hid + c * hc)
        h_c = (_gelu(gate_c) * val_c).astype(_BF16)
        o_c = _dot(h_c, wdn_ref[c * hc:(c + 1) * hc, :].astype(_BF16))
        acc = o_c if c == 0 else acc + o_c
    return acc


def _layer_kernel(*refs, layer, bm, pos_base, with_pool, with_ffn, in_batch_major, out_batch_major,
                  state_batch_major, cast_next, pool_zero, ffn_zero):
    refs = list(refs)
    x_ref, mod_ref = refs[:2]
    del refs[:2]
    if with_pool:
        st0_ref = None if pool_zero else refs.pop(0)
        mpre_ref, mpost_ref, pw_ref, ps_ref = refs[:4]
        del refs[:4]
    if with_ffn:
        cv0_ref = None if ffn_zero else refs.pop(0)
        fpre_ref, fpost_ref, wup_ref, cw_ref, cb_ref, wdn_ref = refs[:6]
        del refs[:6]
        if cast_next:
            wnext_ref = refs.pop(0)
    y_ref = refs.pop(0)
    if with_pool:
        st_ref = refs.pop(0)
    if with_ffn:
        cv_ref = refs.pop(0)
        if cast_next:
            wnext_out_ref = refs.pop(0)
            nh = wnext_ref.shape[1] // 2
            wnext_out_ref[...] = _paired_cols(wnext_ref[:, :nh], wnext_ref[:, nh:]).astype(wnext_out_ref.dtype)
        (h_ref,) = refs
    i = pl.program_id(0)
    d = mod_ref.shape[1] // 6
    lrow = slice(layer, layer + 1)

    @pl.when(i == 0)
    def _():
        if with_pool and not state_batch_major:
            st_ref[...] = jnp.zeros_like(st_ref) if pool_zero else st0_ref[...]
        if with_ffn:
            cv_ref[...] = jnp.zeros_like(cv_ref) if ffn_zero else cv0_ref[...]

    x = _load_rows(x_ref, in_batch_major)
    if with_pool:
        shift, scale, gate = _mod3(mod_ref, 0, d)
        h = _pre_mod(x, mpre_ref[lrow, :], scale, shift, bm)
        st = _load_rows(st0_ref, True) if state_batch_major else st_ref[...]
        m, st = _pool_mix(h, st, pw_ref, ps_ref, i, bm=bm, pos_base=pos_base)
        _store_rows(st_ref, st, state_batch_major)
        x = _gated_residual(x, m, mpost_ref[lrow, :], gate, bm)
    if with_ffn:
        shift, scale, gate = _mod3(mod_ref, 1, d)
        f = _pre_mod(x, fpre_ref[lrow, :], scale, shift, bm).astype(_BF16)
        o = _conv_ffn(f, cv_ref, wup_ref, cw_ref, cb_ref.at[lrow, :], wdn_ref, h_ref, bm=bm)
        x = _gated_residual(x, o, fpost_ref[lrow, :], gate, bm)
    _store_rows(y_ref, x, out_batch_major)


def _token_layer(x, mod, gains, *, layer, bm, tt, pos_base=0, pool=None, ffn=None, cast_next=None,
                 in_batch_major=False, out_batch_major=False, state_batch_major=False):
    mix_pre_g, mix_post_g, ffn_pre_g, ffn_post_g = gains
    d = x.shape[-1]
    nrows = x.shape[0] * x.shape[1] if in_batch_major else x.shape[0]
    t = tt * bm
    assert not state_batch_major or nrows == t
    tm_spec = pl.BlockSpec((t, d), lambda i: (i, 0))
    bm_spec = pl.BlockSpec((bm, tt, d), lambda i: (0, i, 0))

    args = [x, mod]
    in_specs = [bm_spec if in_batch_major else tm_spec, _layer_slab(mod.shape, layer)]
    out_specs = [bm_spec if out_batch_major else tm_spec]
    out_shape = [jax.ShapeDtypeStruct((bm, nrows // bm, d) if out_batch_major else (nrows, d), _F32)]
    scratch = []
    if pool is not None:
        st0, st_slab, pool_w, pool_scale = pool
        if st0 is None:
            st_shape = (POOL_STATE * bm, d)
        else:
            st_shape = st0.shape if st_slab is None else st0.shape[1:]
            args += [st0]
            in_specs += [_whole(st0.shape) if st_slab is None else _layer_slab(st0.shape, st_slab)]
        args += [mix_pre_g, mix_post_g, pool_w, pool_scale]
        in_specs += [_whole(a.shape) for a in args[-4:]]
        out_specs += [_whole(st_shape)]
        out_shape += [jax.ShapeDtypeStruct(st_shape, _F32)]
    if ffn is not None:
        cv0, w_up, conv_w, conv_b, w_down = ffn
        hid = w_down.shape[1]
        cv_shape = ((CONV_W - 1) * bm, 2 * hid)
        assert hid % V7X_MXU_DIM == 0 and (cv0 is None or cv0.shape == cv_shape)
        if cv0 is not None:
            args += [cv0]
            in_specs += [_whole(cv_shape)]
        args += [ffn_pre_g, ffn_post_g, w_up, conv_w, conv_b, w_down]
        in_specs += [_whole(ffn_pre_g.shape), _whole(ffn_post_g.shape), _whole(w_up.shape),
                     _layer_slab(conv_w.shape, layer), _whole(conv_b.shape), _layer_slab(w_down.shape, layer)]
        out_specs += [_whole(cv_shape)]
        out_shape += [jax.ShapeDtypeStruct(cv_shape, _F32)]
        if cast_next is not None:
            w_stack, slab = cast_next
            _, r_w, c_w = w_stack.shape
            rows = r_w // (nrows // t)
            assert rows * (nrows // t) == r_w and rows % (2 * V7X_SUBLANES) == 0
            args += [w_stack]
            in_specs += [pl.BlockSpec((None, rows, c_w), lambda i: (slab, i, 0))]
            out_specs += [pl.BlockSpec((rows, c_w), lambda i: (i, 0))]
            out_shape += [jax.ShapeDtypeStruct((r_w, c_w), _BF16)]
        scratch += [pltpu.VMEM((t, hid), _BF16)]
    return pl.pallas_call(
        functools.partial(_layer_kernel, layer=layer, bm=bm, pos_base=pos_base,
                          with_pool=pool is not None, with_ffn=ffn is not None,
                          in_batch_major=in_batch_major, out_batch_major=out_batch_major,
                          state_batch_major=state_batch_major, cast_next=cast_next is not None,
                          pool_zero=pool is not None and pool[0] is None,
                          ffn_zero=ffn is not None and ffn[0] is None),
        grid=(nrows // t,),
        in_specs=in_specs,
        out_specs=out_specs,
        out_shape=out_shape,
        scratch_shapes=scratch,
        compiler_params=pltpu.CompilerParams(
            dimension_semantics=("arbitrary",), vmem_limit_bytes=VMEM_LIMIT_BYTES),
        name="_".join(n for n, on in (("pool", pool), ("ffn", ffn)) if on is not None) + "_layer",
    )(*args)


def _s5_prep_kernel(are_ref, aim_ref, ldt_ref, btr_ref, bti_ref, ctr_ref, cti_ref,
                    lbr_ref, lbi_ref, wb_ref, wc_ref):
    a_re, a_im = are_ref[...], aim_ref[...]
    dt = jnp.exp(ldt_ref[...])
    mag = jnp.exp(a_re * dt)
    ang = a_im * dt
    lb_re = mag * jnp.cos(ang)
    lb_im = mag * jnp.sin(ang)
    n_re = lb_re - 1.0
    n_im = lb_im
    den = a_re * a_re + a_im * a_im
    f_re = ((n_re * a_re + n_im * a_im) / den)[:, None, :]
    f_im = ((n_im * a_re - n_re * a_im) / den)[:, None, :]
    lbr_ref[...] = lb_re
    lbi_ref[...] = lb_im
    b_re, b_im = btr_ref[...], bti_ref[...]
    bb_re = (f_re * b_re - f_im * b_im).astype(wb_ref.dtype)
    bb_im = (f_re * b_im + f_im * b_re).astype(wb_ref.dtype)
    c_re = ctr_ref[...].astype(wc_ref.dtype)
    nc_im = (-cti_ref[...]).astype(wc_ref.dtype)
    wb_ref[...] = jnp.zeros(wb_ref.shape, wb_ref.dtype)
    wc_ref[...] = jnp.zeros(wc_ref.shape, wc_ref.dtype)
    g, gc, p = b_re.shape
    gpt = wb_ref.shape[1] // gc
    half = gpt * p
    for gi in range(g):
        j, gl = divmod(gi, gpt)
        rows, cols = slice(gl * gc, (gl + 1) * gc), slice(gl * p, (gl + 1) * p)
        icols = slice(half + gl * p, half + (gl + 1) * p)
        wb_ref[j, rows, cols] = bb_re[gi]
        wb_ref[j, rows, icols] = bb_im[gi]
        wc_ref[j, cols, rows] = c_re[gi]
        wc_ref[j, icols, rows] = nc_im[gi]


def _s5_prep(a_re, a_im, log_dt, b_re, b_im, c_re, c_im):
    g, p = a_re.shape
    gc = b_re.shape[-1]
    gpt = V7X_MXU_DIM // gc
    tr = lambda a: jnp.swapaxes(a, 1, 2)
    return pl.pallas_call(
        _s5_prep_kernel,
        out_shape=[
            jax.ShapeDtypeStruct((g, p), _F32),
            jax.ShapeDtypeStruct((g, p), _F32),
            jax.ShapeDtypeStruct((g // gpt, gpt * gc, 2 * gpt * p), _BF16),
            jax.ShapeDtypeStruct((g // gpt, 2 * gpt * p, gpt * gc), _BF16),
        ],
        compiler_params=pltpu.CompilerParams(vmem_limit_bytes=VMEM_LIMIT_BYTES),
        name="s5_prep",
    )(a_re, a_im, log_dt.reshape(g, 1), tr(b_re), tr(b_im), tr(c_re), tr(c_im))


def _s5_kernel(*refs, layer, bm, zero_state):
    refs = list(refs)
    x_ref, mod_ref = refs[:2]
    del refs[:2]
    if not zero_state:
        xr0_ref, xi0_ref = refs[:2]
        del refs[:2]
    gpre_ref, gpost_ref, wb_ref, wc_ref, lbr_ref, lbi_ref, dsk_ref, ga_ref, gb_ref, y_ref, xr_ref, xi_ref = refs
    i = pl.program_id(0)
    t, d = x_ref.shape
    tt = t // bm
    ntile, kin, ncol2 = wb_ref.shape
    ncol = ncol2 // 2
    sub = V7X_SUBLANES
    lrow = slice(layer, layer + 1)

    @pl.when(i == 0)
    def _():
        xr_ref[...] = jnp.zeros_like(xr_ref) if zero_state else xr0_ref[...]
        xi_ref[...] = jnp.zeros_like(xi_ref) if zero_state else xi0_ref[...]

    x = x_ref[...]
    shift, scale, gate = _mod3(mod_ref, 0, d)
    h = _pre_mod(x, gpre_ref[lrow, :], scale, shift, bm)
    u = h.astype(_BF16)

    ys = []
    for j in range(ntile):
        scol = slice(j * ncol, (j + 1) * ncol)
        bu = _dot(u[:, j * kin:(j + 1) * kin], wb_ref[j])
        lr = jnp.broadcast_to(lbr_ref[:, scol], (sub, ncol))
        li = jnp.broadcast_to(lbi_ref[:, scol], (sub, ncol))
        blocks = [None] * (t // sub)
        for rb in range(bm // sub):
            rows = slice(rb * sub, (rb + 1) * sub)
            pr, pi = xr_ref[rows, scol], xi_ref[rows, scol]
            for ts in range(tt):
                r0 = ts * bm + rb * sub
                nr = lr * pr - li * pi + bu[r0:r0 + sub, :ncol]
                ni = lr * pi + li * pr + bu[r0:r0 + sub, ncol:]
                blocks[r0 // sub] = jnp.concatenate([nr, ni], axis=1)
                pr, pi = nr, ni
            xr_ref[rows, scol] = pr
            xi_ref[rows, scol] = pi
        ys.append(_dot(jnp.concatenate(blocks, axis=0).astype(_BF16), wc_ref[j]))
    y = jnp.concatenate(ys, axis=1) + dsk_ref[...] * h
    g = _gelu(y).astype(_BF16)
    out = _dot(g, ga_ref[...].astype(_BF16)) * jax.nn.sigmoid(_dot(g, gb_ref[...].astype(_BF16)))
    y_ref[...] = _gated_residual(x, out, gpost_ref[lrow, :], gate, bm)


def _s5_layer(x, mod, state0, g_pre, g_post, wb, wc, lbr, lbi, dskip, glu_a, glu_b, *, layer, bm, tt):
    r, d = x.shape
    t = tt * bm
    nst = lbr.shape[1]
    state_args = [] if state0 is None else list(state0)
    return pl.pallas_call(
        functools.partial(_s5_kernel, layer=layer, bm=bm, zero_state=state0 is None),
        grid=(r // t,),
        in_specs=[
            pl.BlockSpec((t, d), lambda i: (i, 0)),
            _layer_slab(mod.shape, layer),
            *[_whole((bm, nst)) for _ in state_args],
            _whole(g_pre.shape),
            _whole(g_post.shape),
            _whole(wb.shape),
            _whole(wc.shape),
            _whole((1, nst)),
            _whole((1, nst)),
            _whole((1, d)),
            _whole(glu_a.shape),
            _whole(glu_b.shape),
        ],
        out_specs=[
            pl.BlockSpec((t, d), lambda i: (i, 0)),
            _whole((bm, nst)),
            _whole((bm, nst)),
        ],
        out_shape=[
            jax.ShapeDtypeStruct((r, d), _F32),
            jax.ShapeDtypeStruct((bm, nst), _F32),
            jax.ShapeDtypeStruct((bm, nst), _F32),
        ],
        compiler_params=pltpu.CompilerParams(
            dimension_semantics=("arbitrary",), vmem_limit_bytes=VMEM_LIMIT_BYTES),
        name="s5_layer",
    )(x, mod, *state_args, g_pre, g_post, wb, wc, lbr, lbi, dskip, glu_a, glu_b)


def _time_major(a):
    b, l, c = a.shape
    return jnp.swapaxes(a, 0, 1).reshape(l * b, c)


def _batch_major(a, b):
    lb, c = a.shape
    return jnp.swapaxes(a.reshape(lb // b, b, c), 0, 1)


def _stacked_batch_major(states, b):
    st = jnp.stack(states)
    n, tb, c = st.shape
    return jnp.swapaxes(st.reshape(n, tb // b, b, c), 1, 2)


def kernel(x_prompt, x_sample, c_prompt, c_sample, state_pool, state_ssm_re, state_ssm_im, state_ffn_conv, ada_w, ada_b, mix_pre_g, mix_post_g, ffn_pre_g, ffn_post_g, pool_w, pool_scale, ssm_A_re, ssm_A_im, ssm_log_dt, ssm_B_re, ssm_B_im, ssm_C_re, ssm_C_im, ssm_D, ssm_glu_a, ssm_glu_b, ffn_w_up, ffn_conv_w, ffn_conv_b, ffn_w_down):
    depth = ada_w.shape[0]
    bp, lp, d = x_prompt.shape
    bs, ls, _ = x_sample.shape
    groups = ssm_A_re.shape[1]
    nst = groups * SSM_P

    mod_p, mod_s = _ada_mod(c_prompt, c_sample, ada_w, ada_b)
    gains = (mix_pre_g, mix_post_g, ffn_pre_g, ffn_post_g)
    w_dn = ffn_w_down
    w_up_next = _slab_to_bf16(ffn_w_up, 0, CAST_TILE_ROWS)

    geo = {
        "p": dict(bm=bp, pool_ffn_tt=PROMPT_POOL_FFN_TT, ffn_tt=PROMPT_FFN_TT, s5_tt=PROMPT_S5_TT, pos_base=0,
                  fuse_pool=True),
        "s": dict(bm=bs, pool_ffn_tt=ls // 2, ffn_tt=ls // 2, s5_tt=ls // 2, pos_base=PAST_LEN, fuse_pool=False,
                  pool_tt=ls),
    }
    ys = {"p": x_prompt, "s": x_sample}
    mods = {"p": mod_p, "s": mod_s}
    pool_out = {"p": [], "s": []}
    ssm_out = {"p": [], "s": []}
    conv_out = {"p": [], "s": []}

    for l in range(depth):
        j = l // 2
        w_up = w_up_next
        pool = {"p": None, "s": None}
        if l % 2 == 0:
            pw = pool_w[j]
            ps = pool_scale[j].reshape(1, d)
            pool["p"] = (None, None, pw, ps)
            pool["s"] = (state_pool, j, pw, ps)
        else:
            lb_re, lb_im, wb, wc = _s5_prep(ssm_A_re[j], ssm_A_im[j], ssm_log_dt[j], ssm_B_re[j], ssm_B_im[j],
                                            ssm_C_re[j], ssm_C_im[j])
            lbr = lb_re.reshape(1, nst)
            lbi = lb_im.reshape(1, nst)
            ga = ssm_glu_a[j]
            gb = ssm_glu_b[j]
            for k in ("p", "s"):
                gk = geo[k]
                state0 = None if k == "p" else (state_ssm_re[j].reshape(bs, nst), state_ssm_im[j].reshape(bs, nst))
                ys[k], xr, xi = _s5_layer(ys[k], mods[k], state0, mix_pre_g, mix_post_g,
                                          wb, wc, lbr, lbi, ssm_D[j].reshape(1, d), ga, gb,
                                          layer=l, bm=gk["bm"], tt=gk["s5_tt"])
                ssm_out[k].append((xr.reshape(gk["bm"], groups, SSM_P), xi.reshape(gk["bm"], groups, SSM_P)))

        for k in ("p", "s"):
            gk = geo[k]
            cv0 = None if k == "p" else _time_major(state_ffn_conv[l])
            ffn = (cv0, w_up, ffn_conv_w, ffn_conv_b, w_dn)
            cast_next = (ffn_w_up, l + 1) if k == "p" and l + 1 < depth else None
            common = dict(layer=l, bm=gk["bm"], pos_base=gk["pos_base"])
            if pool[k] is not None and not gk["fuse_pool"]:
                ys[k], st = _token_layer(ys[k], mods[k], gains, tt=gk["pool_tt"], pool=pool[k],
                                         in_batch_major=True, state_batch_major=True, **common)
                pool_out[k].append(st)
                pool[k] = None
            tt = gk["ffn_tt"] if pool[k] is None else gk["pool_ffn_tt"]
            outs = list(_token_layer(ys[k], mods[k], gains, tt=tt, pool=pool[k], ffn=ffn, cast_next=cast_next,
                                in_batch_major=(l == 0 and pool[k] is not None),
                                out_batch_major=(k == "p" and l == depth - 1), **common))
            if cast_next is not None:
                w_up_next = outs.pop()
            ys[k], cv = outs[0], outs[-1]
            if pool[k] is not None:
                pool_out[k].append(_batch_major(outs[1], gk["bm"]))
            conv_out[k].append(cv)

    y_prompt = ys["p"]
    y_sample = _batch_major(ys["s"], bs)
    return (y_prompt, y_sample,
            jnp.stack(pool_out["p"]), jnp.stack(pool_out["s"]),
            jnp.stack([a for a, _ in ssm_out["p"]]), jnp.stack([b for _, b in ssm_out["p"]]),
            jnp.stack([a for a, _ in ssm_out["s"]]), jnp.stack([b for _, b in ssm_out["s"]]),
            _stacked_batch_major(conv_out["p"], bp), _stacked_batch_major(conv_out["s"], bs))
```
